```python
import functools
import jax
import jax.numpy as jnp
from jax import lax
import numpy as np

D_MODEL = 4096
BATCH = 1
SEQ = 8192
DEPTH = 2

GRID_W = 64
CTX_LEN = 256
CHUNK = 128
Q_BLOCK = 128
CONV_W = 5

D_SSD = 3 * D_MODEL // 8
D_ATT = D_MODEL // 4
D_MLS = D_MODEL - D_SSD - D_ATT
D_MIX = D_SSD + D_ATT + D_MLS

SSD_HEADDIM = 64
SSD_HEADS = D_SSD // SSD_HEADDIM
SSD_GROUPS = 4
SSD_HPG = SSD_HEADS // SSD_GROUPS
SSD_STATE = 128
SSD_CONV_CH = D_SSD + 2 * SSD_GROUPS * SSD_STATE

HEAD_DIM = 128
ATT_HEADS = D_ATT // HEAD_DIM
ATT_KV_HEADS = 2
ATT_GROUP = ATT_HEADS // ATT_KV_HEADS
ROPE_THETA = 10000.0

MLS_V_DIM = 256
MLS_QK_DIM = 128
MLS_HEADS = D_MLS // MLS_V_DIM
MLS_CONV_CH = 2 * MLS_HEADS * MLS_QK_DIM

N_EXPERTS = 32
TOP_K = 4
D_EXPERT = D_MODEL // 8
SWIGLU_LIMIT = 7.0
SWIGLU_ALPHA = 1.702

DEEPNORM_ALPHA = (2 * DEPTH) ** 0.25
DEEPNORM_BETA = (8 * DEPTH) ** -0.25

IN_SIZES = (D_SSD, SSD_CONV_CH, 2 * SSD_HEADS,
            D_ATT, ATT_KV_HEADS * HEAD_DIM, ATT_KV_HEADS * HEAD_DIM,
            MLS_CONV_CH, D_MLS, D_MLS, 4 * MLS_HEADS)
D_IN = sum(IN_SIZES)

kernel_name = 'hybrid_ssd_gqa_mlstm_moe_dit'


def rms_norm(x, g, eps=1e-6):
    xf = x.astype(jnp.float32)
    return (xf * lax.rsqrt(jnp.mean(xf * xf, axis=-1, keepdims=True) + eps) * g).astype(x.dtype)


def layer_norm(x, g, b, eps=1e-5):
    xf = x.astype(jnp.float32)
    mu = jnp.mean(xf, axis=-1, keepdims=True)
    var = jnp.mean(jnp.square(xf - mu), axis=-1, keepdims=True)
    return ((xf - mu) * lax.rsqrt(var + eps) * g + b).astype(x.dtype)


def dwconv_centred(u, w, b):
    pad = w.shape[0] // 2
    y = lax.conv_general_dilated(u, w[:, None, :].astype(u.dtype), window_strides=(1,),
                                 padding=[(pad, pad)], dimension_numbers=('NWC', 'WIO', 'NWC'),
                                 feature_group_count=u.shape[-1])
    return y + b.astype(u.dtype)


def conv_two_streams(u, w, b, n_ctx):
    return jnp.concatenate([dwconv_centred(u[:, :n_ctx], w, b), dwconv_centred(u[:, n_ctx:], w, b)], axis=1)


def to_chunks(a):
    B, T = a.shape[:2]
    return a.reshape(B, T // CHUNK, CHUNK, *a.shape[2:]).swapaxes(0, 1)


def from_chunks(y):
    nc, B, Q = y.shape[:3]
    return y.swapaxes(0, 1).reshape(B, nc * Q, *y.shape[3:])


def bidirectional_scan(fwd, bwd, init, n_ctx):
    outs = []
    for (scan_fn, seqs), rev in ((fwd, False), (bwd, True)):
        ctx_in = [a[:, :n_ctx] for a in seqs]
        lat_in = [a[:, n_ctx:] for a in seqs]
        if rev:
            ctx_in = [jnp.flip(a, axis=1) for a in ctx_in]
            lat_in = [jnp.flip(a, axis=1) for a in lat_in]
        y_c, s_c = scan_fn(ctx_in, init)
        y_l, _ = scan_fn(lat_in, s_c)
        if rev:
            y_c, y_l = jnp.flip(y_c, axis=1), jnp.flip(y_l, axis=1)
        outs.append((y_c, y_l))
    return outs[0][0] + outs[1][0], outs[0][1] + outs[1][1]


def ssd_scan(inputs, h0, A):
    xs, Bm, Cm, dt = (a.astype(jnp.float32) for a in inputs)
    tri = jnp.tril(jnp.ones((CHUNK, CHUNK), bool))[None, :, :, None, None]

    def body(h, chunk):
        xc, Bc, Cc, dtc = chunk
        cum = jnp.cumsum(dtc * A, axis=1)
        seg = cum[:, :, None] - cum[:, None, :]
        decay = jnp.exp(jnp.where(tri, seg, -jnp.inf))
        scores = jnp.einsum('btgn,bsgn->btsg', Cc, Bc)[..., None] * decay
        xdt = xc * dtc[..., None]
        y = jnp.einsum('btsgr,bsgrp->btgrp', scores, xdt)
        y = y + jnp.einsum('btgn,bgrpn->btgrp', Cc, h) * jnp.exp(cum)[..., None]
        tail = jnp.exp(cum[:, -1:] - cum)[..., None]
        h = h * jnp.exp(cum[:, -1])[..., None, None] + jnp.einsum('bsgn,bsgrp->bgrpn', Bc, xdt * tail)
        return h, y

    h, y = lax.scan(body, h0, (to_chunks(xs), to_chunks(Bm), to_chunks(Cm), to_chunks(dt)))
    return from_chunks(y), h


def mlstm_scan(inputs, state):
    q, k, v, log_i, log_f = (a.astype(jnp.float32) for a in inputs)
    tri = jnp.tril(jnp.ones((CHUNK, CHUNK), bool))[None, :, :, None]

    def body(carry, chunk):
        C, n, m = carry
        qc, kc, vc, lic, lfc = chunk
        b = jnp.cumsum(lfc, axis=1)
        dmat = jnp.where(tri, b[:, :, None] - b[:, None, :] + lic[:, None], -jnp.inf)
        m_prev = b + m[:, None]
        m_t = jnp.maximum(m_prev, dmat.max(axis=2))
        w = jnp.exp(dmat - m_t[:, :, None]) * jnp.einsum('bthd,bshd->btsh', qc, kc)
        s_prev = jnp.exp(m_prev - m_t)
        num = jnp.einsum('btsh,bshv->bthv', w, vc) + s_prev[..., None] * jnp.einsum('bhvd,bthd->bthv', C, qc)
        den = w.sum(axis=2) + s_prev * jnp.einsum('bhd,bthd->bth', n, qc)
        h = num / jnp.maximum(jnp.abs(den), jnp.exp(-m_t))[..., None]
        g = b[:, -1:] - b + lic
        m_new = jnp.maximum(b[:, -1] + m, g.max(axis=1))
        wk = jnp.exp(g - m_new[:, None])
        decay = jnp.exp(b[:, -1] + m - m_new)
        C = decay[..., None, None] * C + jnp.einsum('bsh,bshv,bshd->bhvd', wk, vc, kc)
        n = decay[..., None] * n + jnp.einsum('bsh,bshd->bhd', wk, kc)
        return (C, n, m_new), h

    state, h = lax.scan(body, state, (to_chunks(q), to_chunks(k), to_chunks(v), to_chunks(log_i), to_chunks(log_f)))
    return from_chunks(h), state


def axial_rope(rows):
    row = jnp.repeat(jnp.arange(rows), GRID_W).astype(jnp.float32)
    col = jnp.tile(jnp.arange(GRID_W), rows).astype(jnp.float32)
    n_freq = HEAD_DIM // 4
    inv = ROPE_THETA ** (-jnp.arange(n_freq, dtype=jnp.float32) / n_freq)
    ang = jnp.concatenate([row[:, None] * inv, col[:, None] * inv], axis=-1)
    return jnp.cos(ang), jnp.sin(ang)


def apply_rope(x, cos, sin):
    shape = (cos.shape[0],) + (1,) * (x.ndim - 3) + (cos.shape[1],)
    cs, sn = cos.reshape(shape), sin.reshape(shape)
    xf = x.astype(jnp.float32).reshape(*x.shape[:-1], x.shape[-1] // 2, 2)
    x0, x1 = xf[..., 0], xf[..., 1]
    out = jnp.stack([x0 * cs - x1 * sn, x0 * sn + x1 * cs], axis=-1)
    return out.reshape(x.shape).astype(x.dtype)


def gqa_attend(q, k, v):
    s = jnp.einsum('bqgrd,bkgd->bgrqk', q, k).astype(jnp.float32) * HEAD_DIM ** -0.5
    p = jax.nn.softmax(s, axis=-1).astype(v.dtype)
    return jnp.einsum('bgrqk,bkgd->bqgrd', p, v)


def ssd_mixer(z, xbc, dt_raw, conv_w, conv_b, A_log, dt_bias, D_skip, norm_g, n_ctx, need_ctx):
    B, T = z.shape[:2]
    xbc = jax.nn.silu(conv_two_streams(xbc, conv_w, conv_b, n_ctx))
    xs, Bm, Cm = jnp.split(xbc, [D_SSD, D_SSD + SSD_GROUPS * SSD_STATE], axis=-1)
    xs = xs.reshape(B, T, SSD_GROUPS, SSD_HPG, SSD_HEADDIM)
    Bm = Bm.reshape(B, T, SSD_GROUPS, SSD_STATE)
    Cm = Cm.reshape(B, T, SSD_GROUPS, SSD_STATE)
    dt = jax.nn.softplus(dt_raw.astype(jnp.float32).reshape(B, T, 2, SSD_GROUPS, SSD_HPG)
                         + dt_bias.astype(jnp.float32).reshape(2, SSD_GROUPS, SSD_HPG))
    A = -jnp.exp(A_log.astype(jnp.float32)).reshape(2, SSD_GROUPS, SSD_HPG)
    h0 = jnp.zeros((B, SSD_GROUPS, SSD_HPG, SSD_HEADDIM, SSD_STATE), jnp.float32)
    y_c, y_l = bidirectional_scan((functools.partial(ssd_scan, A=A[0]), (xs, Bm, Cm, dt[:, :, 0])),
                                  (functools.partial(ssd_scan, A=A[1]), (xs, Bm, Cm, dt[:, :, 1])),
                                  h0, n_ctx)
    d = D_skip.astype(jnp.float32).reshape(SSD_GROUPS, SSD_HPG)[..., None]

    def finish(y, x_part, z_part):
        y = (y + d * x_part).reshape(B, -1, D_SSD) * jax.nn.silu(z_part.astype(jnp.float32))
        return rms_norm(y, norm_g).astype(z.dtype)

    out_l = finish(y_l, xs[:, n_ctx:], z[:, n_ctx:])
    out_c = finish(y_c, xs[:, :n_ctx], z[:, :n_ctx]) if need_ctx else None
    return out_c, out_l


def attention_mixer(q, k, v, q_g, k_g, cos, sin, n_ctx, need_ctx):
    B, T = q.shape[:2]
    q = rms_norm(q.reshape(B, T, ATT_KV_HEADS, ATT_GROUP, HEAD_DIM), q_g)
    k = rms_norm(k.reshape(B, T, ATT_KV_HEADS, HEAD_DIM), k_g)
    v = v.reshape(B, T, ATT_KV_HEADS, HEAD_DIM)
    q_c, k_c, v_c = q[:, :n_ctx], k[:, :n_ctx], v[:, :n_ctx]
    q_l = apply_rope(q[:, n_ctx:], cos, sin)
    k_l = apply_rope(k[:, n_ctx:], cos, sin)
    k_all = jnp.concatenate([k_l, k_c], axis=1)
    v_all = jnp.concatenate([v[:, n_ctx:], v_c], axis=1)
    t_lat = q_l.shape[1]
    q_blocks = q_l.reshape(B, t_lat // Q_BLOCK, Q_BLOCK, ATT_KV_HEADS, ATT_GROUP, HEAD_DIM).swapaxes(0, 1)
    o_l = lax.map(lambda qb: gqa_attend(qb, k_all, v_all), q_blocks)
    o_l = o_l.swapaxes(0, 1).reshape(B, t_lat, D_ATT)
    o_c = gqa_attend(q_c, k_c, v_c).reshape(B, n_ctx, D_ATT) if need_ctx else None
    return o_c, o_l


def mlstm_mixer(qk, v, o, gates, conv_w, conv_b, gate_b, norm_g, n_ctx, need_ctx):
    B, T = v.shape[:2]
    qk = jax.nn.silu(conv_two_streams(qk, conv_w, conv_b, n_ctx))
    q, k = jnp.split(qk, 2, axis=-1)
    q = q.reshape(B, T, MLS_HEADS, MLS_QK_DIM)
    k = k.reshape(B, T, MLS_HEADS, MLS_QK_DIM) * MLS_QK_DIM ** -0.5
    v = v.reshape(B, T, MLS_HEADS, MLS_V_DIM)
    g = gates.astype(jnp.float32).reshape(B, T, 4, MLS_HEADS) + gate_b.astype(jnp.float32)
    log_i = g[:, :, 0::2]
    log_f = jax.nn.log_sigmoid(g[:, :, 1::2])
    init = (jnp.zeros((B, MLS_HEADS, MLS_V_DIM, MLS_QK_DIM), jnp.float32),
            jnp.zeros((B, MLS_HEADS, MLS_QK_DIM), jnp.float32),
            jnp.zeros((B, MLS_HEADS), jnp.float32))
    h_c, h_l = bidirectional_scan((mlstm_scan, (q, k, v, log_i[:, :, 0], log_f[:, :, 0])),
                                  (mlstm_scan, (q, k, v, log_i[:, :, 1], log_f[:, :, 1])),
                                  init, n_ctx)
    gain = norm_g.reshape(MLS_HEADS, MLS_V_DIM)

    def finish(h, o_part):
        h = rms_norm(h, gain).reshape(B, -1, D_MLS)
        return (jax.nn.sigmoid(o_part.astype(jnp.float32)) * h).astype(v.dtype)

    out_l = finish(h_l, o[:, n_ctx:])
    out_c = finish(h_c, o[:, :n_ctx]) if need_ctx else None
    return out_c, out_l


def moe(h, w_router, b_router, w_gu, b_gu, w_dn, b_dn):
    B, T, D = h.shape
    tok = h.reshape(B * T, D)
    logits = (tok @ w_router + b_router).astype(jnp.float32)
    top_logit, top_idx = lax.top_k(logits, TOP_K)
    weights = jax.nn.softmax(top_logit, axis=-1)
    combine = jnp.einsum('tk,tke->te', weights, jax.nn.one_hot(top_idx, N_EXPERTS, dtype=jnp.float32))
    out = jnp.zeros((B * T, D), jnp.float32)
    for e in range(N_EXPERTS):
        gu = tok @ w_gu[e] + b_gu[e]
        glu = jnp.minimum(gu[:, 0::2], SWIGLU_LIMIT)
        lin = jnp.clip(gu[:, 1::2], -SWIGLU_LIMIT, SWIGLU_LIMIT)
        act = glu * jax.nn.sigmoid(SWIGLU_ALPHA * glu) * (lin + 1.0)
        out = out + combine[:, e:e + 1] * (act @ w_dn[e] + b_dn[e])
    return out.astype(h.dtype).reshape(B, T, D)


def setup_inputs(seed: int = 0) -> dict:
    key = jax.random.key(seed)
    ks = jax.random.split(key, 32)
    f32 = jnp.float32

    def nrm(k, shape, scale):
        return jax.random.normal(k, shape, f32) * scale

    dt0 = jnp.exp(jax.random.uniform(ks[9], (DEPTH, 2, SSD_HEADS), f32, np.log(1e-3), np.log(1e-1)))
    gate_base = jnp.array([0.0, 3.0, 0.0, 3.0], f32)[None, :, None]
    return {
        'x': nrm(ks[0], (BATCH, SEQ, D_MODEL), 1.0),
        'c': nrm(ks[1], (BATCH, D_MODEL), 1.0),
        'ctx': nrm(ks[2], (BATCH, CTX_LEN, D_MODEL), 1.0),
        'c_ctx': nrm(ks[3], (D_MODEL,), 1.0),
        'w_ada': nrm(ks[4], (DEPTH, D_MODEL, 6 * D_MODEL), 0.5 * D_MODEL ** -0.5),
        'b_ada': nrm(ks[5], (DEPTH, 6 * D_MODEL), 0.02),
        'w_in': nrm(ks[6], (DEPTH, D_MODEL, D_IN), D_MODEL ** -0.5),
        'ssd_conv_w': nrm(ks[7], (DEPTH, CONV_W, SSD_CONV_CH), CONV_W ** -0.5),
        'ssd_conv_b': nrm(ks[8], (DEPTH, SSD_CONV_CH), 0.02),
        'ssd_A_log': jnp.log(jax.random.uniform(ks[10], (DEPTH, 2, SSD_HEADS), f32, 1.0, 16.0)),
        'ssd_dt_bias': dt0 + jnp.log(-jnp.expm1(-dt0)),
        'ssd_D': 1.0 + nrm(ks[11], (DEPTH, SSD_HEADS), 0.1),
        'ssd_norm_g': 1.0 + nrm(ks[12], (DEPTH, D_SSD), 0.02),
        'att_q_norm_g': 1.0 + nrm(ks[13], (DEPTH, HEAD_DIM), 0.02),
        'att_k_norm_g': 1.0 + nrm(ks[14], (DEPTH, HEAD_DIM), 0.02),
        'mls_conv_w': nrm(ks[15], (DEPTH, CONV_W, MLS_CONV_CH), CONV_W ** -0.5),
        'mls_conv_b': nrm(ks[16], (DEPTH, MLS_CONV_CH), 0.02),
        'mls_gate_b': gate_base + nrm(ks[17], (DEPTH, 4, MLS_HEADS), 0.1),
        'mls_norm_g': 1.0 + nrm(ks[18], (DEPTH, D_MLS), 0.02),
        'w_out': nrm(ks[19], (DEPTH, D_MIX, D_MODEL), DEEPNORM_BETA * D_MIX ** -0.5),
        'ln1_g': 1.0 + nrm(ks[20], (DEPTH, D_MODEL), 0.02),
        'ln1_b': nrm(ks[21], (DEPTH, D_MODEL), 0.02),
        'w_router': nrm(ks[22], (DEPTH, D_MODEL, N_EXPERTS), D_MODEL ** -0.5),
        'b_router': nrm(ks[23], (DEPTH, N_EXPERTS), 0.01),
        'w_gu': nrm(ks[24], (DEPTH, N_EXPERTS, D_MODEL, 2 * D_EXPERT), D_MODEL ** -0.5),
        'b_gu': nrm(ks[25], (DEPTH, N_EXPERTS, 2 * D_EXPERT), 0.02),
        'w_dn': nrm(ks[26], (DEPTH, N_EXPERTS, D_EXPERT, D_MODEL), DEEPNORM_BETA * D_EXPERT ** -0.5),
        'b_dn': nrm(ks[27], (DEPTH, N_EXPERTS, D_MODEL), 0.02),
        'ln2_g': 1.0 + nrm(ks[28], (DEPTH, D_MODEL), 0.02),
        'ln2_b': nrm(ks[29], (DEPTH, D_MODEL), 0.02),
    }


def reference(x, c, ctx, c_ctx, w_ada, b_ada, w_in, ssd_conv_w, ssd_conv_b, ssd_A_log, ssd_dt_bias,
              ssd_D, ssd_norm_g, att_q_norm_g, att_k_norm_g, mls_conv_w, mls_conv_b, mls_gate_b,
              mls_norm_g, w_out, ln1_g, ln1_b, w_router, b_router, w_gu, b_gu, w_dn, b_dn,
              ln2_g, ln2_b):
    n_ctx = ctx.shape[1]
    ROWS = x.shape[1] // GRID_W
    cos, sin = axial_rope(ROWS)
    split_points = [int(p) for p in np.cumsum(IN_SIZES)[:-1]]
    xc = ctx
    for l in range(DEPTH):
        need_ctx = l < DEPTH - 1
        mod = jax.nn.silu(c) @ w_ada[l] + b_ada[l]
        mod_c = jax.nn.silu(c_ctx) @ w_ada[l] + b_ada[l]
        sh1, sc1, g1, sh2, sc2, g2 = jnp.split(mod[:, None, :], 6, axis=-1)
        sh1c, sc1c, g1c, sh2c, sc2c, g2c = jnp.split(mod_c, 6, axis=-1)

        h_all = jnp.concatenate([xc * (1.0 + sc1c) + sh1c, x * (1.0 + sc1) + sh1], axis=1)
        proj = h_all @ w_in[l]
        z, xbc, dt_raw, aq, ak, av, mqk, mv, mo, mg = jnp.split(proj, split_points, axis=-1)
        ssd_c, ssd_l = ssd_mixer(z, xbc, dt_raw, ssd_conv_w[l], ssd_conv_b[l], ssd_A_log[l],
                                 ssd_dt_bias[l], ssd_D[l], ssd_norm_g[l], n_ctx, need_ctx)
        att_c, att_l = attention_mixer(aq, ak, av, att_q_norm_g[l], att_k_norm_g[l], cos, sin,
                                       n_ctx, need_ctx)
        mls_c, mls_l = mlstm_mixer(mqk, mv, mo, mg, mls_conv_w[l], mls_conv_b[l], mls_gate_b[l],
                                   mls_norm_g[l], n_ctx, need_ctx)
        mix_l = jnp.concatenate([ssd_l, att_l, mls_l], axis=-1) @ w_out[l]
        x = layer_norm(DEEPNORM_ALPHA * x + g1 * mix_l, ln1_g[l], ln1_b[l])

        if need_ctx:
            mix_c = jnp.concatenate([ssd_c, att_c, mls_c], axis=-1) @ w_out[l]
            xc = layer_norm(DEEPNORM_ALPHA * xc + g1c * mix_c, ln1_g[l], ln1_b[l])
            f_all = moe(jnp.concatenate([xc * (1.0 + sc2c) + sh2c, x * (1.0 + sc2) + sh2], axis=1),
                        w_router[l], b_router[l], w_gu[l], b_gu[l], w_dn[l], b_dn[l])
            f_l = f_all[:, n_ctx:]
            xc = layer_norm(DEEPNORM_ALPHA * xc + g2c * f_all[:, :n_ctx], ln2_g[l], ln2_b[l])
        else:
            f_l = moe(x * (1.0 + sc2) + sh2, w_router[l], b_router[l], w_gu[l], b_gu[l], w_dn[l], b_dn[l])
        x = layer_norm(DEEPNORM_ALPHA * x + g2 * f_l, ln2_g[l], ln2_b[l])
    return x
```

```python
import functools

import numpy as np
import jax
import jax.numpy as jnp
from jax import lax
from jax.experimental import pallas as pl
from jax.experimental.pallas import tpu as pltpu

F32 = jnp.float32
BF16 = jnp.bfloat16

GRID_W = 64
CHUNK = 128
CONV_W = 5
SSD_HEADDIM = 64
SSD_GROUPS = 4
SSD_STATE = 128
HEAD_DIM = 128
ATT_KV_HEADS = 2
ROPE_THETA = 10000.0
MLS_V_DIM = 256
MLS_QK_DIM = 128
TOP_K = 4
SWIGLU_LIMIT = 7.0
SWIGLU_ALPHA = 1.702

LANE = 128
SMALL_W = LANE
NEG_BIG = -1e30
VMEM_LIMIT = 56 * 1024 * 1024


def _cparams(sem, vmem=VMEM_LIMIT):
    return pltpu.CompilerParams(dimension_semantics=sem, vmem_limit_bytes=vmem)


def _pick(n, cands):
    for c in cands:
        if n % c == 0:
            return c
    raise ValueError(f"no tile in {cands} divides {n}")


def _sigmoid(x):
    return 1.0 / (1.0 + jnp.exp(-x))


def _softplus(x):
    return jnp.maximum(x, 0.0) + jnp.log(1.0 + jnp.exp(-jnp.abs(x)))


def _split_bf16(a, n):
    parts = []
    r = a
    for _ in range(n):
        p = r.astype(BF16)
        parts.append(p)
        r = r - p.astype(F32)
    return parts


def _dot01_left(m01, a, n=3):
    out = None
    for p in _split_bf16(a, n):
        t = jnp.dot(m01, p, preferred_element_type=F32)
        out = t if out is None else out + t
    return out


def _dot01_right(a, m01, n=2):
    out = None
    for p in _split_bf16(a, n):
        t = jnp.dot(p, m01, preferred_element_type=F32)
        out = t if out is None else out + t
    return out


def _ada_kernel(c_ref, w_ref, b_ref, o_ref):
    c = c_ref[...]
    s = (c * _sigmoid(c)).astype(BF16)
    o_ref[...] = jnp.dot(s, w_ref[...].astype(BF16), preferred_element_type=F32) + b_ref[...]


def _ada(cvec, w_ada, b_ada):
    depth, d, n = w_ada.shape
    tn = _pick(n, (512, 256, 128))
    return pl.pallas_call(
        _ada_kernel,
        grid=(depth, n // tn),
        in_specs=[
            pl.BlockSpec((8, d), lambda l, j: (0, 0)),
            pl.BlockSpec((None, d, tn), lambda l, j: (l, 0, j)),
            pl.BlockSpec((None, 1, tn), lambda l, j: (l, 0, j)),
        ],
        out_specs=pl.BlockSpec((None, 8, tn), lambda l, j: (l, 0, j)),
        out_shape=jax.ShapeDtypeStruct((depth, 8, n), F32),
        compiler_params=_cparams(("arbitrary", "arbitrary")),
        name="ada_mod",
    )(cvec, w_ada, b_ada.reshape(depth, 1, n))


def _inproj_kernel(x_ref, sh_ref, sc_ref, w_ref, o_ref, h_scr, *, n_ctx, tm):
    i = pl.program_id(0)
    j = pl.program_id(1)

    @pl.when(j == 0)
    def _():
        rows = i * tm + lax.broadcasted_iota(jnp.int32, (tm, 1), 0)
        is_ctx = rows < n_ctx
        sc = jnp.where(is_ctx, sc_ref[1:2, :], sc_ref[0:1, :])
        sh = jnp.where(is_ctx, sh_ref[1:2, :], sh_ref[0:1, :])
        h_scr[...] = (x_ref[...] * (1.0 + sc) + sh).astype(BF16)

    o_ref[...] = jnp.dot(h_scr[...], w_ref[...], preferred_element_type=F32)


def _inproj(xall, mod, w, n_ctx):
    t, d = xall.shape
    n = w.shape[1]
    tm = _pick(t, (768, 512, 256, 128))
    tn = _pick(n, (512, 256, 128))
    return pl.pallas_call(
        functools.partial(_inproj_kernel, n_ctx=n_ctx, tm=tm),
        grid=(t // tm, n // tn),
        in_specs=[
            pl.BlockSpec((tm, d), lambda i, j: (i, 0), pipeline_mode=pl.Buffered(1)),
            pl.BlockSpec((8, d), lambda i, j: (0, 0)),
            pl.BlockSpec((8, d), lambda i, j: (0, 1)),
            pl.BlockSpec((d, tn), lambda i, j: (0, j)),
        ],
        out_specs=pl.BlockSpec((tm, tn), lambda i, j: (i, j)),
        out_shape=jax.ShapeDtypeStruct((t, n), F32),
        scratch_shapes=[pltpu.VMEM((tm, d), BF16)],
        compiler_params=_cparams(("arbitrary", "arbitrary")),
        name="in_proj",
    )(xall, mod, mod, w)


def _conv_kernel(x_ref, w_ref, b_ref, o_ref, *, n_ctx, tt):
    t = x_ref.shape[0]
    w = w_ref[...]
    bias = b_ref[...]
    halo = 8
    pad = CONV_W // 2

    def body(i, carry):
        t0 = pl.multiple_of(i * tt, tt)
        cur = x_ref[pl.ds(t0, tt), :]
        p0 = pl.multiple_of(jnp.maximum(t0 - halo, 0), halo)
        n0 = pl.multiple_of(jnp.minimum(t0 + tt, t - halo), halo)
        prev = x_ref[pl.ds(p0, halo), :]
        nxt = x_ref[pl.ds(n0, halo), :]
        seg_start = jnp.logical_or(t0 == 0, t0 == n_ctx)
        seg_end = jnp.logical_or(t0 + tt == n_ctx, t0 + tt == t)
        prev = jnp.where(seg_start, 0.0, prev)
        nxt = jnp.where(seg_end, 0.0, nxt)
        win = jnp.concatenate([prev, cur, nxt], axis=0)
        acc = bias + w[0:1, :] * win[halo - pad:halo - pad + tt, :]
        for k in range(1, CONV_W):
            acc = acc + w[k:k + 1, :] * win[halo - pad + k:halo - pad + k + tt, :]
        o_ref[pl.ds(t0, tt), :] = acc * _sigmoid(acc)
        return carry

    lax.fori_loop(0, t // tt, body, 0)


def _conv(proj, w8, b, col0, width, n_ctx):
    t = proj.shape[0]
    tt = _pick(n_ctx, (256, 128))
    assert t % tt == 0 and col0 % LANE == 0 and width % LANE == 0
    cb0 = col0 // LANE
    return pl.pallas_call(
        functools.partial(_conv_kernel, n_ctx=n_ctx, tt=tt),
        grid=(width // LANE,),
        in_specs=[
            pl.BlockSpec((t, LANE), lambda c: (0, cb0 + c)),
            pl.BlockSpec((8, LANE), lambda c: (0, c)),
            pl.BlockSpec((1, LANE), lambda c: (0, c)),
        ],
        out_specs=pl.BlockSpec((t, LANE), lambda c: (0, c)),
        out_shape=jax.ShapeDtypeStruct((t, width), F32),
        compiler_params=_cparams(("arbitrary",)),
        name="dwconv_silu",
    )(proj, w8, b)


def _chunk_order(i, n_chunks, n_ctx_chunks, reverse):
    if not reverse:
        return i
    return jnp.where(i < n_ctx_chunks, n_ctx_chunks - 1 - i, n_chunks - 1 - (i - n_ctx_chunks))


def _tri_mask(reverse):
    row = lax.broadcasted_iota(jnp.int32, (CHUNK, CHUNK), 0)
    col = lax.broadcasted_iota(jnp.int32, (CHUNK, CHUNK), 1)
    return (col >= row) if reverse else (col <= row)


def _ssd_kernel(*refs, reverse, finish, heads, hpg):
    if finish:
        (xs_ref, b_ref, c_ref, sm_ref, bias_ref, alog_ref, exp_ref,
         z_ref, yprev_ref, dskip_ref, g_ref, o_ref, h_scr) = refs
    else:
        (xs_ref, b_ref, c_ref, sm_ref, bias_ref, alog_ref, exp_ref, o_ref, h_scr) = refs
    gw = hpg * SSD_HEADDIM

    @pl.when(pl.program_id(0) == 0)
    def _():
        h_scr[...] = jnp.zeros_like(h_scr)

    mask = _tri_mask(reverse)
    tri = mask.astype(BF16)
    p = sm_ref[...] + bias_ref[...]
    dt = _softplus(p)
    dta = dt * (-jnp.exp(alog_ref[...]))
    cum = _dot01_left(tri, dta)
    cum_t = cum.T
    tot = cum[0:1, :] if reverse else cum[CHUNK - 1:CHUNK, :]
    e01 = exp_ref[...]
    dt_x = _dot01_right(dt, e01)
    in_x = _dot01_right(jnp.exp(cum), e01)
    tail_x = _dot01_right(jnp.exp(tot - cum), e01)
    tot_x = _dot01_right(jnp.broadcast_to(jnp.exp(tot), (8, SMALL_W)), e01)[0:1, :]

    xs = xs_ref[...]
    xdt = xs * dt_x
    xdt_b = xdt.astype(BF16)
    xtail_b = (xdt * tail_x).astype(BF16)
    bm = b_ref[...]
    cm = c_ref[...]
    c0 = heads if reverse else 0
    ys = []
    for g in range(SSD_GROUPS):
        bg = bm[:, g * SSD_STATE:(g + 1) * SSD_STATE]
        cg = cm[:, g * SSD_STATE:(g + 1) * SSD_STATE].astype(BF16)
        cb = lax.dot_general(cg, bg.astype(BF16), (((1,), (1,)), ((), ())),
                             preferred_element_type=F32)
        h_t = h_scr[g]
        y_g = jnp.dot(cg, h_t.astype(BF16), preferred_element_type=F32) * in_x[:, g * gw:(g + 1) * gw]
        parts = []
        for r in range(hpg):
            h = g * hpg + r
            c = c0 + h
            seg = cum[:, c:c + 1] - cum_t[c:c + 1, :]
            decay = jnp.exp(jnp.where(mask, seg, NEG_BIG))
            m = (cb * decay).astype(BF16)
            parts.append(jnp.dot(m, xdt_b[:, h * SSD_HEADDIM:(h + 1) * SSD_HEADDIM],
                                 preferred_element_type=F32))
        ys.append(y_g + jnp.concatenate(parts, axis=1))
        h_scr[g] = h_t * tot_x[:, g * gw:(g + 1) * gw] + jnp.dot(
            bg.T.astype(BF16), xtail_b[:, g * gw:(g + 1) * gw], preferred_element_type=F32)
    y = jnp.concatenate(ys, axis=1)

    if finish:
        z = z_ref[...]
        yt = (yprev_ref[...] + y + dskip_ref[...] * xs) * (z * _sigmoid(z))
        ms = jnp.mean(yt * yt, axis=-1, keepdims=True)
        o_ref[...] = (yt * lax.rsqrt(ms + 1e-6) * g_ref[...]).astype(o_ref.dtype)
    else:
        o_ref[...] = y


def _ssd_pass(conv_out, proj, lay, bias1, alog_row, e01, n_ctx, reverse, extra=None):
    t = proj.shape[0]
    nch, ncc = t // CHUNK, n_ctx // CHUNK
    d_ssd = lay["d_ssd"]
    heads = d_ssd // SSD_HEADDIM
    gs = SSD_GROUPS * SSD_STATE
    order = functools.partial(_chunk_order, n_chunks=nch, n_ctx_chunks=ncc, reverse=reverse)
    const = lambda i: (0, 0)
    in_specs = [
        pl.BlockSpec((CHUNK, d_ssd), lambda i: (order(i), lay["co_xs"] // d_ssd)),
        pl.BlockSpec((CHUNK, gs), lambda i: (order(i), lay["co_b"] // gs)),
        pl.BlockSpec((CHUNK, gs), lambda i: (order(i), lay["co_c"] // gs)),
        pl.BlockSpec((CHUNK, SMALL_W), lambda i: (order(i), lay["sm1"] // SMALL_W)),
        pl.BlockSpec((1, SMALL_W), const),
        pl.BlockSpec((1, SMALL_W), const),
        pl.BlockSpec((SMALL_W, d_ssd), const),
    ]
    args = [conv_out, conv_out, conv_out, proj, bias1, alog_row, e01]
    finish = extra is not None
    if finish:
        y_prev, dskip_x, norm_g = extra
        in_specs += [
            pl.BlockSpec((CHUNK, d_ssd), lambda i: (order(i), lay["z"] // d_ssd)),
            pl.BlockSpec((CHUNK, d_ssd), lambda i: (order(i), 0)),
            pl.BlockSpec((1, d_ssd), const),
            pl.BlockSpec((1, d_ssd), const),
        ]
        args += [proj, y_prev, dskip_x, norm_g]
    return pl.pallas_call(
        functools.partial(_ssd_kernel, reverse=reverse, finish=finish, heads=heads,
                          hpg=heads // SSD_GROUPS),
        grid=(nch,),
        in_specs=in_specs,
        out_specs=pl.BlockSpec((CHUNK, d_ssd), lambda i: (order(i), 0)),
        out_shape=jax.ShapeDtypeStruct((t, d_ssd), BF16 if finish else F32),
        scratch_shapes=[pltpu.VMEM((SSD_GROUPS, SSD_STATE, d_ssd // SSD_GROUPS), F32)],
        compiler_params=_cparams(("arbitrary",)),
        name="ssd_bwd_finish" if finish else "ssd_fwd",
    )(*args)


def _mls_kernel(*refs, reverse, finish, heads):
    if finish:
        (q_ref, k_ref, v_ref, sm1_ref, sm2_ref, b1_ref, b2_ref,
         og_ref, hprev_ref, gain_ref, o_ref, s_scr, m_scr) = refs
    else:
        (q_ref, k_ref, v_ref, sm1_ref, sm2_ref, b1_ref, b2_ref, o_ref, s_scr, m_scr) = refs

    @pl.when(pl.program_id(0) == 0)
    def _():
        s_scr[...] = jnp.zeros_like(s_scr)
        m_scr[...] = jnp.zeros_like(m_scr)

    mask = _tri_mask(reverse)
    tri = mask.astype(BF16)
    lf = -_softplus(-(sm1_ref[...] + b1_ref[...]))
    li = sm2_ref[...] + b2_ref[...]
    b = _dot01_left(tri, lf)
    b_t = b.T
    li_t = li.T
    tot = b[0:1, :] if reverse else b[CHUNK - 1:CHUNK, :]
    m_row = m_scr[...]
    g_all = tot - b + li
    m_new = jnp.maximum(tot + m_row, jnp.max(g_all, axis=0, keepdims=True))
    wk_all = jnp.exp(g_all - m_new)
    decay_row = jnp.exp(tot + m_row - m_new)
    mprev_all = b + m_row
    m_scr[...] = m_new

    q = q_ref[...]
    k = k_ref[...] * (MLS_QK_DIM ** -0.5)
    v = v_ref[...]
    ones_col = (lax.broadcasted_iota(jnp.int32, (CHUNK, LANE), 1) == 0).astype(F32)
    c0 = lay_fcol(heads, reverse)
    outs = []
    for h in range(heads):
        c = c0 + h
        qh = q[:, h * MLS_QK_DIM:(h + 1) * MLS_QK_DIM].astype(BF16)
        kh = k[:, h * MLS_QK_DIM:(h + 1) * MLS_QK_DIM]
        khb = kh.astype(BF16)
        vext = jnp.concatenate([v[:, h * MLS_V_DIM:(h + 1) * MLS_V_DIM], ones_col], axis=1)
        dmat = jnp.where(mask, b[:, c:c + 1] - b_t[c:c + 1, :] + li_t[c:c + 1, :], NEG_BIG)
        m_prev = mprev_all[:, c:c + 1]
        m_t = jnp.maximum(m_prev, jnp.max(dmat, axis=1, keepdims=True))
        qk = lax.dot_general(qh, khb, (((1,), (1,)), ((), ())), preferred_element_type=F32)
        w = jnp.exp(dmat - m_t) * qk
        s_prev = jnp.exp(m_prev - m_t)
        s_h = s_scr[h]
        numx = (jnp.dot(w.astype(BF16), vext.astype(BF16), preferred_element_type=F32)
                + s_prev * jnp.dot(qh, s_h.astype(BF16), preferred_element_type=F32))
        den = numx[:, MLS_V_DIM:MLS_V_DIM + 1]
        outs.append(numx[:, :MLS_V_DIM] / jnp.maximum(jnp.abs(den), jnp.exp(-m_t)))
        s_scr[h] = decay_row[:, c:c + 1] * s_h + jnp.dot(
            kh.T.astype(BF16), (wk_all[:, c:c + 1] * vext).astype(BF16), preferred_element_type=F32)

    if finish:
        hp = hprev_ref[...]
        og = og_ref[...]
        gain = gain_ref[...]
        for h in range(heads):
            sl = slice(h * MLS_V_DIM, (h + 1) * MLS_V_DIM)
            hs = hp[:, sl] + outs[h]
            ms = jnp.mean(hs * hs, axis=-1, keepdims=True)
            o_ref[:, sl] = (_sigmoid(og[:, sl]) * (hs * lax.rsqrt(ms + 1e-6) * gain[:, sl])).astype(o_ref.dtype)
    else:
        for h in range(heads):
            o_ref[:, h * MLS_V_DIM:(h + 1) * MLS_V_DIM] = outs[h]


def lay_fcol(heads, reverse):
    return GATE_COL0 + (heads if reverse else 0)


def _mls_pass(conv_out, proj, lay, bias1, bias2, n_ctx, reverse, extra=None):
    t = proj.shape[0]
    nch, ncc = t // CHUNK, n_ctx // CHUNK
    d_mls = lay["d_mls"]
    heads = d_mls // MLS_V_DIM
    qw = heads * MLS_QK_DIM
    order = functools.partial(_chunk_order, n_chunks=nch, n_ctx_chunks=ncc, reverse=reverse)
    const = lambda i: (0, 0)
    in_specs = [
        pl.BlockSpec((CHUNK, qw), lambda i: (order(i), lay["co_q"] // qw)),
        pl.BlockSpec((CHUNK, qw), lambda i: (order(i), lay["co_k"] // qw)),
        pl.BlockSpec((CHUNK, d_mls), lambda i: (order(i), lay["mv"] // d_mls)),
        pl.BlockSpec((CHUNK, SMALL_W), lambda i: (order(i), lay["sm1"] // SMALL_W)),
        pl.BlockSpec((CHUNK, SMALL_W), lambda i: (order(i), lay["sm2"] // SMALL_W)),
        pl.BlockSpec((1, SMALL_W), const),
        pl.BlockSpec((1, SMALL_W), const),
    ]
    args = [conv_out, conv_out, proj, proj, proj, bias1, bias2]
    finish = extra is not None
    if finish:
        h_prev, gain = extra
        in_specs += [
            pl.BlockSpec((CHUNK, d_mls), lambda i: (order(i), lay["mo"] // d_mls)),
            pl.BlockSpec((CHUNK, d_mls), lambda i: (order(i), 0)),
            pl.BlockSpec((1, d_mls), const),
        ]
        args += [proj, h_prev, gain]
    return pl.pallas_call(
        functools.partial(_mls_kernel, reverse=reverse, finish=finish, heads=heads),
        grid=(nch,),
        in_specs=in_specs,
        out_specs=pl.BlockSpec((CHUNK, d_mls), lambda i: (order(i), 0)),
        out_shape=jax.ShapeDtypeStruct((t, d_mls), BF16 if finish else F32),
        scratch_shapes=[pltpu.VMEM((heads, MLS_QK_DIM, MLS_V_DIM + LANE), F32),
                        pltpu.VMEM((1, SMALL_W), F32)],
        compiler_params=_cparams(("arbitrary",)),
        name="mlstm_bwd_finish" if finish else "mlstm_fwd",
    )(*args)


def _qk_prep_kernel(q_ref, k_ref, v_ref, cos_ref, sin_ref, gq_ref, gk_ref, qo_ref, ko_ref, vo_ref):
    cos = cos_ref[...]
    sin = sin_ref[...]

    def norm_rope(xh, g, scale):
        ms = jnp.mean(xh * xh, axis=-1, keepdims=True)
        xn = xh * lax.rsqrt(ms + 1e-6) * g
        return (xn * cos + pltpu.roll(xn, HEAD_DIM // 2, axis=1) * sin) * scale

    q = q_ref[...]
    for h in range(q.shape[1] // HEAD_DIM):
        sl = slice(h * HEAD_DIM, (h + 1) * HEAD_DIM)
        qo_ref[:, sl] = norm_rope(q[:, sl], gq_ref[...], HEAD_DIM ** -0.5).astype(qo_ref.dtype)
    k = k_ref[...]
    for h in range(k.shape[1] // HEAD_DIM):
        sl = slice(h * HEAD_DIM, (h + 1) * HEAD_DIM)
        ko_ref[:, sl] = norm_rope(k[:, sl], gk_ref[...], 1.0).astype(ko_ref.dtype)
    vo_ref[...] = v_ref[...].astype(vo_ref.dtype)


def _qk_prep(proj, lay, cos_t, sin_t, gq, gk):
    t = proj.shape[0]
    tm = _pick(t, (256, 128))
    d_att = lay["d_att"]
    kvw = ATT_KV_HEADS * HEAD_DIM
    const = lambda i: (0, 0)
    return pl.pallas_call(
        _qk_prep_kernel,
        grid=(t // tm,),
        in_specs=[
            pl.BlockSpec((tm, d_att), lambda i: (i, lay["aq"] // d_att)),
            pl.BlockSpec((tm, kvw), lambda i: (i, lay["ak"] // kvw)),
            pl.BlockSpec((tm, kvw), lambda i: (i, lay["av"] // kvw)),
            pl.BlockSpec((tm, HEAD_DIM), lambda i: (i, 0)),
            pl.BlockSpec((tm, HEAD_DIM), lambda i: (i, 0)),
            pl.BlockSpec((1, HEAD_DIM), const),
            pl.BlockSpec((1, HEAD_DIM), const),
        ],
        out_specs=[
            pl.BlockSpec((tm, d_att), lambda i: (i, 0)),
            pl.BlockSpec((tm, kvw), lambda i: (i, 0)),
            pl.BlockSpec((tm, kvw), lambda i: (i, 0)),
        ],
        out_shape=[jax.ShapeDtypeStruct((t, d_att), BF16),
                   jax.ShapeDtypeStruct((t, kvw), BF16),
                   jax.ShapeDtypeStruct((t, kvw), BF16)],
        compiler_params=_cparams(("arbitrary",)),
        name="qk_norm_rope",
    )(proj, proj, proj, cos_t, sin_t, gq, gk)


def _flash_kernel(q_ref, k_ref, v_ref, o_ref, m_scr, l_scr, acc_scr, *, group):
    j = pl.program_id(2)

    @pl.when(j == 0)
    def _():
        m_scr[...] = jnp.full_like(m_scr, NEG_BIG)
        l_scr[...] = jnp.zeros_like(l_scr)
        acc_scr[...] = jnp.zeros_like(acc_scr)

    k = k_ref[...]
    v = v_ref[...]
    for h in range(group):
        qh = q_ref[:, h * HEAD_DIM:(h + 1) * HEAD_DIM]
        s = lax.dot_general(qh, k, (((1,), (1,)), ((), ())), preferred_element_type=F32)
        m_old = m_scr[h]
        m_new = jnp.maximum(m_old, jnp.max(s, axis=1, keepdims=True))
        p = jnp.exp(s - m_new)
        alpha = jnp.exp(m_old - m_new)
        l_scr[h] = alpha * l_scr[h] + jnp.sum(p, axis=1, keepdims=True)
        acc_scr[h] = alpha * acc_scr[h] + jnp.dot(p.astype(BF16), v, preferred_element_type=F32)
        m_scr[h] = m_new

    @pl.when(j == pl.num_programs(2) - 1)
    def _():
        for h in range(group):
            o_ref[:, h * HEAD_DIM:(h + 1) * HEAD_DIM] = (acc_scr[h] / l_scr[h]).astype(o_ref.dtype)


def _flash(qn, kn, vb, q_row0, n_q, n_k):
    d_att = qn.shape[1]
    group = d_att // HEAD_DIM // ATT_KV_HEADS
    gw = group * HEAD_DIM
    tq = _pick(n_q, (256, 128))
    tk = _pick(n_k, (768, 512, 256, 128))
    assert q_row0 % tq == 0
    qb0 = q_row0 // tq
    return pl.pallas_call(
        functools.partial(_flash_kernel, group=group),
        grid=(ATT_KV_HEADS, n_q // tq, n_k // tk),
        in_specs=[
            pl.BlockSpec((tq, gw), lambda g, i, j: (qb0 + i, g)),
            pl.BlockSpec((tk, HEAD_DIM), lambda g, i, j: (j, g)),
            pl.BlockSpec((tk, HEAD_DIM), lambda g, i, j: (j, g)),
        ],
        out_specs=pl.BlockSpec((tq, gw), lambda g, i, j: (i, g)),
        out_shape=jax.ShapeDtypeStruct((n_q, d_att), BF16),
        scratch_shapes=[pltpu.VMEM((group, tq, 1), F32),
                        pltpu.VMEM((group, tq, 1), F32),
                        pltpu.VMEM((group, tq, HEAD_DIM), F32)],
        compiler_params=_cparams(("arbitrary", "arbitrary", "arbitrary")),
        name="flash_gqa",
    )(qn, kn, vb)


def _outproj_kernel(a1_ref, a2_ref, a3_ref, w1_ref, w2_ref, w3_ref, o_ref):
    acc = jnp.dot(a1_ref[...], w1_ref[...], preferred_element_type=F32)
    acc = acc + jnp.dot(a2_ref[...], w2_ref[...], preferred_element_type=F32)
    acc = acc + jnp.dot(a3_ref[...], w3_ref[...], preferred_element_type=F32)
    o_ref[...] = acc


def _outproj(a1, a2, a3, w1, w2, w3):
    t = a1.shape[0]
    n = w1.shape[1]
    tm = _pick(t, (768, 512, 256, 128))
    tn = _pick(n, (512, 256, 128))
    lhs = lambda a: pl.BlockSpec((tm, a.shape[1]), lambda i, j: (i, 0))
    rhs = lambda w: pl.BlockSpec((w.shape[0], tn), lambda i, j: (0, j))
    return pl.pallas_call(
        _outproj_kernel,
        grid=(t // tm, n // tn),
        in_specs=[lhs(a1), lhs(a2), lhs(a3), rhs(w1), rhs(w2), rhs(w3)],
        out_specs=pl.BlockSpec((tm, tn), lambda i, j: (i, j)),
        out_shape=jax.ShapeDtypeStruct((t, n), F32),
        compiler_params=_cparams(("arbitrary", "arbitrary")),
        name="out_proj",
    )(a1, a2, a3, w1, w2, w3)


def _topk_combine(logits, n_experts):
    lane = lax.broadcasted_iota(jnp.int32, logits.shape, 1)
    l = jnp.where(lane < n_experts, logits, NEG_BIG)
    tops = []
    for _ in range(TOP_K):
        m = jnp.max(l, axis=1, keepdims=True)
        idx = jnp.min(jnp.where(l == m, lane, LANE), axis=1, keepdims=True)
        sel = lane == idx
        tops.append((m, sel))
        l = jnp.where(sel, NEG_BIG, l)
    es = [jnp.exp(m - tops[0][0]) for m, _ in tops]
    den = es[0]
    for e in es[1:]:
        den = den + e
    comb = jnp.zeros(logits.shape, F32)
    for (m, sel), e in zip(tops, es):
        comb = jnp.where(sel, e / den, comb)
    return comb


def _ln_kernel(*refs, n_ctx, tm, alpha, route, n_experts):
    if route:
        (x_ref, y_ref, gate_ref, lng_ref, lnb_ref, sh_ref, sc_ref, wr_ref, br_ref,
         xo_ref, tok_ref, comb_ref) = refs
    else:
        x_ref, y_ref, gate_ref, lng_ref, lnb_ref, xo_ref = refs
    rows = pl.program_id(0) * tm + lax.broadcasted_iota(jnp.int32, (tm, 1), 0)
    is_ctx = rows < n_ctx
    gate = jnp.where(is_ctx, gate_ref[1:2, :], gate_ref[0:1, :])
    u = alpha * x_ref[...] + gate * y_ref[...]
    mu = jnp.mean(u, axis=-1, keepdims=True)
    uc = u - mu
    var = jnp.mean(uc * uc, axis=-1, keepdims=True)
    xn = uc * lax.rsqrt(var + 1e-5) * lng_ref[...] + lnb_ref[...]
    xo_ref[...] = xn
    if route:
        sc = jnp.where(is_ctx, sc_ref[1:2, :], sc_ref[0:1, :])
        sh = jnp.where(is_ctx, sh_ref[1:2, :], sh_ref[0:1, :])
        tok = xn * (1.0 + sc) + sh
        tok_ref[...] = tok.astype(tok_ref.dtype)
        t_hi, t_lo = _split_bf16(tok, 2)
        w_hi, w_lo = _split_bf16(wr_ref[...], 2)
        logits = (jnp.dot(t_hi, w_hi, preferred_element_type=F32)
                  + jnp.dot(t_hi, w_lo, preferred_element_type=F32)
                  + jnp.dot(t_lo, w_hi, preferred_element_type=F32)) + br_ref[...]
        comb_ref[...] = _topk_combine(logits, n_experts)


def _resid_ln(xall, y, mod, gate_blk, lng, lnb, n_ctx, alpha, route=None):
    t, d = xall.shape
    tm = _pick(t, (256, 128))
    const = lambda i: (0, 0)
    row = pl.BlockSpec((tm, d), lambda i: (i, 0))
    in_specs = [row, row, pl.BlockSpec((8, d), lambda i: (0, gate_blk)),
                pl.BlockSpec((1, d), const), pl.BlockSpec((1, d), const)]
    args = [xall, y, mod, lng, lnb]
    out_specs = [row]
    out_shape = [jax.ShapeDtypeStruct((t, d), F32)]
    n_experts = 0
    if route is not None:
        sh_blk, sc_blk, w_router, b_router, n_experts = route
        in_specs += [pl.BlockSpec((8, d), lambda i: (0, sh_blk)),
                     pl.BlockSpec((8, d), lambda i: (0, sc_blk)),
                     pl.BlockSpec((d, LANE), const), pl.BlockSpec((1, LANE), const)]
        args += [mod, mod, w_router, b_router]
        out_specs += [row, pl.BlockSpec((tm, LANE), lambda i: (i, 0))]
        out_shape += [jax.ShapeDtypeStruct((t, d), BF16), jax.ShapeDtypeStruct((t, LANE), F32)]
    return pl.pallas_call(
        functools.partial(_ln_kernel, n_ctx=n_ctx, tm=tm, alpha=alpha, route=route is not None,
                          n_experts=n_experts),
        grid=(t // tm,),
        in_specs=in_specs,
        out_specs=out_specs,
        out_shape=out_shape,
        compiler_params=_cparams(("arbitrary",)),
        name="resid_ln_route" if route is not None else "resid_ln",
    )(*args)


def _moe_kernel(tok_ref, comb_ref, wgu_ref, bgu_ref, wdn_ref, bdn_ref, o_ref):
    e = pl.program_id(1)
    lane = lax.broadcasted_iota(jnp.int32, comb_ref.shape, 1)
    ce = jnp.sum(jnp.where(lane == e, comb_ref[...], 0.0), axis=1, keepdims=True)
    de = wdn_ref.shape[0]
    gu = jnp.dot(tok_ref[...], wgu_ref[...], preferred_element_type=F32) + bgu_ref[...]
    glu = jnp.minimum(gu[:, :de], SWIGLU_LIMIT)
    lin = jnp.clip(gu[:, de:], -SWIGLU_LIMIT, SWIGLU_LIMIT)
    act = glu * _sigmoid(SWIGLU_ALPHA * glu) * (lin + 1.0)
    y = jnp.dot((act * ce).astype(BF16), wdn_ref[...], preferred_element_type=F32) + ce * bdn_ref[...]

    @pl.when(e == 0)
    def _():
        o_ref[...] = y

    @pl.when(e > 0)
    def _():
        o_ref[...] += y


def _moe(tok, comb, wgu, bgu, wdn, bdn):
    t, d = tok.shape
    ne, _, two_de = wgu.shape
    de = two_de // 2
    tm = _pick(t, (384, 256, 128))
    return pl.pallas_call(
        _moe_kernel,
        grid=(t // tm, ne),
        in_specs=[
            pl.BlockSpec((tm, d), lambda i, e: (i, 0)),
            pl.BlockSpec((tm, LANE), lambda i, e: (i, 0)),
            pl.BlockSpec((None, d, two_de), lambda i, e: (e, 0, 0)),
            pl.BlockSpec((None, 1, two_de), lambda i, e: (e, 0, 0)),
            pl.BlockSpec((None, de, d), lambda i, e: (e, 0, 0)),
            pl.BlockSpec((None, 1, d), lambda i, e: (e, 0, 0)),
        ],
        out_specs=pl.BlockSpec((tm, d), lambda i, e: (i, 0)),
        out_shape=jax.ShapeDtypeStruct((t, d), F32),
        compiler_params=_cparams(("arbitrary", "arbitrary")),
        name="moe_experts",
    )(tok, comb, wgu, bgu, wdn, bdn)


GATE_COL0 = 48


def _layout(d_model):
    d_ssd = 3 * d_model // 8
    d_att = d_model // 4
    d_mls = d_model - d_ssd - d_att
    ssd_heads = d_ssd // SSD_HEADDIM
    mls_heads = d_mls // MLS_V_DIM
    gs = SSD_GROUPS * SSD_STATE
    kvw = ATT_KV_HEADS * HEAD_DIM
    qkw = mls_heads * MLS_QK_DIM
    assert 2 * ssd_heads == GATE_COL0 and GATE_COL0 + 2 * mls_heads <= SMALL_W
    sizes = (d_ssd, d_ssd + 2 * gs, 2 * ssd_heads, d_att, kvw, kvw, 2 * qkw, d_mls, d_mls, 4 * mls_heads)
    o = np.concatenate([[0], np.cumsum(sizes)])
    src = dict(z=o[0], xbc=o[1], dt=o[2], aq=o[3], ak=o[4], av=o[5], mqk=o[6], mv=o[7], mo=o[8], mg=o[9])
    lay = dict(d_ssd=d_ssd, d_att=d_att, d_mls=d_mls, ssd_heads=ssd_heads, mls_heads=mls_heads,
               src={k: int(v) for k, v in src.items()}, gs=gs, kvw=kvw, qkw=qkw)
    cols = []

    def put(name, idx):
        lay[name] = len(cols)
        cols.extend(int(v) for v in idx)

    half = np.concatenate([np.arange(0, HEAD_DIM, 2), np.arange(1, HEAD_DIM, 2)])

    def heads_split(base, n_heads):
        return np.concatenate([base + h * HEAD_DIM + half for h in range(n_heads)])

    put("z", src["z"] + np.arange(d_ssd))
    put("mv", src["mv"] + np.arange(d_mls))
    put("mo", src["mo"] + np.arange(d_mls))
    put("mqk", src["mqk"] + np.arange(2 * qkw))
    put("xbc", src["xbc"] + np.arange(d_ssd + 2 * gs))
    put("ak", heads_split(src["ak"], ATT_KV_HEADS))
    put("av", src["av"] + np.arange(kvw))
    put("aq", heads_split(src["aq"], d_att // HEAD_DIM))
    sm1 = -np.ones(SMALL_W, np.int64)
    sm1[:2 * ssd_heads] = src["dt"] + np.arange(2 * ssd_heads)
    sm2 = -np.ones(SMALL_W, np.int64)
    for direction in range(2):
        c = GATE_COL0 + direction * mls_heads
        sm1[c:c + mls_heads] = src["mg"] + (2 * direction + 1) * mls_heads + np.arange(mls_heads)
        sm2[c:c + mls_heads] = src["mg"] + (2 * direction) * mls_heads + np.arange(mls_heads)
    put("sm1", sm1)
    put("sm2", sm2)
    n_pad = -len(cols) % 512
    cols.extend([-1] * n_pad)
    lay["cols"] = np.asarray(cols, np.int64)
    lay["conv0"] = lay["mqk"]
    lay["conv_w"] = 2 * qkw + d_ssd + 2 * gs
    lay["co_q"], lay["co_k"], lay["co_xs"] = 0, qkw, 2 * qkw
    lay["co_b"], lay["co_c"] = 2 * qkw + d_ssd, 2 * qkw + d_ssd + gs
    for name, width in (("z", d_ssd), ("mv", d_mls), ("mo", d_mls), ("aq", d_att), ("ak", kvw), ("av", kvw),
                        ("sm1", SMALL_W), ("sm2", SMALL_W), ("conv0", LANE)):
        assert lay[name] % width == 0, name
    assert lay["co_k"] % qkw == 0 and lay["co_xs"] % d_ssd == 0 and lay["co_b"] % gs == 0 and lay["co_c"] % gs == 0
    return lay


def _relayout_w_in(w, lay):
    w = w.astype(BF16)
    d = w.shape[0]
    s = lay["src"]
    sh, mh = lay["ssd_heads"], lay["mls_heads"]

    def seg(name, width):
        return w[:, s[name]:s[name] + width]

    def split_heads(a):
        n = a.shape[1] // HEAD_DIM
        return a.reshape(d, n, HEAD_DIM // 2, 2).swapaxes(2, 3).reshape(d, n * HEAD_DIM)

    zeros = lambda n: jnp.zeros((d, n), BF16)
    mg = seg("mg", 4 * mh).reshape(d, 4, mh)
    tail = SMALL_W - GATE_COL0 - 2 * mh
    parts = [seg("z", lay["d_ssd"]), seg("mv", lay["d_mls"]), seg("mo", lay["d_mls"]),
             seg("mqk", 2 * lay["qkw"]), seg("xbc", lay["d_ssd"] + 2 * lay["gs"]),
             split_heads(seg("ak", lay["kvw"])), seg("av", lay["kvw"]), split_heads(seg("aq", lay["d_att"])),
             seg("dt", 2 * sh), mg[:, 1], mg[:, 3], zeros(tail),
             zeros(GATE_COL0), mg[:, 0], mg[:, 2], zeros(tail)]
    out = jnp.concatenate(parts, axis=1)
    return jnp.concatenate([out, zeros(len(lay["cols"]) - out.shape[1])], axis=1)


def _rope_tables(n_ctx, seq):
    rows = seq // GRID_W
    row = jnp.repeat(jnp.arange(rows), GRID_W).astype(F32)
    col = jnp.tile(jnp.arange(GRID_W), rows).astype(F32)
    n_freq = HEAD_DIM // 4
    inv = ROPE_THETA ** (-jnp.arange(n_freq, dtype=F32) / n_freq)
    ang = jnp.concatenate([row[:, None] * inv, col[:, None] * inv], axis=-1)
    cos, sin = jnp.cos(ang), jnp.sin(ang)
    cos_t = jnp.concatenate([jnp.ones((n_ctx, HEAD_DIM), F32), jnp.concatenate([cos, cos], axis=-1)], axis=0)
    sin_t = jnp.concatenate([jnp.zeros((n_ctx, HEAD_DIM), F32), jnp.concatenate([-sin, sin], axis=-1)], axis=0)
    return cos_t, sin_t


def _pad_row(v, width=SMALL_W, at=0):
    out = jnp.zeros((1, width), F32)
    return out.at[0, at:at + v.shape[0]].set(v.astype(F32))


def _layer(l, xall, mod, lay, tables, n_ctx, alpha, p):
    d = xall.shape[1]
    proj = _inproj(xall, mod, _relayout_w_in(p["w_in"][l], lay), n_ctx)

    conv_w = jnp.concatenate([p["mls_conv_w"][l], p["ssd_conv_w"][l]], axis=1)
    conv_w8 = jnp.concatenate([conv_w, jnp.zeros((8 - CONV_W, conv_w.shape[1]), F32)], axis=0)
    conv_b = jnp.concatenate([p["mls_conv_b"][l], p["ssd_conv_b"][l]])[None, :]
    conv_out = _conv(proj, conv_w8, conv_b, lay["conv0"], lay["conv_w"], n_ctx)

    sh, mh = lay["ssd_heads"], lay["mls_heads"]
    gate_b = p["mls_gate_b"][l]
    bias1 = (_pad_row(p["ssd_dt_bias"][l].reshape(-1))
             + _pad_row(gate_b[1], at=GATE_COL0) + _pad_row(gate_b[3], at=GATE_COL0 + mh))
    bias2 = _pad_row(gate_b[0], at=GATE_COL0) + _pad_row(gate_b[2], at=GATE_COL0 + mh)
    alog_row = _pad_row(p["ssd_A_log"][l].reshape(-1))
    hcol = np.repeat(np.arange(sh), SSD_HEADDIM)
    e_f = jnp.asarray(np.arange(SMALL_W)[:, None] == hcol[None, :], BF16)
    e_b = jnp.asarray(np.arange(SMALL_W)[:, None] == (hcol + sh)[None, :], BF16)
    dskip_x = jnp.repeat(p["ssd_D"][l], SSD_HEADDIM)[None, :]
    y_f = _ssd_pass(conv_out, proj, lay, bias1, alog_row, e_f, n_ctx, False)
    ssd = _ssd_pass(conv_out, proj, lay, bias1, alog_row, e_b, n_ctx, True,
                    extra=(y_f, dskip_x, p["ssd_norm_g"][l][None, :]))

    h_f = _mls_pass(conv_out, proj, lay, bias1, bias2, n_ctx, False)
    mls = _mls_pass(conv_out, proj, lay, bias1, bias2, n_ctx, True, extra=(h_f, p["mls_norm_g"][l][None, :]))

    half = np.concatenate([np.arange(0, HEAD_DIM, 2), np.arange(1, HEAD_DIM, 2)])
    gq = p["att_q_norm_g"][l][half][None, :]
    gk = p["att_k_norm_g"][l][half][None, :]
    qn, kn, vb = _qk_prep(proj, lay, tables[0], tables[1], gq, gk)
    t = xall.shape[0]
    att_c = _flash(qn, kn, vb, 0, n_ctx, n_ctx)
    att_l = _flash(qn, kn, vb, n_ctx, t - n_ctx, t)
    att = jnp.concatenate([att_c, att_l], axis=0)

    w_out = p["w_out"][l].astype(BF16)
    d_ssd, d_att = lay["d_ssd"], lay["d_att"]
    mix = _outproj(ssd, att, mls, w_out[:d_ssd], w_out[d_ssd:d_ssd + d_att], w_out[d_ssd + d_att:])

    ne = p["w_router"].shape[2]
    w_router = jnp.zeros((d, LANE), F32).at[:, :ne].set(p["w_router"][l])
    b_router = _pad_row(p["b_router"][l], LANE)
    x1, tok, comb = _resid_ln(xall, mix, mod, 2, p["ln1_g"][l][None, :], p["ln1_b"][l][None, :], n_ctx, alpha,
                              route=(3, 4, w_router, b_router, ne))

    wgu = jnp.concatenate([p["w_gu"][l][:, :, 0::2], p["w_gu"][l][:, :, 1::2]], axis=-1).astype(BF16)
    bgu = jnp.concatenate([p["b_gu"][l][:, 0::2], p["b_gu"][l][:, 1::2]], axis=-1)[:, None, :]
    f = _moe(tok, comb, wgu, bgu, p["w_dn"][l].astype(BF16), p["b_dn"][l][:, None, :])
    (x2,) = _resid_ln(x1, f, mod, 5, p["ln2_g"][l][None, :], p["ln2_b"][l][None, :], n_ctx, alpha)
    aux = dict(proj=proj, conv_out=conv_out, ssd=ssd, mls=mls, att=att, mix=mix, x1=x1, tok=tok, comb=comb, f=f)
    return x2, aux


def kernel(x, c, ctx, c_ctx, w_ada, b_ada, w_in, ssd_conv_w, ssd_conv_b, ssd_A_log, ssd_dt_bias, ssd_D,
           ssd_norm_g, att_q_norm_g, att_k_norm_g, mls_conv_w, mls_conv_b, mls_gate_b, mls_norm_g, w_out,
           ln1_g, ln1_b, w_router, b_router, w_gu, b_gu, w_dn, b_dn, ln2_g, ln2_b):
    p = dict(w_in=w_in, ssd_conv_w=ssd_conv_w, ssd_conv_b=ssd_conv_b, ssd_A_log=ssd_A_log,
             ssd_dt_bias=ssd_dt_bias, ssd_D=ssd_D, ssd_norm_g=ssd_norm_g, att_q_norm_g=att_q_norm_g,
             att_k_norm_g=att_k_norm_g, mls_conv_w=mls_conv_w, mls_conv_b=mls_conv_b, mls_gate_b=mls_gate_b,
             mls_norm_g=mls_norm_g, w_out=w_out, ln1_g=ln1_g, ln1_b=ln1_b, w_router=w_router,
             b_router=b_router, w_gu=w_gu, b_gu=b_gu, w_dn=w_dn, b_dn=b_dn, ln2_g=ln2_g, ln2_b=ln2_b)
    batch, seq, d = x.shape
    assert batch == 1
    n_ctx = ctx.shape[1]
    depth = w_ada.shape[0]
    alpha = (2 * depth) ** 0.25
    lay = _layout(d)
    tables = _rope_tables(n_ctx, seq)
    cvec = jnp.zeros((8, d), F32).at[0].set(c[0]).at[1].set(c_ctx)
    mods = _ada(cvec, w_ada, b_ada)
    xall = jnp.concatenate([ctx[0], x[0]], axis=0)
    for l in range(depth):
        xall, _ = _layer(l, xall, mods[l], lay, tables, n_ctx, alpha, p)
    return xall[n_ctx:][None]
```

```python
import functools

import numpy as np
import jax
import jax.numpy as jnp
from jax import lax
from jax.experimental import pallas as pl
from jax.experimental.pallas import tpu as pltpu

F32 = jnp.float32
BF16 = jnp.bfloat16

GRID_W = 64
CHUNK = 128
CONV_W = 5
SSD_HEADDIM = 64
SSD_GROUPS = 4
SSD_STATE = 128
HEAD_DIM = 128
ATT_KV_HEADS = 2
ROPE_THETA = 10000.0
MLS_V_DIM = 256
MLS_QK_DIM = 128
TOP_K = 4
SWIGLU_LIMIT = 7.0
SWIGLU_ALPHA = 1.702

LANE = 128
SMALL_W = LANE
NEG_BIG = -1e30
VMEM_LIMIT = 56 * 1024 * 1024


def _cparams(sem, vmem=VMEM_LIMIT):
    return pltpu.CompilerParams(dimension_semantics=sem, vmem_limit_bytes=vmem)


def _pick(n, cands):
    for c in cands:
        if n % c == 0:
            return c
    raise ValueError(f"no tile in {cands} divides {n}")


def _sigmoid(x):
    return 1.0 / (1.0 + jnp.exp(-x))


def _softplus(x):
    return jnp.maximum(x, 0.0) + jnp.log(1.0 + jnp.exp(-jnp.abs(x)))


def _split_bf16(a, n):
    parts = []
    r = a
    for _ in range(n):
        p = r.astype(BF16)
        parts.append(p)
        r = r - p.astype(F32)
    return parts


def _dot01_left(m01, a, n=3):
    out = None
    for p in _split_bf16(a, n):
        t = jnp.dot(m01, p, preferred_element_type=F32)
        out = t if out is None else out + t
    return out


def _dot01_right(a, m01, n=2):
    out = None
    for p in _split_bf16(a, n):
        t = jnp.dot(p, m01, preferred_element_type=F32)
        out = t if out is None else out + t
    return out


def _ada_kernel(c_ref, w_ref, b_ref, o_ref):
    c = c_ref[...]
    s = (c * _sigmoid(c)).astype(BF16)
    o_ref[...] = jnp.dot(s, w_ref[...].astype(BF16), preferred_element_type=F32) + b_ref[...]


def _ada(cvec, w_ada, b_ada):
    depth, d, n = w_ada.shape
    tn = _pick(n, (512, 256, 128))
    return pl.pallas_call(
        _ada_kernel,
        grid=(depth, n // tn),
        in_specs=[
            pl.BlockSpec((8, d), lambda l, j: (0, 0)),
            pl.BlockSpec((None, d, tn), lambda l, j: (l, 0, j)),
            pl.BlockSpec((None, 1, tn), lambda l, j: (l, 0, j)),
        ],
        out_specs=pl.BlockSpec((None, 8, tn), lambda l, j: (l, 0, j)),
        out_shape=jax.ShapeDtypeStruct((depth, 8, n), F32),
        compiler_params=_cparams(("arbitrary", "arbitrary")),
        name="ada_mod",
    )(cvec, w_ada, b_ada.reshape(depth, 1, n))


def _inproj_kernel(x_ref, sh_ref, sc_ref, w_ref, o_ref, h_scr, *, n_ctx, tm):
    i = pl.program_id(0)
    j = pl.program_id(1)

    @pl.when(j == 0)
    def _():
        rows = i * tm + lax.broadcasted_iota(jnp.int32, (tm, 1), 0)
        is_ctx = rows < n_ctx
        sc = jnp.where(is_ctx, sc_ref[1:2, :], sc_ref[0:1, :])
        sh = jnp.where(is_ctx, sh_ref[1:2, :], sh_ref[0:1, :])
        h_scr[...] = (x_ref[...] * (1.0 + sc) + sh).astype(BF16)

    o_ref[...] = jnp.dot(h_scr[...], w_ref[...], preferred_element_type=F32)


def _inproj(xall, mod, w, n_ctx):
    t, d = xall.shape
    n = w.shape[1]
    tm = _pick(t, (768, 512, 256, 128))
    tn = _pick(n, (512, 256, 128))
    return pl.pallas_call(
        functools.partial(_inproj_kernel, n_ctx=n_ctx, tm=tm),
        grid=(t // tm, n // tn),
        in_specs=[
            pl.BlockSpec((tm, d), lambda i, j: (i, 0), pipeline_mode=pl.Buffered(1)),
            pl.BlockSpec((8, d), lambda i, j: (0, 0)),
            pl.BlockSpec((8, d), lambda i, j: (0, 1)),
            pl.BlockSpec((d, tn), lambda i, j: (0, j)),
        ],
        out_specs=pl.BlockSpec((tm, tn), lambda i, j: (i, j)),
        out_shape=jax.ShapeDtypeStruct((t, n), F32),
        scratch_shapes=[pltpu.VMEM((tm, d), BF16)],
        compiler_params=_cparams(("arbitrary", "arbitrary")),
        name="in_proj",
    )(xall, mod, mod, w)


def _conv_kernel(x_ref, w_ref, b_ref, o_ref, *, n_ctx, tt):
    t = x_ref.shape[0]
    w = w_ref[...]
    bias = b_ref[...]
    halo = 8
    pad = CONV_W // 2

    def body(i, carry):
        t0 = pl.multiple_of(i * tt, tt)
        cur = x_ref[pl.ds(t0, tt), :]
        p0 = pl.multiple_of(jnp.maximum(t0 - halo, 0), halo)
        n0 = pl.multiple_of(jnp.minimum(t0 + tt, t - halo), halo)
        prev = x_ref[pl.ds(p0, halo), :]
        nxt = x_ref[pl.ds(n0, halo), :]
        seg_start = jnp.logical_or(t0 == 0, t0 == n_ctx)
        seg_end = jnp.logical_or(t0 + tt == n_ctx, t0 + tt == t)
        prev = jnp.where(seg_start, 0.0, prev)
        nxt = jnp.where(seg_end, 0.0, nxt)
        win = jnp.concatenate([prev, cur, nxt], axis=0)
        acc = bias + w[0:1, :] * win[halo - pad:halo - pad + tt, :]
        for k in range(1, CONV_W):
            acc = acc + w[k:k + 1, :] * win[halo - pad + k:halo - pad + k + tt, :]
        o_ref[pl.ds(t0, tt), :] = acc * _sigmoid(acc)
        return carry

    lax.fori_loop(0, t // tt, body, 0)


def _conv(proj, w8, b, col0, width, n_ctx):
    t = proj.shape[0]
    tt = _pick(n_ctx, (256, 128))
    assert t % tt == 0 and col0 % LANE == 0 and width % LANE == 0
    cb0 = col0 // LANE
    return pl.pallas_call(
        functools.partial(_conv_kernel, n_ctx=n_ctx, tt=tt),
        grid=(width // LANE,),
        in_specs=[
            pl.BlockSpec((t, LANE), lambda c: (0, cb0 + c)),
            pl.BlockSpec((8, LANE), lambda c: (0, c)),
            pl.BlockSpec((1, LANE), lambda c: (0, c)),
        ],
        out_specs=pl.BlockSpec((t, LANE), lambda c: (0, c)),
        out_shape=jax.ShapeDtypeStruct((t, width), F32),
        compiler_params=_cparams(("arbitrary",)),
        name="dwconv_silu",
    )(proj, w8, b)


def _chunk_order(i, n_chunks, n_ctx_chunks, reverse):
    if not reverse:
        return i
    return jnp.where(i < n_ctx_chunks, n_ctx_chunks - 1 - i, n_chunks - 1 - (i - n_ctx_chunks))


def _tri_mask(reverse):
    row = lax.broadcasted_iota(jnp.int32, (CHUNK, CHUNK), 0)
    col = lax.broadcasted_iota(jnp.int32, (CHUNK, CHUNK), 1)
    return (col >= row) if reverse else (col <= row)


def _ssd_kernel(*refs, reverse, finish, heads, hpg):
    if finish:
        (xs_ref, b_ref, c_ref, sm_ref, bias_ref, alog_ref, exp_ref,
         z_ref, yprev_ref, dskip_ref, g_ref, o_ref, h_scr) = refs
    else:
        (xs_ref, b_ref, c_ref, sm_ref, bias_ref, alog_ref, exp_ref, o_ref, h_scr) = refs
    gw = hpg * SSD_HEADDIM

    @pl.when(pl.program_id(0) == 0)
    def _():
        h_scr[...] = jnp.zeros_like(h_scr)

    mask = _tri_mask(reverse)
    tri = mask.astype(BF16)
    p = sm_ref[...] + bias_ref[...]
    dt = _softplus(p)
    dta = dt * (-jnp.exp(alog_ref[...]))
    cum = _dot01_left(tri, dta)
    cum_t = cum.T
    tot = cum[0:1, :] if reverse else cum[CHUNK - 1:CHUNK, :]
    e01 = exp_ref[...]
    dt_x = _dot01_right(dt, e01)
    in_x = _dot01_right(jnp.exp(cum), e01)
    tail_x = _dot01_right(jnp.exp(tot - cum), e01)
    tot_x = _dot01_right(jnp.broadcast_to(jnp.exp(tot), (8, SMALL_W)), e01)[0:1, :]

    xs = xs_ref[...]
    xdt = xs * dt_x
    xdt_b = xdt.astype(BF16)
    xtail_b = (xdt * tail_x).astype(BF16)
    bm = b_ref[...]
    cm = c_ref[...]
    c0 = heads if reverse else 0
    ys = []
    for g in range(SSD_GROUPS):
        bg = bm[:, g * SSD_STATE:(g + 1) * SSD_STATE]
        cg = cm[:, g * SSD_STATE:(g + 1) * SSD_STATE].astype(BF16)
        cb = lax.dot_general(cg, bg.astype(BF16), (((1,), (1,)), ((), ())),
                             preferred_element_type=F32)
        h_t = h_scr[g]
        y_g = jnp.dot(cg, h_t.astype(BF16), preferred_element_type=F32) * in_x[:, g * gw:(g + 1) * gw]
        parts = []
        for r in range(hpg):
            h = g * hpg + r
            c = c0 + h
            seg = cum[:, c:c + 1] - cum_t[c:c + 1, :]
            decay = jnp.exp(jnp.where(mask, seg, NEG_BIG))
            m = (cb * decay).astype(BF16)
            parts.append(jnp.dot(m, xdt_b[:, h * SSD_HEADDIM:(h + 1) * SSD_HEADDIM],
                                 preferred_element_type=F32))
        ys.append(y_g + jnp.concatenate(parts, axis=1))
        h_scr[g] = h_t * tot_x[:, g * gw:(g + 1) * gw] + jnp.dot(
            bg.T.astype(BF16), xtail_b[:, g * gw:(g + 1) * gw], preferred_element_type=F32)
    y = jnp.concatenate(ys, axis=1)

    if finish:
        z = z_ref[...]
        yt = (yprev_ref[...] + y + dskip_ref[...] * xs) * (z * _sigmoid(z))
        ms = jnp.mean(yt * yt, axis=-1, keepdims=True)
        o_ref[...] = (yt * lax.rsqrt(ms + 1e-6) * g_ref[...]).astype(o_ref.dtype)
    else:
        o_ref[...] = y


def _ssd_pass(conv_out, proj, lay, bias1, alog_row, e01, n_ctx, reverse, extra=None):
    t = proj.shape[0]
    nch, ncc = t // CHUNK, n_ctx // CHUNK
    d_ssd = lay["d_ssd"]
    heads = d_ssd // SSD_HEADDIM
    gs = SSD_GROUPS * SSD_STATE
    order = functools.partial(_chunk_order, n_chunks=nch, n_ctx_chunks=ncc, reverse=reverse)
    const = lambda i: (0, 0)
    in_specs = [
        pl.BlockSpec((CHUNK, d_ssd), lambda i: (order(i), lay["co_xs"] // d_ssd)),
        pl.BlockSpec((CHUNK, gs), lambda i: (order(i), lay["co_b"] // gs)),
        pl.BlockSpec((CHUNK, gs), lambda i: (order(i), lay["co_c"] // gs)),
        pl.BlockSpec((CHUNK, SMALL_W), lambda i: (order(i), lay["sm1"] // SMALL_W)),
        pl.BlockSpec((1, SMALL_W), const),
        pl.BlockSpec((1, SMALL_W), const),
        pl.BlockSpec((SMALL_W, d_ssd), const),
    ]
    args = [conv_out, conv_out, conv_out, proj, bias1, alog_row, e01]
    finish = extra is not None
    if finish:
        y_prev, dskip_x, norm_g = extra
        in_specs += [
            pl.BlockSpec((CHUNK, d_ssd), lambda i: (order(i), lay["z"] // d_ssd)),
            pl.BlockSpec((CHUNK, d_ssd), lambda i: (order(i), 0)),
            pl.BlockSpec((1, d_ssd), const),
            pl.BlockSpec((1, d_ssd), const),
        ]
        args += [proj, y_prev, dskip_x, norm_g]
    return pl.pallas_call(
        functools.partial(_ssd_kernel, reverse=reverse, finish=finish, heads=heads,
                          hpg=heads // SSD_GROUPS),
        grid=(nch,),
        in_specs=in_specs,
        out_specs=pl.BlockSpec((CHUNK, d_ssd), lambda i: (order(i), 0)),
        out_shape=jax.ShapeDtypeStruct((t, d_ssd), BF16 if finish else F32),
        scratch_shapes=[pltpu.VMEM((SSD_GROUPS, SSD_STATE, d_ssd // SSD_GROUPS), F32)],
        compiler_params=_cparams(("arbitrary",)),
        name="ssd_bwd_finish" if finish else "ssd_fwd",
    )(*args)


def _mls_kernel(*refs, reverse, finish, heads):
    if finish:
        (q_ref, k_ref, v_ref, sm1_ref, sm2_ref, b1_ref, b2_ref,
         og_ref, hprev_ref, gain_ref, o_ref, s_scr, m_scr) = refs
    else:
        (q_ref, k_ref, v_ref, sm1_ref, sm2_ref, b1_ref, b2_ref, o_ref, s_scr, m_scr) = refs

    @pl.when(pl.program_id(0) == 0)
    def _():
        s_scr[...] = jnp.zeros_like(s_scr)
        m_scr[...] = jnp.zeros_like(m_scr)

    mask = _tri_mask(reverse)
    tri = mask.astype(BF16)
    lf = -_softplus(-(sm1_ref[...] + b1_ref[...]))
    li = sm2_ref[...] + b2_ref[...]
    b = _dot01_left(tri, lf)
    b_t = b.T
    li_t = li.T
    tot = b[0:1, :] if reverse else b[CHUNK - 1:CHUNK, :]
    m_row = m_scr[...]
    g_all = tot - b + li
    m_new = jnp.maximum(tot + m_row, jnp.max(g_all, axis=0, keepdims=True))
    wk_all = jnp.exp(g_all - m_new)
    decay_row = jnp.exp(tot + m_row - m_new)
    mprev_all = b + m_row
    m_scr[...] = m_new

    q = q_ref[...]
    k = k_ref[...] * (MLS_QK_DIM ** -0.5)
    v = v_ref[...]
    ones_col = (lax.broadcasted_iota(jnp.int32, (CHUNK, LANE), 1) == 0).astype(F32)
    c0 = lay_fcol(heads, reverse)
    outs = []
    for h in range(heads):
        c = c0 + h
        qh = q[:, h * MLS_QK_DIM:(h + 1) * MLS_QK_DIM].astype(BF16)
        kh = k[:, h * MLS_QK_DIM:(h + 1) * MLS_QK_DIM]
        khb = kh.astype(BF16)
        vext = jnp.concatenate([v[:, h * MLS_V_DIM:(h + 1) * MLS_V_DIM], ones_col], axis=1)
        dmat = jnp.where(mask, b[:, c:c + 1] - b_t[c:c + 1, :] + li_t[c:c + 1, :], NEG_BIG)
        m_prev = mprev_all[:, c:c + 1]
        m_t = jnp.maximum(m_prev, jnp.max(dmat, axis=1, keepdims=True))
        qk = lax.dot_general(qh, khb, (((1,), (1,)), ((), ())), preferred_element_type=F32)
        w = jnp.exp(dmat - m_t) * qk
        s_prev = jnp.exp(m_prev - m_t)
        s_h = s_scr[h]
        numx = (jnp.dot(w.astype(BF16), vext.astype(BF16), preferred_element_type=F32)
                + s_prev * jnp.dot(qh, s_h.astype(BF16), preferred_element_type=F32))
        den = numx[:, MLS_V_DIM:MLS_V_DIM + 1]
        outs.append(numx[:, :MLS_V_DIM] / jnp.maximum(jnp.abs(den), jnp.exp(-m_t)))
        s_scr[h] = decay_row[:, c:c + 1] * s_h + jnp.dot(
            kh.T.astype(BF16), (wk_all[:, c:c + 1] * vext).astype(BF16), preferred_element_type=F32)

    if finish:
        hp = hprev_ref[...]
        og = og_ref[...]
        gain = gain_ref[...]
        for h in range(heads):
            sl = slice(h * MLS_V_DIM, (h + 1) * MLS_V_DIM)
            hs = hp[:, sl] + outs[h]
            ms = jnp.mean(hs * hs, axis=-1, keepdims=True)
            o_ref[:, sl] = (_sigmoid(og[:, sl]) * (hs * lax.rsqrt(ms + 1e-6) * gain[:, sl])).astype(o_ref.dtype)
    else:
        for h in range(heads):
            o_ref[:, h * MLS_V_DIM:(h + 1) * MLS_V_DIM] = outs[h]


def lay_fcol(heads, reverse):
    return GATE_COL0 + (heads if reverse else 0)


def _mls_pass(conv_out, proj, lay, bias1, bias2, n_ctx, reverse, extra=None):
    t = proj.shape[0]
    nch, ncc = t // CHUNK, n_ctx // CHUNK
    d_mls = lay["d_mls"]
    heads = d_mls // MLS_V_DIM
    qw = heads * MLS_QK_DIM
    order = functools.partial(_chunk_order, n_chunks=nch, n_ctx_chunks=ncc, reverse=reverse)
    const = lambda i: (0, 0)
    in_specs = [
        pl.BlockSpec((CHUNK, qw), lambda i: (order(i), lay["co_q"] // qw)),
        pl.BlockSpec((CHUNK, qw), lambda i: (order(i), lay["co_k"] // qw)),
        pl.BlockSpec((CHUNK, d_mls), lambda i: (order(i), lay["mv"] // d_mls)),
        pl.BlockSpec((CHUNK, SMALL_W), lambda i: (order(i), lay["sm1"] // SMALL_W)),
        pl.BlockSpec((CHUNK, SMALL_W), lambda i: (order(i), lay["sm2"] // SMALL_W)),
        pl.BlockSpec((1, SMALL_W), const),
        pl.BlockSpec((1, SMALL_W), const),
    ]
    args = [conv_out, conv_out, proj, proj, proj, bias1, bias2]
    finish = extra is not None
    if finish:
        h_prev, gain = extra
        in_specs += [
            pl.BlockSpec((CHUNK, d_mls), lambda i: (order(i), lay["mo"] // d_mls)),
            pl.BlockSpec((CHUNK, d_mls), lambda i: (order(i), 0)),
            pl.BlockSpec((1, d_mls), const),
        ]
        args += [proj, h_prev, gain]
    return pl.pallas_call(
        functools.partial(_mls_kernel, reverse=reverse, finish=finish, heads=heads),
        grid=(nch,),
        in_specs=in_specs,
        out_specs=pl.BlockSpec((CHUNK, d_mls), lambda i: (order(i), 0)),
        out_shape=jax.ShapeDtypeStruct((t, d_mls), BF16 if finish else F32),
        scratch_shapes=[pltpu.VMEM((heads, MLS_QK_DIM, MLS_V_DIM + LANE), F32),
                        pltpu.VMEM((1, SMALL_W), F32)],
        compiler_params=_cparams(("arbitrary",)),
        name="mlstm_bwd_finish" if finish else "mlstm_fwd",
    )(*args)


def _qk_prep_kernel(q_ref, k_ref, v_ref, cos_ref, sin_ref, gq_ref, gk_ref, qo_ref, ko_ref, vo_ref):
    cos = cos_ref[...]
    sin = sin_ref[...]

    def norm_rope(xh, g, scale):
        ms = jnp.mean(xh * xh, axis=-1, keepdims=True)
        xn = xh * lax.rsqrt(ms + 1e-6) * g
        return (xn * cos + pltpu.roll(xn, HEAD_DIM // 2, axis=1) * sin) * scale

    q = q_ref[...]
    for h in range(q.shape[1] // HEAD_DIM):
        sl = slice(h * HEAD_DIM, (h + 1) * HEAD_DIM)
        qo_ref[:, sl] = norm_rope(q[:, sl], gq_ref[...], HEAD_DIM ** -0.5).astype(qo_ref.dtype)
    k = k_ref[...]
    for h in range(k.shape[1] // HEAD_DIM):
        sl = slice(h * HEAD_DIM, (h + 1) * HEAD_DIM)
        ko_ref[:, sl] = norm_rope(k[:, sl], gk_ref[...], 1.0).astype(ko_ref.dtype)
    vo_ref[...] = v_ref[...].astype(vo_ref.dtype)


def _qk_prep(proj, lay, cos_t, sin_t, gq, gk):
    t = proj.shape[0]
    tm = _pick(t, (256, 128))
    d_att = lay["d_att"]
    kvw = ATT_KV_HEADS * HEAD_DIM
    const = lambda i: (0, 0)
    return pl.pallas_call(
        _qk_prep_kernel,
        grid=(t // tm,),
        in_specs=[
            pl.BlockSpec((tm, d_att), lambda i: (i, lay["aq"] // d_att)),
            pl.BlockSpec((tm, kvw), lambda i: (i, lay["ak"] // kvw)),
            pl.BlockSpec((tm, kvw), lambda i: (i, lay["av"] // kvw)),
            pl.BlockSpec((tm, HEAD_DIM), lambda i: (i, 0)),
            pl.BlockSpec((tm, HEAD_DIM), lambda i: (i, 0)),
            pl.BlockSpec((1, HEAD_DIM), const),
            pl.BlockSpec((1, HEAD_DIM), const),
        ],
        out_specs=[
            pl.BlockSpec((tm, d_att), lambda i: (i, 0)),
            pl.BlockSpec((tm, kvw), lambda i: (i, 0)),
            pl.BlockSpec((tm, kvw), lambda i: (i, 0)),
        ],
        out_shape=[jax.ShapeDtypeStruct((t, d_att), BF16),
                   jax.ShapeDtypeStruct((t, kvw), BF16),
                   jax.ShapeDtypeStruct((t, kvw), BF16)],
        compiler_params=_cparams(("arbitrary",)),
        name="qk_norm_rope",
    )(proj, proj, proj, cos_t, sin_t, gq, gk)


def _flash_kernel(q_ref, k_ref, v_ref, o_ref, m_scr, l_scr, acc_scr, *, group):
    j = pl.program_id(2)

    @pl.when(j == 0)
    def _():
        m_scr[...] = jnp.full_like(m_scr, NEG_BIG)
        l_scr[...] = jnp.zeros_like(l_scr)
        acc_scr[...] = jnp.zeros_like(acc_scr)

    k = k_ref[...]
    v = v_ref[...]
    for h in range(group):
        qh = q_ref[:, h * HEAD_DIM:(h + 1) * HEAD_DIM]
        s = lax.dot_general(qh, k, (((1,), (1,)), ((), ())), preferred_element_type=F32)
        m_old = m_scr[h]
        m_new = jnp.maximum(m_old, jnp.max(s, axis=1, keepdims=True))
        p = jnp.exp(s - m_new)
        alpha = jnp.exp(m_old - m_new)
        l_scr[h] = alpha * l_scr[h] + jnp.sum(p, axis=1, keepdims=True)
        acc_scr[h] = alpha * acc_scr[h] + jnp.dot(p.astype(BF16), v, preferred_element_type=F32)
        m_scr[h] = m_new

    @pl.when(j == pl.num_programs(2) - 1)
    def _():
        for h in range(group):
            o_ref[:, h * HEAD_DIM:(h + 1) * HEAD_DIM] = (acc_scr[h] / l_scr[h]).astype(o_ref.dtype)


def _flash(qn, kn, vb, q_row0, n_q, n_k):
    d_att = qn.shape[1]
    group = d_att // HEAD_DIM // ATT_KV_HEADS
    gw = group * HEAD_DIM
    tq = _pick(n_q, (256, 128))
    tk = _pick(n_k, (768, 512, 256, 128))
    assert q_row0 % tq == 0
    qb0 = q_row0 // tq
    return pl.pallas_call(
        functools.partial(_flash_kernel, group=group),
        grid=(ATT_KV_HEADS, n_q // tq, n_k // tk),
        in_specs=[
            pl.BlockSpec((tq, gw), lambda g, i, j: (qb0 + i, g)),
            pl.BlockSpec((tk, HEAD_DIM), lambda g, i, j: (j, g)),
            pl.BlockSpec((tk, HEAD_DIM), lambda g, i, j: (j, g)),
        ],
        out_specs=pl.BlockSpec((tq, gw), lambda g, i, j: (i, g)),
        out_shape=jax.ShapeDtypeStruct((n_q, d_att), BF16),
        scratch_shapes=[pltpu.VMEM((group, tq, 1), F32),
                        pltpu.VMEM((group, tq, 1), F32),
                        pltpu.VMEM((group, tq, HEAD_DIM), F32)],
        compiler_params=_cparams(("arbitrary", "arbitrary", "arbitrary")),
        name="flash_gqa",
    )(qn, kn, vb)


def _outproj_kernel(a1_ref, a2_ref, a3_ref, w1_ref, w2_ref, w3_ref, o_ref):
    acc = jnp.dot(a1_ref[...], w1_ref[...], preferred_element_type=F32)
    acc = acc + jnp.dot(a2_ref[...], w2_ref[...], preferred_element_type=F32)
    acc = acc + jnp.dot(a3_ref[...], w3_ref[...], preferred_element_type=F32)
    o_ref[...] = acc


def _outproj(a1, a2, a3, w1, w2, w3):
    t = a1.shape[0]
    n = w1.shape[1]
    tm = _pick(t, (768, 512, 256, 128))
    tn = _pick(n, (512, 256, 128))
    lhs = lambda a: pl.BlockSpec((tm, a.shape[1]), lambda i, j: (i, 0))
    rhs = lambda w: pl.BlockSpec((w.shape[0], tn), lambda i, j: (0, j))
    return pl.pallas_call(
        _outproj_kernel,
        grid=(t // tm, n // tn),
        in_specs=[lhs(a1), lhs(a2), lhs(a3), rhs(w1), rhs(w2), rhs(w3)],
        out_specs=pl.BlockSpec((tm, tn), lambda i, j: (i, j)),
        out_shape=jax.ShapeDtypeStruct((t, n), F32),
        compiler_params=_cparams(("arbitrary", "arbitrary")),
        name="out_proj",
    )(a1, a2, a3, w1, w2, w3)


def _topk_route(logits, n_experts):
    lane = lax.broadcasted_iota(jnp.int32, logits.shape, 1)
    l = jnp.where(lane < n_experts, logits, NEG_BIG)
    tops = []
    for _ in range(TOP_K):
        m = jnp.max(l, axis=1, keepdims=True)
        idx = jnp.min(jnp.where(l == m, lane, LANE), axis=1, keepdims=True)
        tops.append((m, idx))
        l = jnp.where(lane == idx, NEG_BIG, l)
    es = [jnp.exp(m - tops[0][0]) for m, _ in tops]
    den = es[0]
    for e in es[1:]:
        den = den + e
    wts = jnp.zeros(logits.shape, F32)
    ids = jnp.zeros(logits.shape, jnp.int32)
    for k, ((m, idx), e) in enumerate(zip(tops, es)):
        wts = jnp.where(lane == k, e / den, wts)
        ids = jnp.where(lane == k, idx, ids)
    return wts, ids


def _ln_kernel(*refs, n_ctx, tm, alpha, route, n_experts):
    if route:
        (x_ref, y_ref, gate_ref, lng_ref, lnb_ref, sh_ref, sc_ref, wr_ref, br_ref,
         xo_ref, tok_ref, wts_ref, ids_ref) = refs
    else:
        x_ref, y_ref, gate_ref, lng_ref, lnb_ref, xo_ref = refs
    rows = pl.program_id(0) * tm + lax.broadcasted_iota(jnp.int32, (tm, 1), 0)
    is_ctx = rows < n_ctx
    gate = jnp.where(is_ctx, gate_ref[1:2, :], gate_ref[0:1, :])
    u = alpha * x_ref[...] + gate * y_ref[...]
    mu = jnp.mean(u, axis=-1, keepdims=True)
    uc = u - mu
    var = jnp.mean(uc * uc, axis=-1, keepdims=True)
    xn = uc * lax.rsqrt(var + 1e-5) * lng_ref[...] + lnb_ref[...]
    xo_ref[...] = xn
    if route:
        sc = jnp.where(is_ctx, sc_ref[1:2, :], sc_ref[0:1, :])
        sh = jnp.where(is_ctx, sh_ref[1:2, :], sh_ref[0:1, :])
        tok = xn * (1.0 + sc) + sh
        tok_ref[...] = tok.astype(tok_ref.dtype)
        t_hi, t_lo = _split_bf16(tok, 2)
        w_hi, w_lo = _split_bf16(wr_ref[...], 2)
        logits = (jnp.dot(t_hi, w_hi, preferred_element_type=F32)
                  + jnp.dot(t_hi, w_lo, preferred_element_type=F32)
                  + jnp.dot(t_lo, w_hi, preferred_element_type=F32)) + br_ref[...]
        wts_ref[...], ids_ref[...] = _topk_route(logits, n_experts)


def _resid_ln(xall, y, mod, gate_blk, lng, lnb, n_ctx, alpha, route=None):
    t, d = xall.shape
    tm = _pick(t, (256, 128))
    const = lambda i: (0, 0)
    row = pl.BlockSpec((tm, d), lambda i: (i, 0))
    in_specs = [row, row, pl.BlockSpec((8, d), lambda i: (0, gate_blk)),
                pl.BlockSpec((1, d), const), pl.BlockSpec((1, d), const)]
    args = [xall, y, mod, lng, lnb]
    out_specs = [row]
    out_shape = [jax.ShapeDtypeStruct((t, d), F32)]
    n_experts = 0
    if route is not None:
        sh_blk, sc_blk, w_router, b_router, n_experts = route
        in_specs += [pl.BlockSpec((8, d), lambda i: (0, sh_blk)),
                     pl.BlockSpec((8, d), lambda i: (0, sc_blk)),
                     pl.BlockSpec((d, LANE), const), pl.BlockSpec((1, LANE), const)]
        args += [mod, mod, w_router, b_router]
        lanes = pl.BlockSpec((tm, LANE), lambda i: (i, 0))
        out_specs += [row, lanes, lanes]
        out_shape += [jax.ShapeDtypeStruct((t, d), F32), jax.ShapeDtypeStruct((t, LANE), F32),
                      jax.ShapeDtypeStruct((t, LANE), jnp.int32)]
    return pl.pallas_call(
        functools.partial(_ln_kernel, n_ctx=n_ctx, tm=tm, alpha=alpha, route=route is not None,
                          n_experts=n_experts),
        grid=(t // tm,),
        in_specs=in_specs,
        out_specs=out_specs,
        out_shape=out_shape,
        compiler_params=_cparams(("arbitrary",)),
        name="resid_ln_route" if route is not None else "resid_ln",
    )(*args)


MOE_ROW_TILE = 256
COMBINE_TILE = 128


def _route_plan(ids, n_experts, tm):
    t = ids.shape[0]
    n_rows = t * TOP_K
    flat = ids[:, :TOP_K].reshape(-1)
    onehot = (flat[:, None] == jnp.arange(n_experts, dtype=jnp.int32)[None, :]).astype(jnp.int32)
    csum = jnp.cumsum(onehot, axis=0)
    rank = jnp.sum((csum - onehot) * onehot, axis=1)
    cnt = csum[-1]
    ptiles = (cnt + tm - 1) // tm
    tile_end = jnp.cumsum(ptiles)
    row_start = (tile_end - ptiles) * tm
    pos = (jnp.sum(onehot * row_start[None, :], axis=1) + rank).astype(jnp.int32)
    n_tiles = (n_rows + n_experts * (tm - 1)) // tm
    row_token = jnp.zeros((n_tiles * tm,), jnp.int32).at[pos].set(
        jnp.arange(n_rows, dtype=jnp.int32) // TOP_K)
    tiles = jnp.arange(n_tiles, dtype=jnp.int32)
    tile_expert = jnp.minimum(jnp.sum((tiles[:, None] >= tile_end[None, :]).astype(jnp.int32), axis=1),
                              n_experts - 1).astype(jnp.int32)
    tile_valid = (tiles < tile_end[-1]).astype(jnp.int32)
    return pos, row_token, tile_expert, tile_valid, n_tiles


def _moe_kernel(te_ref, tv_ref, rt_ref, tok_hbm, wgu_ref, bgu_ref, wdn_ref, bdn_ref, sel_ref, o_ref,
                buf, sem, *, tm, n_tiles):
    i = pl.program_id(0)

    def issue(tile, slot):
        def body(r, carry):
            row = rt_ref[tile * tm + r]
            pltpu.make_async_copy(tok_hbm.at[pl.ds(row, 1), :], buf.at[slot, pl.ds(r, 1), :],
                                  sem.at[slot]).start()
            return carry
        lax.fori_loop(0, tm, body, 0, unroll=8)

    @pl.when(jnp.logical_and(i == 0, tv_ref[0] > 0))
    def _():
        issue(0, 0)

    nxt = jnp.minimum(i + 1, n_tiles - 1)

    @pl.when(jnp.logical_and(i + 1 < n_tiles, tv_ref[nxt] > 0))
    def _():
        issue(nxt, nxt % 2)

    slot = i % 2

    @pl.when(tv_ref[i] > 0)
    def _():
        pltpu.make_async_copy(tok_hbm.at[pl.ds(0, tm), :], buf.at[slot], sem.at[slot]).wait()
        x = buf[slot].astype(BF16)
        gu = jnp.dot(x, wgu_ref[...], preferred_element_type=F32) + bgu_ref[...]
        g = jnp.minimum(gu, SWIGLU_LIMIT)
        a = g * _sigmoid(SWIGLU_ALPHA * g)
        lin1 = jnp.clip(gu, -SWIGLU_LIMIT, SWIGLU_LIMIT) + 1.0
        pair = (a * pltpu.roll(lin1, gu.shape[1] - 1, axis=1)).astype(BF16)
        act = jnp.dot(pair, sel_ref[...], preferred_element_type=F32).astype(BF16)
        o_ref[...] = jnp.dot(act, wdn_ref[...], preferred_element_type=F32) + bdn_ref[...]

    @pl.when(tv_ref[i] == 0)
    def _():
        o_ref[...] = jnp.zeros_like(o_ref)


def _moe_experts(l, tok, plan, wgu, bgu, wdn, bdn):
    _, row_token, tile_expert, tile_valid, n_tiles = plan
    tm = MOE_ROW_TILE
    d = tok.shape[1]
    two_de, de = wgu.shape[3], wdn.shape[2]
    sel = jnp.asarray(np.arange(two_de)[:, None] == 2 * np.arange(de)[None, :], BF16)
    grid_spec = pltpu.PrefetchScalarGridSpec(
        num_scalar_prefetch=3,
        grid=(n_tiles,),
        in_specs=[
            pl.BlockSpec(memory_space=pl.ANY),
            pl.BlockSpec((None, None, d, two_de), lambda i, te, tv, rt: (l, te[i], 0, 0)),
            pl.BlockSpec((None, None, 1, two_de), lambda i, te, tv, rt: (l, te[i], 0, 0)),
            pl.BlockSpec((None, None, de, d), lambda i, te, tv, rt: (l, te[i], 0, 0)),
            pl.BlockSpec((None, None, 1, d), lambda i, te, tv, rt: (l, te[i], 0, 0)),
            pl.BlockSpec((two_de, de), lambda i, te, tv, rt: (0, 0)),
        ],
        out_specs=pl.BlockSpec((tm, d), lambda i, te, tv, rt: (i, 0)),
        scratch_shapes=[pltpu.VMEM((2, tm, d), F32), pltpu.SemaphoreType.DMA((2,))],
    )
    return pl.pallas_call(
        functools.partial(_moe_kernel, tm=tm, n_tiles=n_tiles),
        grid_spec=grid_spec,
        out_shape=jax.ShapeDtypeStruct((n_tiles * tm, d), F32),
        compiler_params=_cparams(("arbitrary",)),
        name="moe_experts",
    )(tile_expert, tile_valid, row_token, tok, wgu, bgu, wdn, bdn, sel)


def _combine_ln_kernel(pos_ref, y_hbm, x_ref, wts_ref, gate_ref, lng_ref, lnb_ref, xo_ref, buf, sem,
                       *, n_ctx, tm, alpha, n_steps):
    i = pl.program_id(0)

    def issue(tile, slot):
        def body(r, carry):
            for k in range(TOP_K):
                row = pos_ref[(tile * tm + r) * TOP_K + k]
                pltpu.make_async_copy(y_hbm.at[pl.ds(row, 1), :], buf.at[slot, k, pl.ds(r, 1), :],
                                      sem.at[slot]).start()
            return carry
        lax.fori_loop(0, tm, body, 0, unroll=4)

    @pl.when(i == 0)
    def _():
        issue(0, 0)

    @pl.when(i + 1 < n_steps)
    def _():
        issue(i + 1, (i + 1) % 2)

    slot = i % 2
    for k in range(TOP_K):
        pltpu.make_async_copy(y_hbm.at[pl.ds(0, tm), :], buf.at[slot, k], sem.at[slot]).wait()
    wts = wts_ref[...]
    f = wts[:, 0:1] * buf[slot, 0]
    for k in range(1, TOP_K):
        f = f + wts[:, k:k + 1] * buf[slot, k]
    rows = i * tm + lax.broadcasted_iota(jnp.int32, (tm, 1), 0)
    gate = jnp.where(rows < n_ctx, gate_ref[1:2, :], gate_ref[0:1, :])
    u = alpha * x_ref[...] + gate * f
    mu = jnp.mean(u, axis=-1, keepdims=True)
    uc = u - mu
    var = jnp.mean(uc * uc, axis=-1, keepdims=True)
    xo_ref[...] = uc * lax.rsqrt(var + 1e-5) * lng_ref[...] + lnb_ref[...]


def _combine_ln(x1, y_sorted, pos, wts, mod, gate_blk, lng, lnb, n_ctx, alpha):
    t, d = x1.shape
    tm = COMBINE_TILE
    n_steps = t // tm
    const = lambda i, p: (0, 0)
    grid_spec = pltpu.PrefetchScalarGridSpec(
        num_scalar_prefetch=1,
        grid=(n_steps,),
        in_specs=[
            pl.BlockSpec(memory_space=pl.ANY),
            pl.BlockSpec((tm, d), lambda i, p: (i, 0)),
            pl.BlockSpec((tm, LANE), lambda i, p: (i, 0)),
            pl.BlockSpec((8, d), lambda i, p: (0, gate_blk)),
            pl.BlockSpec((1, d), const),
            pl.BlockSpec((1, d), const),
        ],
        out_specs=pl.BlockSpec((tm, d), lambda i, p: (i, 0)),
        scratch_shapes=[pltpu.VMEM((2, TOP_K, tm, d), F32), pltpu.SemaphoreType.DMA((2,))],
    )
    return pl.pallas_call(
        functools.partial(_combine_ln_kernel, n_ctx=n_ctx, tm=tm, alpha=alpha, n_steps=n_steps),
        grid_spec=grid_spec,
        out_shape=jax.ShapeDtypeStruct((t, d), F32),
        compiler_params=_cparams(("arbitrary",)),
        name="moe_combine_ln",
    )(pos, y_sorted, x1, wts, mod, lng, lnb)


GATE_COL0 = 48


def _layout(d_model):
    d_ssd = 3 * d_model // 8
    d_att = d_model // 4
    d_mls = d_model - d_ssd - d_att
    ssd_heads = d_ssd // SSD_HEADDIM
    mls_heads = d_mls // MLS_V_DIM
    gs = SSD_GROUPS * SSD_STATE
    kvw = ATT_KV_HEADS * HEAD_DIM
    qkw = mls_heads * MLS_QK_DIM
    assert 2 * ssd_heads == GATE_COL0 and GATE_COL0 + 2 * mls_heads <= SMALL_W
    sizes = (d_ssd, d_ssd + 2 * gs, 2 * ssd_heads, d_att, kvw, kvw, 2 * qkw, d_mls, d_mls, 4 * mls_heads)
    o = np.concatenate([[0], np.cumsum(sizes)])
    src = dict(z=o[0], xbc=o[1], dt=o[2], aq=o[3], ak=o[4], av=o[5], mqk=o[6], mv=o[7], mo=o[8], mg=o[9])
    lay = dict(d_ssd=d_ssd, d_att=d_att, d_mls=d_mls, ssd_heads=ssd_heads, mls_heads=mls_heads,
               src={k: int(v) for k, v in src.items()}, gs=gs, kvw=kvw, qkw=qkw)
    cols = []

    def put(name, idx):
        lay[name] = len(cols)
        cols.extend(int(v) for v in idx)

    half = np.concatenate([np.arange(0, HEAD_DIM, 2), np.arange(1, HEAD_DIM, 2)])

    def heads_split(base, n_heads):
        return np.concatenate([base + h * HEAD_DIM + half for h in range(n_heads)])

    put("z", src["z"] + np.arange(d_ssd))
    put("mv", src["mv"] + np.arange(d_mls))
    put("mo", src["mo"] + np.arange(d_mls))
    put("mqk", src["mqk"] + np.arange(2 * qkw))
    put("xbc", src["xbc"] + np.arange(d_ssd + 2 * gs))
    put("ak", heads_split(src["ak"], ATT_KV_HEADS))
    put("av", src["av"] + np.arange(kvw))
    put("aq", heads_split(src["aq"], d_att // HEAD_DIM))
    sm1 = -np.ones(SMALL_W, np.int64)
    sm1[:2 * ssd_heads] = src["dt"] + np.arange(2 * ssd_heads)
    sm2 = -np.ones(SMALL_W, np.int64)
    for direction in range(2):
        c = GATE_COL0 + direction * mls_heads
        sm1[c:c + mls_heads] = src["mg"] + (2 * direction + 1) * mls_heads + np.arange(mls_heads)
        sm2[c:c + mls_heads] = src["mg"] + (2 * direction) * mls_heads + np.arange(mls_heads)
    put("sm1", sm1)
    put("sm2", sm2)
    n_pad = -len(cols) % 512
    cols.extend([-1] * n_pad)
    lay["cols"] = np.asarray(cols, np.int64)
    lay["conv0"] = lay["mqk"]
    lay["conv_w"] = 2 * qkw + d_ssd + 2 * gs
    lay["co_q"], lay["co_k"], lay["co_xs"] = 0, qkw, 2 * qkw
    lay["co_b"], lay["co_c"] = 2 * qkw + d_ssd, 2 * qkw + d_ssd + gs
    for name, width in (("z", d_ssd), ("mv", d_mls), ("mo", d_mls), ("aq", d_att), ("ak", kvw), ("av", kvw),
                        ("sm1", SMALL_W), ("sm2", SMALL_W), ("conv0", LANE)):
        assert lay[name] % width == 0, name
    assert lay["co_k"] % qkw == 0 and lay["co_xs"] % d_ssd == 0 and lay["co_b"] % gs == 0 and lay["co_c"] % gs == 0
    return lay


def _relayout_w_in(w, lay):
    w = w.astype(BF16)
    d = w.shape[0]
    s = lay["src"]
    sh, mh = lay["ssd_heads"], lay["mls_heads"]

    def seg(name, width):
        return w[:, s[name]:s[name] + width]

    def split_heads(a):
        n = a.shape[1] // HEAD_DIM
        return a.reshape(d, n, HEAD_DIM // 2, 2).swapaxes(2, 3).reshape(d, n * HEAD_DIM)

    zeros = lambda n: jnp.zeros((d, n), BF16)
    mg = seg("mg", 4 * mh).reshape(d, 4, mh)
    tail = SMALL_W - GATE_COL0 - 2 * mh
    parts = [seg("z", lay["d_ssd"]), seg("mv", lay["d_mls"]), seg("mo", lay["d_mls"]),
             seg("mqk", 2 * lay["qkw"]), seg("xbc", lay["d_ssd"] + 2 * lay["gs"]),
             split_heads(seg("ak", lay["kvw"])), seg("av", lay["kvw"]), split_heads(seg("aq", lay["d_att"])),
             seg("dt", 2 * sh), mg[:, 1], mg[:, 3], zeros(tail),
             zeros(GATE_COL0), mg[:, 0], mg[:, 2], zeros(tail)]
    out = jnp.concatenate(parts, axis=1)
    return jnp.concatenate([out, zeros(len(lay["cols"]) - out.shape[1])], axis=1)


def _rope_tables(n_ctx, seq):
    rows = seq // GRID_W
    row = jnp.repeat(jnp.arange(rows), GRID_W).astype(F32)
    col = jnp.tile(jnp.arange(GRID_W), rows).astype(F32)
    n_freq = HEAD_DIM // 4
    inv = ROPE_THETA ** (-jnp.arange(n_freq, dtype=F32) / n_freq)
    ang = jnp.concatenate([row[:, None] * inv, col[:, None] * inv], axis=-1)
    cos, sin = jnp.cos(ang), jnp.sin(ang)
    cos_t = jnp.concatenate([jnp.ones((n_ctx, HEAD_DIM), F32), jnp.concatenate([cos, cos], axis=-1)], axis=0)
    sin_t = jnp.concatenate([jnp.zeros((n_ctx, HEAD_DIM), F32), jnp.concatenate([-sin, sin], axis=-1)], axis=0)
    return cos_t, sin_t


def _pad_row(v, width=SMALL_W, at=0):
    out = jnp.zeros((1, width), F32)
    return out.at[0, at:at + v.shape[0]].set(v.astype(F32))


def _layer(l, xall, mod, lay, tables, n_ctx, alpha, p):
    d = xall.shape[1]
    proj = _inproj(xall, mod, _relayout_w_in(p["w_in"][l], lay), n_ctx)

    conv_w = jnp.concatenate([p["mls_conv_w"][l], p["ssd_conv_w"][l]], axis=1)
    conv_w8 = jnp.concatenate([conv_w, jnp.zeros((8 - CONV_W, conv_w.shape[1]), F32)], axis=0)
    conv_b = jnp.concatenate([p["mls_conv_b"][l], p["ssd_conv_b"][l]])[None, :]
    conv_out = _conv(proj, conv_w8, conv_b, lay["conv0"], lay["conv_w"], n_ctx)

    sh, mh = lay["ssd_heads"], lay["mls_heads"]
    gate_b = p["mls_gate_b"][l]
    bias1 = (_pad_row(p["ssd_dt_bias"][l].reshape(-1))
             + _pad_row(gate_b[1], at=GATE_COL0) + _pad_row(gate_b[3], at=GATE_COL0 + mh))
    bias2 = _pad_row(gate_b[0], at=GATE_COL0) + _pad_row(gate_b[2], at=GATE_COL0 + mh)
    alog_row = _pad_row(p["ssd_A_log"][l].reshape(-1))
    hcol = np.repeat(np.arange(sh), SSD_HEADDIM)
    e_f = jnp.asarray(np.arange(SMALL_W)[:, None] == hcol[None, :], BF16)
    e_b = jnp.asarray(np.arange(SMALL_W)[:, None] == (hcol + sh)[None, :], BF16)
    dskip_x = jnp.repeat(p["ssd_D"][l], SSD_HEADDIM)[None, :]
    y_f = _ssd_pass(conv_out, proj, lay, bias1, alog_row, e_f, n_ctx, False)
    ssd = _ssd_pass(conv_out, proj, lay, bias1, alog_row, e_b, n_ctx, True,
                    extra=(y_f, dskip_x, p["ssd_norm_g"][l][None, :]))

    h_f = _mls_pass(conv_out, proj, lay, bias1, bias2, n_ctx, False)
    mls = _mls_pass(conv_out, proj, lay, bias1, bias2, n_ctx, True, extra=(h_f, p["mls_norm_g"][l][None, :]))

    half = np.concatenate([np.arange(0, HEAD_DIM, 2), np.arange(1, HEAD_DIM, 2)])
    gq = p["att_q_norm_g"][l][half][None, :]
    gk = p["att_k_norm_g"][l][half][None, :]
    qn, kn, vb = _qk_prep(proj, lay, tables[0], tables[1], gq, gk)
    t = xall.shape[0]
    att_c = _flash(qn, kn, vb, 0, n_ctx, n_ctx)
    att_l = _flash(qn, kn, vb, n_ctx, t - n_ctx, t)
    att = jnp.concatenate([att_c, att_l], axis=0)

    w_out = p["w_out"][l].astype(BF16)
    d_ssd, d_att = lay["d_ssd"], lay["d_att"]
    mix = _outproj(ssd, att, mls, w_out[:d_ssd], w_out[d_ssd:d_ssd + d_att], w_out[d_ssd + d_att:])

    ne = p["w_router"].shape[2]
    w_router = jnp.zeros((d, LANE), F32).at[:, :ne].set(p["w_router"][l])
    b_router = _pad_row(p["b_router"][l], LANE)
    x1, tok, wts, ids = _resid_ln(xall, mix, mod, 2, p["ln1_g"][l][None, :], p["ln1_b"][l][None, :], n_ctx,
                                  alpha, route=(3, 4, w_router, b_router, ne))

    plan = _route_plan(ids, ne, MOE_ROW_TILE)
    y_sorted = _moe_experts(l, tok, plan, p["w_gu_bf16"], p["b_gu"][:, :, None, :], p["w_dn_bf16"],
                            p["b_dn"][:, :, None, :])
    x2 = _combine_ln(x1, y_sorted, plan[0], wts, mod, 5, p["ln2_g"][l][None, :], p["ln2_b"][l][None, :],
                     n_ctx, alpha)
    aux = dict(proj=proj, conv_out=conv_out, ssd=ssd, mls=mls, att=att, mix=mix, x1=x1, tok=tok, wts=wts,
               ids=ids, y_sorted=y_sorted, pos=plan[0])
    return x2, aux


def kernel(x, c, ctx, c_ctx, w_ada, b_ada, w_in, ssd_conv_w, ssd_conv_b, ssd_A_log, ssd_dt_bias, ssd_D,
           ssd_norm_g, att_q_norm_g, att_k_norm_g, mls_conv_w, mls_conv_b, mls_gate_b, mls_norm_g, w_out,
           ln1_g, ln1_b, w_router, b_router, w_gu, b_gu, w_dn, b_dn, ln2_g, ln2_b):
    p = dict(w_in=w_in, ssd_conv_w=ssd_conv_w, ssd_conv_b=ssd_conv_b, ssd_A_log=ssd_A_log,
             ssd_dt_bias=ssd_dt_bias, ssd_D=ssd_D, ssd_norm_g=ssd_norm_g, att_q_norm_g=att_q_norm_g,
             att_k_norm_g=att_k_norm_g, mls_conv_w=mls_conv_w, mls_conv_b=mls_conv_b, mls_gate_b=mls_gate_b,
             mls_norm_g=mls_norm_g, w_out=w_out, ln1_g=ln1_g, ln1_b=ln1_b, w_router=w_router,
             b_router=b_router, w_gu=w_gu, b_gu=b_gu, w_dn=w_dn, b_dn=b_dn, ln2_g=ln2_g, ln2_b=ln2_b)
    p["w_gu_bf16"] = w_gu.astype(BF16)
    p["w_dn_bf16"] = w_dn.astype(BF16)
    batch, seq, d = x.shape
    assert batch == 1
    n_ctx = ctx.shape[1]
    depth = w_ada.shape[0]
    alpha = (2 * depth) ** 0.25
    lay = _layout(d)
    tables = _rope_tables(n_ctx, seq)
    cvec = jnp.zeros((8, d), F32).at[0].set(c[0]).at[1].set(c_ctx)
    mods = _ada(cvec, w_ada, b_ada)
    xall = jnp.concatenate([ctx[0], x[0]], axis=0)
    for l in range(depth):
        xall, _ = _layer(l, xall, mods[l], lay, tables, n_ctx, alpha, p)
    return xall[n_ctx:][None]
```

```python
import functools

import numpy as np
import jax
import jax.numpy as jnp
from jax import lax
from jax.experimental import pallas as pl
from jax.experimental.pallas import tpu as pltpu

F32 = jnp.float32
BF16 = jnp.bfloat16

GRID_W = 64
CHUNK = 128
CONV_W = 5
SSD_HEADDIM = 64
SSD_GROUPS = 4
SSD_STATE = 128
HEAD_DIM = 128
ATT_KV_HEADS = 2
ROPE_THETA = 10000.0
MLS_V_DIM = 256
MLS_QK_DIM = 128
TOP_K = 4
SWIGLU_LIMIT = 7.0
SWIGLU_ALPHA = 1.702

LANE = 128
SMALL_W = LANE
NEG_BIG = -1e30
LOG2_E = 1.4426950408889634
VMEM_LIMIT = 56 * 1024 * 1024


def _cparams(sem, vmem=VMEM_LIMIT):
    return pltpu.CompilerParams(dimension_semantics=sem, vmem_limit_bytes=vmem)


def _pick(n, cands):
    for c in cands:
        if n % c == 0:
            return c
    raise ValueError(f"no tile in {cands} divides {n}")


def _sigmoid(x):
    return 1.0 / (1.0 + jnp.exp(-x))


def _softplus(x):
    return jnp.maximum(x, 0.0) + jnp.log(1.0 + jnp.exp(-jnp.abs(x)))


def _split_bf16(a, n):
    parts = []
    r = a
    for _ in range(n):
        p = r.astype(BF16)
        parts.append(p)
        r = r - p.astype(F32)
    return parts


def _dot01_left(m01, a, n=3):
    out = None
    for p in _split_bf16(a, n):
        t = jnp.dot(m01, p, preferred_element_type=F32)
        out = t if out is None else out + t
    return out


def _dot01_right(a, m01, n=2):
    out = None
    for p in _split_bf16(a, n):
        t = jnp.dot(p, m01, preferred_element_type=F32)
        out = t if out is None else out + t
    return out


def _ada_kernel(c_ref, w_ref, b_ref, o_ref):
    c = c_ref[...]
    s = (c * _sigmoid(c)).astype(BF16)
    o_ref[...] = jnp.dot(s, w_ref[...].astype(BF16), preferred_element_type=F32) + b_ref[...]


def _ada(cvec, w_ada, b_ada):
    depth, d, n = w_ada.shape
    tn = _pick(n, (512, 256, 128))
    return pl.pallas_call(
        _ada_kernel,
        grid=(depth, n // tn),
        in_specs=[
            pl.BlockSpec((8, d), lambda l, j: (0, 0)),
            pl.BlockSpec((None, d, tn), lambda l, j: (l, 0, j)),
            pl.BlockSpec((None, 1, tn), lambda l, j: (l, 0, j)),
        ],
        out_specs=pl.BlockSpec((None, 8, tn), lambda l, j: (l, 0, j)),
        out_shape=jax.ShapeDtypeStruct((depth, 8, n), F32),
        compiler_params=_cparams(("arbitrary", "arbitrary")),
        name="ada_mod",
    )(cvec, w_ada, b_ada.reshape(depth, 1, n))


def _inproj_kernel(x_ref, sh_ref, sc_ref, w_ref, o_ref, h_scr, *, n_ctx, tm):
    i = pl.program_id(0)
    j = pl.program_id(1)

    @pl.when(j == 0)
    def _():
        rows = i * tm + lax.broadcasted_iota(jnp.int32, (tm, 1), 0)
        is_ctx = rows < n_ctx
        sc = jnp.where(is_ctx, sc_ref[1:2, :], sc_ref[0:1, :])
        sh = jnp.where(is_ctx, sh_ref[1:2, :], sh_ref[0:1, :])
        h_scr[...] = (x_ref[...] * (1.0 + sc) + sh).astype(BF16)

    o_ref[...] = jnp.dot(h_scr[...], w_ref[...], preferred_element_type=F32)


def _inproj(xall, mod, w, n_ctx):
    t, d = xall.shape
    n = w.shape[1]
    tm = _pick(t, (768, 512, 256, 128))
    tn = _pick(n, (512, 256, 128))
    return pl.pallas_call(
        functools.partial(_inproj_kernel, n_ctx=n_ctx, tm=tm),
        grid=(t // tm, n // tn),
        in_specs=[
            pl.BlockSpec((tm, d), lambda i, j: (i, 0), pipeline_mode=pl.Buffered(1)),
            pl.BlockSpec((8, d), lambda i, j: (0, 0)),
            pl.BlockSpec((8, d), lambda i, j: (0, 1)),
            pl.BlockSpec((d, tn), lambda i, j: (0, j)),
        ],
        out_specs=pl.BlockSpec((tm, tn), lambda i, j: (i, j)),
        out_shape=jax.ShapeDtypeStruct((t, n), F32),
        scratch_shapes=[pltpu.VMEM((tm, d), BF16)],
        compiler_params=_cparams(("arbitrary", "arbitrary")),
        name="in_proj",
    )(xall, mod, mod, w)


def _conv_kernel(x_ref, w_ref, b_ref, o_ref, *, n_ctx, tt):
    t = x_ref.shape[0]
    w = w_ref[...]
    bias = b_ref[...]
    halo = 8
    pad = CONV_W // 2

    def body(i, carry):
        t0 = pl.multiple_of(i * tt, tt)
        cur = x_ref[pl.ds(t0, tt), :]
        p0 = pl.multiple_of(jnp.maximum(t0 - halo, 0), halo)
        n0 = pl.multiple_of(jnp.minimum(t0 + tt, t - halo), halo)
        prev = x_ref[pl.ds(p0, halo), :]
        nxt = x_ref[pl.ds(n0, halo), :]
        seg_start = jnp.logical_or(t0 == 0, t0 == n_ctx)
        seg_end = jnp.logical_or(t0 + tt == n_ctx, t0 + tt == t)
        prev = jnp.where(seg_start, 0.0, prev)
        nxt = jnp.where(seg_end, 0.0, nxt)
        win = jnp.concatenate([prev, cur, nxt], axis=0)
        acc = bias + w[0:1, :] * win[halo - pad:halo - pad + tt, :]
        for k in range(1, CONV_W):
            acc = acc + w[k:k + 1, :] * win[halo - pad + k:halo - pad + k + tt, :]
        o_ref[pl.ds(t0, tt), :] = acc * _sigmoid(acc)
        return carry

    lax.fori_loop(0, t // tt, body, 0)


def _conv(proj, w8, b, col0, width, n_ctx):
    t = proj.shape[0]
    tt = _pick(n_ctx, (256, 128))
    assert t % tt == 0 and col0 % LANE == 0 and width % LANE == 0
    cb0 = col0 // LANE
    return pl.pallas_call(
        functools.partial(_conv_kernel, n_ctx=n_ctx, tt=tt),
        grid=(width // LANE,),
        in_specs=[
            pl.BlockSpec((t, LANE), lambda c: (0, cb0 + c)),
            pl.BlockSpec((8, LANE), lambda c: (0, c)),
            pl.BlockSpec((1, LANE), lambda c: (0, c)),
        ],
        out_specs=pl.BlockSpec((t, LANE), lambda c: (0, c)),
        out_shape=jax.ShapeDtypeStruct((t, width), F32),
        compiler_params=_cparams(("arbitrary",)),
        name="dwconv_silu",
    )(proj, w8, b)


def _chunk_order(i, n_chunks, n_ctx_chunks, reverse):
    if not reverse:
        return i
    return jnp.where(i < n_ctx_chunks, n_ctx_chunks - 1 - i, n_chunks - 1 - (i - n_ctx_chunks))


def _tri_mask(reverse):
    row = lax.broadcasted_iota(jnp.int32, (CHUNK, CHUNK), 0)
    col = lax.broadcasted_iota(jnp.int32, (CHUNK, CHUNK), 1)
    return (col >= row) if reverse else (col <= row)


def _ssd_kernel(*refs, reverse, finish, heads, hpg):
    if finish:
        (xs_ref, b_ref, c_ref, sm_ref, bias_ref, alog_ref, exp_ref,
         z_ref, yprev_ref, dskip_ref, g_ref, o_ref, h_scr) = refs
    else:
        (xs_ref, b_ref, c_ref, sm_ref, bias_ref, alog_ref, exp_ref, o_ref, h_scr) = refs
    gw = hpg * SSD_HEADDIM

    @pl.when(pl.program_id(0) == 0)
    def _():
        h_scr[...] = jnp.zeros_like(h_scr)

    mask = _tri_mask(reverse)
    tri = mask.astype(BF16)
    p = sm_ref[...] + bias_ref[...]
    dt = _softplus(p)
    dta = dt * (-jnp.exp(alog_ref[...]))
    cum = _dot01_left(tri, dta)
    cum_t = cum.T
    tot = cum[0:1, :] if reverse else cum[CHUNK - 1:CHUNK, :]
    e01 = exp_ref[...]
    dt_x = _dot01_right(dt, e01)
    in_x = _dot01_right(jnp.exp(cum), e01)
    tail_x = _dot01_right(jnp.exp(tot - cum), e01)
    tot_x = _dot01_right(jnp.broadcast_to(jnp.exp(tot), (8, SMALL_W)), e01)[0:1, :]

    xs = xs_ref[...]
    xdt = xs * dt_x
    xdt_b = xdt.astype(BF16)
    xtail_b = (xdt * tail_x).astype(BF16)
    bm = b_ref[...]
    cm = c_ref[...]
    c0 = heads if reverse else 0
    ys = []
    for g in range(SSD_GROUPS):
        bg = bm[:, g * SSD_STATE:(g + 1) * SSD_STATE]
        cg = cm[:, g * SSD_STATE:(g + 1) * SSD_STATE].astype(BF16)
        cb = lax.dot_general(cg, bg.astype(BF16), (((1,), (1,)), ((), ())),
                             preferred_element_type=F32)
        h_t = h_scr[g]
        y_g = jnp.dot(cg, h_t.astype(BF16), preferred_element_type=F32) * in_x[:, g * gw:(g + 1) * gw]
        parts = []
        for r in range(hpg):
            h = g * hpg + r
            c = c0 + h
            seg = cum[:, c:c + 1] - cum_t[c:c + 1, :]
            decay = jnp.exp(jnp.where(mask, seg, NEG_BIG))
            m = (cb * decay).astype(BF16)
            parts.append(jnp.dot(m, xdt_b[:, h * SSD_HEADDIM:(h + 1) * SSD_HEADDIM],
                                 preferred_element_type=F32))
        ys.append(y_g + jnp.concatenate(parts, axis=1))
        h_scr[g] = h_t * tot_x[:, g * gw:(g + 1) * gw] + jnp.dot(
            bg.T.astype(BF16), xtail_b[:, g * gw:(g + 1) * gw], preferred_element_type=F32)
    y = jnp.concatenate(ys, axis=1)

    if finish:
        z = z_ref[...]
        yt = (yprev_ref[...] + y + dskip_ref[...] * xs) * (z * _sigmoid(z))
        ms = jnp.mean(yt * yt, axis=-1, keepdims=True)
        o_ref[...] = (yt * lax.rsqrt(ms + 1e-6) * g_ref[...]).astype(o_ref.dtype)
    else:
        o_ref[...] = y


def _ssd_pass(conv_out, proj, lay, bias1, alog_row, e01, n_ctx, reverse, extra=None):
    t = proj.shape[0]
    nch, ncc = t // CHUNK, n_ctx // CHUNK
    d_ssd = lay["d_ssd"]
    heads = d_ssd // SSD_HEADDIM
    gs = SSD_GROUPS * SSD_STATE
    order = functools.partial(_chunk_order, n_chunks=nch, n_ctx_chunks=ncc, reverse=reverse)
    const = lambda i: (0, 0)
    in_specs = [
        pl.BlockSpec((CHUNK, d_ssd), lambda i: (order(i), lay["co_xs"] // d_ssd)),
        pl.BlockSpec((CHUNK, gs), lambda i: (order(i), lay["co_b"] // gs)),
        pl.BlockSpec((CHUNK, gs), lambda i: (order(i), lay["co_c"] // gs)),
        pl.BlockSpec((CHUNK, SMALL_W), lambda i: (order(i), lay["sm1"] // SMALL_W)),
        pl.BlockSpec((1, SMALL_W), const),
        pl.BlockSpec((1, SMALL_W), const),
        pl.BlockSpec((SMALL_W, d_ssd), const),
    ]
    args = [conv_out, conv_out, conv_out, proj, bias1, alog_row, e01]
    finish = extra is not None
    if finish:
        y_prev, dskip_x, norm_g = extra
        in_specs += [
            pl.BlockSpec((CHUNK, d_ssd), lambda i: (order(i), lay["z"] // d_ssd)),
            pl.BlockSpec((CHUNK, d_ssd), lambda i: (order(i), 0)),
            pl.BlockSpec((1, d_ssd), const),
            pl.BlockSpec((1, d_ssd), const),
        ]
        args += [proj, y_prev, dskip_x, norm_g]
    return pl.pallas_call(
        functools.partial(_ssd_kernel, reverse=reverse, finish=finish, heads=heads,
                          hpg=heads // SSD_GROUPS),
        grid=(nch,),
        in_specs=in_specs,
        out_specs=pl.BlockSpec((CHUNK, d_ssd), lambda i: (order(i), 0)),
        out_shape=jax.ShapeDtypeStruct((t, d_ssd), BF16 if finish else F32),
        scratch_shapes=[pltpu.VMEM((SSD_GROUPS, SSD_STATE, d_ssd // SSD_GROUPS), F32)],
        compiler_params=_cparams(("arbitrary",)),
        name="ssd_bwd_finish" if finish else "ssd_fwd",
    )(*args)


def _mls_kernel(*refs, reverse, finish, heads):
    if finish:
        (q_ref, k_ref, v_ref, sm1_ref, sm2_ref, b1_ref, b2_ref,
         og_ref, hprev_ref, gain_ref, o_ref, s_scr, m_scr) = refs
    else:
        (q_ref, k_ref, v_ref, sm1_ref, sm2_ref, b1_ref, b2_ref, o_ref, s_scr, m_scr) = refs

    @pl.when(pl.program_id(0) == 0)
    def _():
        s_scr[...] = jnp.zeros_like(s_scr)
        m_scr[...] = jnp.zeros_like(m_scr)

    mask = _tri_mask(reverse)
    tri = mask.astype(BF16)
    lf = -_softplus(-(sm1_ref[...] + b1_ref[...]))
    li = sm2_ref[...] + b2_ref[...]
    b = _dot01_left(tri, lf)
    b_t = b.T
    li_t = li.T
    tot = b[0:1, :] if reverse else b[CHUNK - 1:CHUNK, :]
    m_row = m_scr[...]
    g_all = tot - b + li
    m_new = jnp.maximum(tot + m_row, jnp.max(g_all, axis=0, keepdims=True))
    wk_all = jnp.exp(g_all - m_new)
    decay_row = jnp.exp(tot + m_row - m_new)
    mprev_all = b + m_row
    m_scr[...] = m_new

    q = q_ref[...]
    k = k_ref[...] * (MLS_QK_DIM ** -0.5)
    v = v_ref[...]
    ones_col = (lax.broadcasted_iota(jnp.int32, (CHUNK, LANE), 1) == 0).astype(F32)
    c0 = lay_fcol(heads, reverse)
    outs = []
    for h in range(heads):
        c = c0 + h
        qh = q[:, h * MLS_QK_DIM:(h + 1) * MLS_QK_DIM].astype(BF16)
        kh = k[:, h * MLS_QK_DIM:(h + 1) * MLS_QK_DIM]
        khb = kh.astype(BF16)
        vext = jnp.concatenate([v[:, h * MLS_V_DIM:(h + 1) * MLS_V_DIM], ones_col], axis=1)
        dmat = jnp.where(mask, b[:, c:c + 1] - b_t[c:c + 1, :] + li_t[c:c + 1, :], NEG_BIG)
        m_prev = mprev_all[:, c:c + 1]
        m_t = jnp.maximum(m_prev, jnp.max(dmat, axis=1, keepdims=True))
        qk = lax.dot_general(qh, khb, (((1,), (1,)), ((), ())), preferred_element_type=F32)
        w = jnp.exp(dmat - m_t) * qk
        s_prev = jnp.exp(m_prev - m_t)
        s_h = s_scr[h]
        numx = (jnp.dot(w.astype(BF16), vext.astype(BF16), preferred_element_type=F32)
                + s_prev * jnp.dot(qh, s_h.astype(BF16), preferred_element_type=F32))
        den = numx[:, MLS_V_DIM:MLS_V_DIM + 1]
        outs.append(numx[:, :MLS_V_DIM] / jnp.maximum(jnp.abs(den), jnp.exp(-m_t)))
        s_scr[h] = decay_row[:, c:c + 1] * s_h + jnp.dot(
            kh.T.astype(BF16), (wk_all[:, c:c + 1] * vext).astype(BF16), preferred_element_type=F32)

    if finish:
        hp = hprev_ref[...]
        og = og_ref[...]
        gain = gain_ref[...]
        for h in range(heads):
            sl = slice(h * MLS_V_DIM, (h + 1) * MLS_V_DIM)
            hs = hp[:, sl] + outs[h]
            ms = jnp.mean(hs * hs, axis=-1, keepdims=True)
            o_ref[:, sl] = (_sigmoid(og[:, sl]) * (hs * lax.rsqrt(ms + 1e-6) * gain[:, sl])).astype(o_ref.dtype)
    else:
        for h in range(heads):
            o_ref[:, h * MLS_V_DIM:(h + 1) * MLS_V_DIM] = outs[h]


def lay_fcol(heads, reverse):
    return GATE_COL0 + (heads if reverse else 0)


def _mls_pass(conv_out, proj, lay, bias1, bias2, n_ctx, reverse, extra=None):
    t = proj.shape[0]
    nch, ncc = t // CHUNK, n_ctx // CHUNK
    d_mls = lay["d_mls"]
    heads = d_mls // MLS_V_DIM
    qw = heads * MLS_QK_DIM
    order = functools.partial(_chunk_order, n_chunks=nch, n_ctx_chunks=ncc, reverse=reverse)
    const = lambda i: (0, 0)
    in_specs = [
        pl.BlockSpec((CHUNK, qw), lambda i: (order(i), lay["co_q"] // qw)),
        pl.BlockSpec((CHUNK, qw), lambda i: (order(i), lay["co_k"] // qw)),
        pl.BlockSpec((CHUNK, d_mls), lambda i: (order(i), lay["mv"] // d_mls)),
        pl.BlockSpec((CHUNK, SMALL_W), lambda i: (order(i), lay["sm1"] // SMALL_W)),
        pl.BlockSpec((CHUNK, SMALL_W), lambda i: (order(i), lay["sm2"] // SMALL_W)),
        pl.BlockSpec((1, SMALL_W), const),
        pl.BlockSpec((1, SMALL_W), const),
    ]
    args = [conv_out, conv_out, proj, proj, proj, bias1, bias2]
    finish = extra is not None
    if finish:
        h_prev, gain = extra
        in_specs += [
            pl.BlockSpec((CHUNK, d_mls), lambda i: (order(i), lay["mo"] // d_mls)),
            pl.BlockSpec((CHUNK, d_mls), lambda i: (order(i), 0)),
            pl.BlockSpec((1, d_mls), const),
        ]
        args += [proj, h_prev, gain]
    return pl.pallas_call(
        functools.partial(_mls_kernel, reverse=reverse, finish=finish, heads=heads),
        grid=(nch,),
        in_specs=in_specs,
        out_specs=pl.BlockSpec((CHUNK, d_mls), lambda i: (order(i), 0)),
        out_shape=jax.ShapeDtypeStruct((t, d_mls), BF16 if finish else F32),
        scratch_shapes=[pltpu.VMEM((heads, MLS_QK_DIM, MLS_V_DIM + LANE), F32),
                        pltpu.VMEM((1, SMALL_W), F32)],
        compiler_params=_cparams(("arbitrary",)),
        name="mlstm_bwd_finish" if finish else "mlstm_fwd",
    )(*args)


def _qk_prep_kernel(q_ref, k_ref, v_ref, cos_ref, sin_ref, gq_ref, gk_ref, qo_ref, ko_ref, vo_ref):
    cos = cos_ref[...]
    sin = sin_ref[...]

    def norm_rope(xh, g, scale):
        ms = jnp.mean(xh * xh, axis=-1, keepdims=True)
        xn = xh * lax.rsqrt(ms + 1e-6) * g
        return (xn * cos + pltpu.roll(xn, HEAD_DIM // 2, axis=1) * sin) * scale

    q = q_ref[...]
    for h in range(q.shape[1] // HEAD_DIM):
        sl = slice(h * HEAD_DIM, (h + 1) * HEAD_DIM)
        qo_ref[:, sl] = norm_rope(q[:, sl], gq_ref[...], LOG2_E * HEAD_DIM ** -0.5).astype(qo_ref.dtype)
    k = k_ref[...]
    for h in range(k.shape[1] // HEAD_DIM):
        sl = slice(h * HEAD_DIM, (h + 1) * HEAD_DIM)
        ko_ref[:, sl] = norm_rope(k[:, sl], gk_ref[...], 1.0).astype(ko_ref.dtype)
    v = v_ref[...]
    ones = jnp.ones((v.shape[0], HEAD_DIM), vo_ref.dtype)
    for h in range(v.shape[1] // HEAD_DIM):
        vo_ref[:, 2 * h * HEAD_DIM:(2 * h + 1) * HEAD_DIM] = v[:, h * HEAD_DIM:(h + 1) * HEAD_DIM].astype(vo_ref.dtype)
        vo_ref[:, (2 * h + 1) * HEAD_DIM:(2 * h + 2) * HEAD_DIM] = ones


def _qk_prep(proj, lay, cos_t, sin_t, gq, gk):
    t = proj.shape[0]
    tm = _pick(t, (256, 128))
    d_att = lay["d_att"]
    kvw = ATT_KV_HEADS * HEAD_DIM
    const = lambda i: (0, 0)
    return pl.pallas_call(
        _qk_prep_kernel,
        grid=(t // tm,),
        in_specs=[
            pl.BlockSpec((tm, d_att), lambda i: (i, lay["aq"] // d_att)),
            pl.BlockSpec((tm, kvw), lambda i: (i, lay["ak"] // kvw)),
            pl.BlockSpec((tm, kvw), lambda i: (i, lay["av"] // kvw)),
            pl.BlockSpec((tm, HEAD_DIM), lambda i: (i, 0)),
            pl.BlockSpec((tm, HEAD_DIM), lambda i: (i, 0)),
            pl.BlockSpec((1, HEAD_DIM), const),
            pl.BlockSpec((1, HEAD_DIM), const),
        ],
        out_specs=[
            pl.BlockSpec((tm, d_att), lambda i: (i, 0)),
            pl.BlockSpec((tm, kvw), lambda i: (i, 0)),
            pl.BlockSpec((tm, 2 * kvw), lambda i: (i, 0)),
        ],
        out_shape=[jax.ShapeDtypeStruct((t, d_att), BF16),
                   jax.ShapeDtypeStruct((t, kvw), BF16),
                   jax.ShapeDtypeStruct((t, 2 * kvw), BF16)],
        compiler_params=_cparams(("arbitrary",)),
        name="qk_norm_rope",
    )(proj, proj, proj, cos_t, sin_t, gq, gk)


FLASH_ROW_BLOCK = 16
FLASH_KV_SPLIT = 256


def _flash_kernel(q_ref, k_ref, v_ref, o_ref, s_scr, p_scr, m_scr, a_scr, acc_scr, *, group, tq):
    j = pl.program_id(2)

    @pl.when(j == 0)
    def _():
        m_scr[...] = jnp.full_like(m_scr, NEG_BIG)
        acc_scr[...] = jnp.zeros_like(acc_scr)

    n_rows = group * tq
    n_split, _, tks = s_scr.shape
    rb = FLASH_ROW_BLOCK
    lane_tiles = [slice(t * LANE, (t + 1) * LANE) for t in range(tks // LANE)]
    for c in range(n_split):
        kc = k_ref[c * tks:(c + 1) * tks, :]
        for h in range(group):
            s_scr[c, h * tq:(h + 1) * tq, :] = lax.dot_general(
                q_ref[:, h * HEAD_DIM:(h + 1) * HEAD_DIM], kc, (((1,), (1,)), ((), ())),
                preferred_element_type=F32)
    for c in range(n_split):
        for b in range(n_rows // rb):
            rows = slice(b * rb, (b + 1) * rb)
            mx = s_scr[c, rows, lane_tiles[0]]
            for tile in lane_tiles[1:]:
                mx = jnp.maximum(mx, s_scr[c, rows, tile])
            m_old = m_scr[rows, :]
            m_new = jnp.maximum(m_old, jnp.broadcast_to(jnp.max(mx, axis=1, keepdims=True), (rb, LANE)))
            m_scr[rows, :] = m_new
            a_scr[rows, :] = jnp.exp2(m_old - m_new)
        for b in range(n_rows // rb):
            rows = slice(b * rb, (b + 1) * rb)
            m = m_scr[rows, :]
            for tile in lane_tiles:
                p_scr[c, rows, tile] = jnp.exp2(s_scr[c, rows, tile] - m).astype(BF16)
            alpha = a_scr[rows, :]
            acc_scr[rows, :] = acc_scr[rows, :] * jnp.concatenate([alpha, alpha], axis=1)
        acc_scr[...] += jnp.dot(p_scr[c], v_ref[c * tks:(c + 1) * tks, :], preferred_element_type=F32)

    @pl.when(j == pl.num_programs(2) - 1)
    def _():
        for h in range(group):
            rows = slice(h * tq, (h + 1) * tq)
            o_ref[:, h * HEAD_DIM:(h + 1) * HEAD_DIM] = (
                acc_scr[rows, :HEAD_DIM] / acc_scr[rows, HEAD_DIM:]).astype(o_ref.dtype)


def _flash(qn, kn, vb, n_k):
    n_q, d_att = qn.shape
    group = d_att // HEAD_DIM // ATT_KV_HEADS
    gw = group * HEAD_DIM
    tq = _pick(n_q, (512, 256, 128))
    tk = _pick(n_k, (768, 512, 256, 128))
    tks = min(tk, FLASH_KV_SPLIT)
    return pl.pallas_call(
        functools.partial(_flash_kernel, group=group, tq=tq),
        grid=(ATT_KV_HEADS, n_q // tq, n_k // tk),
        in_specs=[
            pl.BlockSpec((tq, gw), lambda g, i, j: (i, g)),
            pl.BlockSpec((tk, HEAD_DIM), lambda g, i, j: (j, g)),
            pl.BlockSpec((tk, 2 * HEAD_DIM), lambda g, i, j: (j, g)),
        ],
        out_specs=pl.BlockSpec((tq, gw), lambda g, i, j: (i, g)),
        out_shape=jax.ShapeDtypeStruct((n_q, d_att), BF16),
        scratch_shapes=[pltpu.VMEM((tk // tks, group * tq, tks), F32),
                        pltpu.VMEM((tk // tks, group * tq, tks), BF16),
                        pltpu.VMEM((group * tq, LANE), F32),
                        pltpu.VMEM((group * tq, LANE), F32),
                        pltpu.VMEM((group * tq, 2 * HEAD_DIM), F32)],
        compiler_params=_cparams(("arbitrary", "arbitrary", "arbitrary")),
        name="flash_gqa",
    )(qn, kn, vb)


def _outproj_kernel(a1_ref, a2_ref, a3_ref, w1_ref, w2_ref, w3_ref, o_ref):
    acc = jnp.dot(a1_ref[...], w1_ref[...], preferred_element_type=F32)
    acc = acc + jnp.dot(a2_ref[...], w2_ref[...], preferred_element_type=F32)
    acc = acc + jnp.dot(a3_ref[...], w3_ref[...], preferred_element_type=F32)
    o_ref[...] = acc


def _outproj(a1, a2, a3, w1, w2, w3):
    t = a1.shape[0]
    n = w1.shape[1]
    tm = _pick(t, (768, 512, 256, 128))
    tn = _pick(n, (512, 256, 128))
    lhs = lambda a: pl.BlockSpec((tm, a.shape[1]), lambda i, j: (i, 0))
    rhs = lambda w: pl.BlockSpec((w.shape[0], tn), lambda i, j: (0, j))
    return pl.pallas_call(
        _outproj_kernel,
        grid=(t // tm, n // tn),
        in_specs=[lhs(a1), lhs(a2), lhs(a3), rhs(w1), rhs(w2), rhs(w3)],
        out_specs=pl.BlockSpec((tm, tn), lambda i, j: (i, j)),
        out_shape=jax.ShapeDtypeStruct((t, n), F32),
        compiler_params=_cparams(("arbitrary", "arbitrary")),
        name="out_proj",
    )(a1, a2, a3, w1, w2, w3)


def _topk_route(logits, n_experts):
    lane = lax.broadcasted_iota(jnp.int32, logits.shape, 1)
    l = jnp.where(lane < n_experts, logits, NEG_BIG)
    tops = []
    for _ in range(TOP_K):
        m = jnp.max(l, axis=1, keepdims=True)
        idx = jnp.min(jnp.where(l == m, lane, LANE), axis=1, keepdims=True)
        tops.append((m, idx))
        l = jnp.where(lane == idx, NEG_BIG, l)
    es = [jnp.exp(m - tops[0][0]) for m, _ in tops]
    den = es[0]
    for e in es[1:]:
        den = den + e
    wts = jnp.zeros(logits.shape, F32)
    ids = jnp.zeros(logits.shape, jnp.int32)
    for k, ((m, idx), e) in enumerate(zip(tops, es)):
        wts = jnp.where(lane == k, e / den, wts)
        ids = jnp.where(lane == k, idx, ids)
    return wts, ids


def _ln_route_kernel(x_ref, y_ref, gate_ref, lng_ref, lnb_ref, sh_ref, sc_ref, wr_ref, br_ref,
                     xo_ref, tok_ref, wts_ref, ids_ref, *, n_ctx, tm, alpha, n_experts):
    rows = pl.program_id(0) * tm + lax.broadcasted_iota(jnp.int32, (tm, 1), 0)
    is_ctx = rows < n_ctx
    gate = jnp.where(is_ctx, gate_ref[1:2, :], gate_ref[0:1, :])
    u = alpha * x_ref[...] + gate * y_ref[...]
    mu = jnp.mean(u, axis=-1, keepdims=True)
    uc = u - mu
    var = jnp.mean(uc * uc, axis=-1, keepdims=True)
    xn = uc * lax.rsqrt(var + 1e-5) * lng_ref[...] + lnb_ref[...]
    xo_ref[...] = xn
    sc = jnp.where(is_ctx, sc_ref[1:2, :], sc_ref[0:1, :])
    sh = jnp.where(is_ctx, sh_ref[1:2, :], sh_ref[0:1, :])
    tok = xn * (1.0 + sc) + sh
    tok_ref[...] = tok
    t_hi, t_lo = _split_bf16(tok, 2)
    w_hi, w_lo = _split_bf16(wr_ref[...], 2)
    logits = (jnp.dot(t_hi, w_hi, preferred_element_type=F32)
              + jnp.dot(t_hi, w_lo, preferred_element_type=F32)
              + jnp.dot(t_lo, w_hi, preferred_element_type=F32)) + br_ref[...]
    wts_ref[...], ids_ref[...] = _topk_route(logits, n_experts)


def _resid_ln_route(xall, y, mod, gate_blk, sh_blk, sc_blk, lng, lnb, w_router, b_router, n_experts, n_ctx, alpha):
    t, d = xall.shape
    tm = _pick(t, (256, 128))
    const = lambda i: (0, 0)
    row = pl.BlockSpec((tm, d), lambda i: (i, 0))
    lanes = pl.BlockSpec((tm, LANE), lambda i: (i, 0))
    mod_blk = lambda blk: pl.BlockSpec((8, d), lambda i: (0, blk))
    return pl.pallas_call(
        functools.partial(_ln_route_kernel, n_ctx=n_ctx, tm=tm, alpha=alpha, n_experts=n_experts),
        grid=(t // tm,),
        in_specs=[row, row, mod_blk(gate_blk), pl.BlockSpec((1, d), const), pl.BlockSpec((1, d), const),
                  mod_blk(sh_blk), mod_blk(sc_blk), pl.BlockSpec((d, LANE), const),
                  pl.BlockSpec((1, LANE), const)],
        out_specs=[row, row, lanes, lanes],
        out_shape=[jax.ShapeDtypeStruct((t, d), F32), jax.ShapeDtypeStruct((t, d), F32),
                   jax.ShapeDtypeStruct((t, LANE), F32), jax.ShapeDtypeStruct((t, LANE), jnp.int32)],
        compiler_params=_cparams(("arbitrary",)),
        name="resid_ln_route",
    )(xall, y, mod, lng, lnb, mod, mod, w_router, b_router)


MOE_ROW_TILE = 256
COMBINE_TILE = 128


def _route_plan(ids, n_experts, tm):
    t = ids.shape[0]
    n_rows = t * TOP_K
    flat = ids[:, :TOP_K].reshape(-1)
    onehot = (flat[:, None] == jnp.arange(n_experts, dtype=jnp.int32)[None, :]).astype(jnp.int32)
    csum = jnp.cumsum(onehot, axis=0)
    rank = jnp.sum((csum - onehot) * onehot, axis=1)
    cnt = csum[-1]
    ptiles = (cnt + tm - 1) // tm
    tile_end = jnp.cumsum(ptiles)
    row_start = (tile_end - ptiles) * tm
    pos = (jnp.sum(onehot * row_start[None, :], axis=1) + rank).astype(jnp.int32)
    n_tiles = (n_rows + n_experts * (tm - 1)) // tm
    row_token = jnp.zeros((n_tiles * tm,), jnp.int32).at[pos].set(
        jnp.arange(n_rows, dtype=jnp.int32) // TOP_K)
    tiles = jnp.arange(n_tiles, dtype=jnp.int32)
    tile_expert = jnp.minimum(jnp.sum((tiles[:, None] >= tile_end[None, :]).astype(jnp.int32), axis=1),
                              n_experts - 1).astype(jnp.int32)
    tile_valid = (tiles < tile_end[-1]).astype(jnp.int32)
    return pos, row_token, tile_expert, tile_valid, n_tiles


MOE_GU_CHUNK = 256


def _moe_kernel(te_ref, tv_ref, rt_ref, tok_hbm, wgu_ref, bgu_ref, wdn_ref, bdn_ref, sel_ref, o_ref,
                buf, xb_scr, pair_scr, sem, *, tm, n_tiles):
    i = pl.program_id(0)
    n_chunks = wgu_ref.shape[1] // MOE_GU_CHUNK
    rows_per_chunk = tm // n_chunks

    def start_row(tile, slot, r):
        row = rt_ref[tile * tm + r]
        pltpu.make_async_copy(tok_hbm.at[pl.ds(row, 1), :], buf.at[slot, pl.ds(r, 1), :], sem.at[slot]).start()

    @pl.when(jnp.logical_and(i == 0, tv_ref[0] > 0))
    def _():
        def body(r, carry):
            start_row(0, 0, r)
            return carry
        lax.fori_loop(0, tm, body, 0, unroll=8)

    nxt = jnp.minimum(i + 1, n_tiles - 1)
    valid = tv_ref[i] > 0
    prefetch = jnp.logical_and(i + 1 < n_tiles, tv_ref[nxt] > 0)
    slot = i % 2

    def compute(issue_next):
        pltpu.make_async_copy(tok_hbm.at[pl.ds(0, tm), :], buf.at[slot], sem.at[slot]).wait()
        xb_scr[...] = buf[slot].astype(BF16)
        for c in range(n_chunks):
            if issue_next:
                for r in range(c * rows_per_chunk, (c + 1) * rows_per_chunk):
                    start_row(nxt, 1 - slot, r)
            cols = slice(c * MOE_GU_CHUNK, (c + 1) * MOE_GU_CHUNK)
            gu = jnp.dot(xb_scr[...], wgu_ref[:, cols], preferred_element_type=F32) + bgu_ref[:, cols]
            g = jnp.minimum(gu, SWIGLU_LIMIT)
            a = g * _sigmoid(SWIGLU_ALPHA * g)
            lin1 = jnp.clip(gu, -SWIGLU_LIMIT, SWIGLU_LIMIT) + 1.0
            pair_scr[:, cols] = (a * pltpu.roll(lin1, MOE_GU_CHUNK - 1, axis=1)).astype(BF16)
        act = jnp.dot(pair_scr[...], sel_ref[...], preferred_element_type=F32).astype(BF16)
        o_ref[...] = jnp.dot(act, wdn_ref[...], preferred_element_type=F32) + bdn_ref[...]

    @pl.when(jnp.logical_and(valid, prefetch))
    def _():
        compute(True)

    @pl.when(jnp.logical_and(valid, jnp.logical_not(prefetch)))
    def _():
        compute(False)

    @pl.when(jnp.logical_not(valid))
    def _():
        o_ref[...] = jnp.zeros_like(o_ref)


def _moe_experts(l, tok, plan, wgu, bgu, wdn, bdn):
    _, row_token, tile_expert, tile_valid, n_tiles = plan
    tm = MOE_ROW_TILE
    d = tok.shape[1]
    two_de, de = wgu.shape[3], wdn.shape[2]
    sel = jnp.asarray(np.arange(two_de)[:, None] == 2 * np.arange(de)[None, :], BF16)
    grid_spec = pltpu.PrefetchScalarGridSpec(
        num_scalar_prefetch=3,
        grid=(n_tiles,),
        in_specs=[
            pl.BlockSpec(memory_space=pl.ANY),
            pl.BlockSpec((None, None, d, two_de), lambda i, te, tv, rt: (l, te[i], 0, 0)),
            pl.BlockSpec((None, None, 1, two_de), lambda i, te, tv, rt: (l, te[i], 0, 0)),
            pl.BlockSpec((None, None, de, d), lambda i, te, tv, rt: (l, te[i], 0, 0)),
            pl.BlockSpec((None, None, 1, d), lambda i, te, tv, rt: (l, te[i], 0, 0)),
            pl.BlockSpec((two_de, de), lambda i, te, tv, rt: (0, 0)),
        ],
        out_specs=pl.BlockSpec((tm, d), lambda i, te, tv, rt: (i, 0)),
        scratch_shapes=[pltpu.VMEM((2, tm, d), F32), pltpu.VMEM((tm, d), BF16), pltpu.VMEM((tm, two_de), BF16),
                        pltpu.SemaphoreType.DMA((2,))],
    )
    assert two_de % MOE_GU_CHUNK == 0 and tm % (two_de // MOE_GU_CHUNK) == 0
    return pl.pallas_call(
        functools.partial(_moe_kernel, tm=tm, n_tiles=n_tiles),
        grid_spec=grid_spec,
        out_shape=jax.ShapeDtypeStruct((n_tiles * tm, d), F32),
        compiler_params=_cparams(("arbitrary",)),
        name="moe_experts",
    )(tile_expert, tile_valid, row_token, tok, wgu, bgu, wdn, bdn, sel)


def _combine_ln_kernel(pos_ref, y_hbm, x_ref, wts_ref, gate_ref, lng_ref, lnb_ref, xo_ref, buf, sem,
                       *, n_ctx, tm, alpha, n_steps):
    i = pl.program_id(0)

    def issue(tile, slot):
        def body(r, carry):
            for k in range(TOP_K):
                row = pos_ref[(tile * tm + r) * TOP_K + k]
                pltpu.make_async_copy(y_hbm.at[pl.ds(row, 1), :], buf.at[slot, k, pl.ds(r, 1), :],
                                      sem.at[slot]).start()
            return carry
        lax.fori_loop(0, tm, body, 0, unroll=4)

    @pl.when(i == 0)
    def _():
        issue(0, 0)

    @pl.when(i + 1 < n_steps)
    def _():
        issue(i + 1, (i + 1) % 2)

    slot = i % 2
    for k in range(TOP_K):
        pltpu.make_async_copy(y_hbm.at[pl.ds(0, tm), :], buf.at[slot, k], sem.at[slot]).wait()
    wts = wts_ref[...]
    f = wts[:, 0:1] * buf[slot, 0]
    for k in range(1, TOP_K):
        f = f + wts[:, k:k + 1] * buf[slot, k]
    rows = i * tm + lax.broadcasted_iota(jnp.int32, (tm, 1), 0)
    gate = jnp.where(rows < n_ctx, gate_ref[1:2, :], gate_ref[0:1, :])
    u = alpha * x_ref[...] + gate * f
    mu = jnp.mean(u, axis=-1, keepdims=True)
    uc = u - mu
    var = jnp.mean(uc * uc, axis=-1, keepdims=True)
    xo_ref[...] = uc * lax.rsqrt(var + 1e-5) * lng_ref[...] + lnb_ref[...]


def _combine_ln(x1, y_sorted, pos, wts, mod, gate_blk, lng, lnb, n_ctx, alpha):
    t, d = x1.shape
    tm = COMBINE_TILE
    n_steps = t // tm
    const = lambda i, p: (0, 0)
    grid_spec = pltpu.PrefetchScalarGridSpec(
        num_scalar_prefetch=1,
        grid=(n_steps,),
        in_specs=[
            pl.BlockSpec(memory_space=pl.ANY),
            pl.BlockSpec((tm, d), lambda i, p: (i, 0)),
            pl.BlockSpec((tm, LANE), lambda i, p: (i, 0)),
            pl.BlockSpec((8, d), lambda i, p: (0, gate_blk)),
            pl.BlockSpec((1, d), const),
            pl.BlockSpec((1, d), const),
        ],
        out_specs=pl.BlockSpec((tm, d), lambda i, p: (i, 0)),
        scratch_shapes=[pltpu.VMEM((2, TOP_K, tm, d), F32), pltpu.SemaphoreType.DMA((2,))],
    )
    return pl.pallas_call(
        functools.partial(_combine_ln_kernel, n_ctx=n_ctx, tm=tm, alpha=alpha, n_steps=n_steps),
        grid_spec=grid_spec,
        out_shape=jax.ShapeDtypeStruct((t, d), F32),
        compiler_params=_cparams(("arbitrary",)),
        name="moe_combine_ln",
    )(pos, y_sorted, x1, wts, mod, lng, lnb)


GATE_COL0 = 48


def _layout(d_model):
    d_ssd = 3 * d_model // 8
    d_att = d_model // 4
    d_mls = d_model - d_ssd - d_att
    ssd_heads = d_ssd // SSD_HEADDIM
    mls_heads = d_mls // MLS_V_DIM
    gs = SSD_GROUPS * SSD_STATE
    kvw = ATT_KV_HEADS * HEAD_DIM
    qkw = mls_heads * MLS_QK_DIM
    assert 2 * ssd_heads == GATE_COL0 and GATE_COL0 + 2 * mls_heads <= SMALL_W
    sizes = (d_ssd, d_ssd + 2 * gs, 2 * ssd_heads, d_att, kvw, kvw, 2 * qkw, d_mls, d_mls, 4 * mls_heads)
    o = np.concatenate([[0], np.cumsum(sizes)])
    src = dict(z=o[0], xbc=o[1], dt=o[2], aq=o[3], ak=o[4], av=o[5], mqk=o[6], mv=o[7], mo=o[8], mg=o[9])
    lay = dict(d_ssd=d_ssd, d_att=d_att, d_mls=d_mls, ssd_heads=ssd_heads, mls_heads=mls_heads,
               src={k: int(v) for k, v in src.items()}, gs=gs, kvw=kvw, qkw=qkw)
    cols = []

    def put(name, idx):
        lay[name] = len(cols)
        cols.extend(int(v) for v in idx)

    half = np.concatenate([np.arange(0, HEAD_DIM, 2), np.arange(1, HEAD_DIM, 2)])

    def heads_split(base, n_heads):
        return np.concatenate([base + h * HEAD_DIM + half for h in range(n_heads)])

    put("z", src["z"] + np.arange(d_ssd))
    put("mv", src["mv"] + np.arange(d_mls))
    put("mo", src["mo"] + np.arange(d_mls))
    put("mqk", src["mqk"] + np.arange(2 * qkw))
    put("xbc", src["xbc"] + np.arange(d_ssd + 2 * gs))
    put("ak", heads_split(src["ak"], ATT_KV_HEADS))
    put("av", src["av"] + np.arange(kvw))
    put("aq", heads_split(src["aq"], d_att // HEAD_DIM))
    sm1 = -np.ones(SMALL_W, np.int64)
    sm1[:2 * ssd_heads] = src["dt"] + np.arange(2 * ssd_heads)
    sm2 = -np.ones(SMALL_W, np.int64)
    for direction in range(2):
        c = GATE_COL0 + direction * mls_heads
        sm1[c:c + mls_heads] = src["mg"] + (2 * direction + 1) * mls_heads + np.arange(mls_heads)
        sm2[c:c + mls_heads] = src["mg"] + (2 * direction) * mls_heads + np.arange(mls_heads)
    put("sm1", sm1)
    put("sm2", sm2)
    n_pad = -len(cols) % 512
    cols.extend([-1] * n_pad)
    lay["cols"] = np.asarray(cols, np.int64)
    lay["conv0"] = lay["mqk"]
    lay["conv_w"] = 2 * qkw + d_ssd + 2 * gs
    lay["co_q"], lay["co_k"], lay["co_xs"] = 0, qkw, 2 * qkw
    lay["co_b"], lay["co_c"] = 2 * qkw + d_ssd, 2 * qkw + d_ssd + gs
    for name, width in (("z", d_ssd), ("mv", d_mls), ("mo", d_mls), ("aq", d_att), ("ak", kvw), ("av", kvw),
                        ("sm1", SMALL_W), ("sm2", SMALL_W), ("conv0", LANE)):
        assert lay[name] % width == 0, name
    assert lay["co_k"] % qkw == 0 and lay["co_xs"] % d_ssd == 0 and lay["co_b"] % gs == 0 and lay["co_c"] % gs == 0
    return lay


def _relayout_w_in(w, lay):
    d = w.shape[0]
    s = lay["src"]
    sh, mh = lay["ssd_heads"], lay["mls_heads"]

    def seg(name, width):
        return w[:, s[name]:s[name] + width]

    def split_heads(a):
        n = a.shape[1] // HEAD_DIM
        return a.reshape(d, n, HEAD_DIM // 2, 2).swapaxes(2, 3).reshape(d, n * HEAD_DIM)

    zeros = lambda n: jnp.zeros((d, n), w.dtype)
    mg = seg("mg", 4 * mh).reshape(d, 4, mh)
    tail = SMALL_W - GATE_COL0 - 2 * mh
    parts = [seg("z", lay["d_ssd"]), seg("mv", lay["d_mls"]), seg("mo", lay["d_mls"]),
             seg("mqk", 2 * lay["qkw"]), seg("xbc", lay["d_ssd"] + 2 * lay["gs"]),
             split_heads(seg("ak", lay["kvw"])), seg("av", lay["kvw"]), split_heads(seg("aq", lay["d_att"])),
             seg("dt", 2 * sh), mg[:, 1], mg[:, 3], zeros(tail),
             zeros(GATE_COL0), mg[:, 0], mg[:, 2], zeros(tail)]
    parts.append(zeros(len(lay["cols"]) - sum(a.shape[1] for a in parts)))
    return jnp.concatenate(parts, axis=1).astype(BF16)


def _rope_tables(n_ctx, seq):
    rows = seq // GRID_W
    row = jnp.repeat(jnp.arange(rows), GRID_W).astype(F32)
    col = jnp.tile(jnp.arange(GRID_W), rows).astype(F32)
    n_freq = HEAD_DIM // 4
    inv = ROPE_THETA ** (-jnp.arange(n_freq, dtype=F32) / n_freq)
    ang = jnp.concatenate([row[:, None] * inv, col[:, None] * inv], axis=-1)
    cos, sin = jnp.cos(ang), jnp.sin(ang)
    cos_t = jnp.concatenate([jnp.ones((n_ctx, HEAD_DIM), F32), jnp.concatenate([cos, cos], axis=-1)], axis=0)
    sin_t = jnp.concatenate([jnp.zeros((n_ctx, HEAD_DIM), F32), jnp.concatenate([-sin, sin], axis=-1)], axis=0)
    return cos_t, sin_t


def _pad_row(v, width=SMALL_W, at=0):
    out = jnp.zeros((1, width), F32)
    return out.at[0, at:at + v.shape[0]].set(v.astype(F32))


def _layer(l, xall, mod, lay, tables, n_ctx, alpha, p):
    d = xall.shape[1]
    proj = _inproj(xall, mod, _relayout_w_in(p["w_in"][l], lay), n_ctx)

    conv_w = jnp.concatenate([p["mls_conv_w"][l], p["ssd_conv_w"][l]], axis=1)
    conv_w8 = jnp.concatenate([conv_w, jnp.zeros((8 - CONV_W, conv_w.shape[1]), F32)], axis=0)
    conv_b = jnp.concatenate([p["mls_conv_b"][l], p["ssd_conv_b"][l]])[None, :]
    conv_out = _conv(proj, conv_w8, conv_b, lay["conv0"], lay["conv_w"], n_ctx)

    sh, mh = lay["ssd_heads"], lay["mls_heads"]
    gate_b = p["mls_gate_b"][l]
    bias1 = (_pad_row(p["ssd_dt_bias"][l].reshape(-1))
             + _pad_row(gate_b[1], at=GATE_COL0) + _pad_row(gate_b[3], at=GATE_COL0 + mh))
    bias2 = _pad_row(gate_b[0], at=GATE_COL0) + _pad_row(gate_b[2], at=GATE_COL0 + mh)
    alog_row = _pad_row(p["ssd_A_log"][l].reshape(-1))
    hcol = np.repeat(np.arange(sh), SSD_HEADDIM)
    e_f = jnp.asarray(np.arange(SMALL_W)[:, None] == hcol[None, :], BF16)
    e_b = jnp.asarray(np.arange(SMALL_W)[:, None] == (hcol + sh)[None, :], BF16)
    dskip_x = jnp.repeat(p["ssd_D"][l], SSD_HEADDIM)[None, :]
    y_f = _ssd_pass(conv_out, proj, lay, bias1, alog_row, e_f, n_ctx, False)
    ssd = _ssd_pass(conv_out, proj, lay, bias1, alog_row, e_b, n_ctx, True,
                    extra=(y_f, dskip_x, p["ssd_norm_g"][l][None, :]))

    h_f = _mls_pass(conv_out, proj, lay, bias1, bias2, n_ctx, False)
    mls = _mls_pass(conv_out, proj, lay, bias1, bias2, n_ctx, True, extra=(h_f, p["mls_norm_g"][l][None, :]))

    half = np.concatenate([np.arange(0, HEAD_DIM, 2), np.arange(1, HEAD_DIM, 2)])
    gq = p["att_q_norm_g"][l][half][None, :]
    gk = p["att_k_norm_g"][l][half][None, :]
    qn, kn, vb = _qk_prep(proj, lay, tables[0], tables[1], gq, gk)
    t = xall.shape[0]
    att_c = _flash(qn[:n_ctx], kn, vb, n_ctx)
    att_l = _flash(qn[n_ctx:], kn, vb, t)
    att = jnp.concatenate([att_c, att_l], axis=0)

    w_out = p["w_out"][l].astype(BF16)
    d_ssd, d_att = lay["d_ssd"], lay["d_att"]
    mix = _outproj(ssd, att, mls, w_out[:d_ssd], w_out[d_ssd:d_ssd + d_att], w_out[d_ssd + d_att:])

    ne = p["w_router"].shape[2]
    w_router = jnp.zeros((d, LANE), F32).at[:, :ne].set(p["w_router"][l])
    b_router = _pad_row(p["b_router"][l], LANE)
    x1, tok, wts, ids = _resid_ln_route(xall, mix, mod, 2, 3, 4, p["ln1_g"][l][None, :], p["ln1_b"][l][None, :],
                                        w_router, b_router, ne, n_ctx, alpha)

    plan = _route_plan(ids, ne, MOE_ROW_TILE)
    y_sorted = _moe_experts(l, tok, plan, p["w_gu_bf16"], p["b_gu"][:, :, None, :], p["w_dn_bf16"],
                            p["b_dn"][:, :, None, :])
    x2 = _combine_ln(x1, y_sorted, plan[0], wts, mod, 5, p["ln2_g"][l][None, :], p["ln2_b"][l][None, :],
                     n_ctx, alpha)
    aux = dict(proj=proj, conv_out=conv_out, ssd=ssd, mls=mls, att=att, mix=mix, x1=x1, tok=tok, wts=wts,
               ids=ids, y_sorted=y_sorted, pos=plan[0])
    return x2, aux


def kernel(x, c, ctx, c_ctx, w_ada, b_ada, w_in, ssd_conv_w, ssd_conv_b, ssd_A_log, ssd_dt_bias, ssd_D,
           ssd_norm_g, att_q_norm_g, att_k_norm_g, mls_conv_w, mls_conv_b, mls_gate_b, mls_norm_g, w_out,
           ln1_g, ln1_b, w_router, b_router, w_gu, b_gu, w_dn, b_dn, ln2_g, ln2_b):
    p = dict(w_in=w_in, ssd_conv_w=ssd_conv_w, ssd_conv_b=ssd_conv_b, ssd_A_log=ssd_A_log,
             ssd_dt_bias=ssd_dt_bias, ssd_D=ssd_D, ssd_norm_g=ssd_norm_g, att_q_norm_g=att_q_norm_g,
             att_k_norm_g=att_k_norm_g, mls_conv_w=mls_conv_w, mls_conv_b=mls_conv_b, mls_gate_b=mls_gate_b,
             mls_norm_g=mls_norm_g, w_out=w_out, ln1_g=ln1_g, ln1_b=ln1_b, w_router=w_router,
             b_router=b_router, w_gu=w_gu, b_gu=b_gu, w_dn=w_dn, b_dn=b_dn, ln2_g=ln2_g, ln2_b=ln2_b)
    p["w_gu_bf16"] = w_gu.astype(BF16)
    p["w_dn_bf16"] = w_dn.astype(BF16)
    batch, seq, d = x.shape
    assert batch == 1
    n_ctx = ctx.shape[1]
    depth = w_ada.shape[0]
    alpha = (2 * depth) ** 0.25
    lay = _layout(d)
    tables = _rope_tables(n_ctx, seq)
    cvec = jnp.zeros((8, d), F32).at[0].set(c[0]).at[1].set(c_ctx)
    mods = _ada(cvec, w_ada, b_ada)
    xall = jnp.concatenate([ctx[0], x[0]], axis=0)
    for l in range(depth):
        xall, _ = _layer(l, xall, mods[l], lay, tables, n_ctx, alpha, p)
    return xall[n_ctx:][None]
```

```python
import functools

import numpy as np
import jax
import jax.numpy as jnp
from jax import lax
from jax.experimental import pallas as pl
from jax.experimental.pallas import tpu as pltpu

F32 = jnp.float32
BF16 = jnp.bfloat16

GRID_W = 64
CHUNK = 128
CONV_W = 5
SSD_HEADDIM = 64
SSD_GROUPS = 4
SSD_STATE = 128
HEAD_DIM = 128
ATT_KV_HEADS = 2
ROPE_THETA = 10000.0
MLS_V_DIM = 256
MLS_QK_DIM = 128
TOP_K = 4
SWIGLU_LIMIT = 7.0
SWIGLU_ALPHA = 1.702

LANE = 128
SMALL_W = LANE
NEG_BIG = -1e30
LOG2_E = 1.4426950408889634
VMEM_LIMIT = 56 * 1024 * 1024


def _cparams(sem, vmem=VMEM_LIMIT):
    return pltpu.CompilerParams(dimension_semantics=sem, vmem_limit_bytes=vmem)


def _pick(n, cands):
    for c in cands:
        if n % c == 0:
            return c
    raise ValueError(f"no tile in {cands} divides {n}")


def _sigmoid(x):
    return 1.0 / (1.0 + jnp.exp(-x))


def _softplus(x):
    return jnp.maximum(x, 0.0) + jnp.log(1.0 + jnp.exp(-jnp.abs(x)))


def _split_bf16(a, n):
    parts = []
    r = a
    for _ in range(n):
        p = r.astype(BF16)
        parts.append(p)
        r = r - p.astype(F32)
    return parts


def _dot01_left(m01, a, n=3):
    out = None
    for p in _split_bf16(a, n):
        t = jnp.dot(m01, p, preferred_element_type=F32)
        out = t if out is None else out + t
    return out


def _dot01_right(a, m01, n=2):
    out = None
    for p in _split_bf16(a, n):
        t = jnp.dot(p, m01, preferred_element_type=F32)
        out = t if out is None else out + t
    return out


def _ada_kernel(c_ref, w_ref, b_ref, o_ref):
    c = c_ref[...]
    s = (c * _sigmoid(c)).astype(BF16)
    o_ref[...] = jnp.dot(s, w_ref[...].astype(BF16), preferred_element_type=F32) + b_ref[...]


def _ada(cvec, w_ada, b_ada):
    depth, d, n = w_ada.shape
    tn = _pick(n, (512, 256, 128))
    return pl.pallas_call(
        _ada_kernel,
        grid=(depth, n // tn),
        in_specs=[
            pl.BlockSpec((8, d), lambda l, j: (0, 0)),
            pl.BlockSpec((None, d, tn), lambda l, j: (l, 0, j)),
            pl.BlockSpec((None, 1, tn), lambda l, j: (l, 0, j)),
        ],
        out_specs=pl.BlockSpec((None, 8, tn), lambda l, j: (l, 0, j)),
        out_shape=jax.ShapeDtypeStruct((depth, 8, n), F32),
        compiler_params=_cparams(("arbitrary", "arbitrary")),
        name="ada_mod",
    )(cvec, w_ada, b_ada.reshape(depth, 1, n))


def _inproj_kernel(x_ref, sh_ref, sc_ref, w_ref, o_ref, h_scr, *, n_ctx, tm):
    i = pl.program_id(0)
    j = pl.program_id(1)

    @pl.when(j == 0)
    def _():
        rows = i * tm + lax.broadcasted_iota(jnp.int32, (tm, 1), 0)
        is_ctx = rows < n_ctx
        sc = jnp.where(is_ctx, sc_ref[1:2, :], sc_ref[0:1, :])
        sh = jnp.where(is_ctx, sh_ref[1:2, :], sh_ref[0:1, :])
        h_scr[...] = (x_ref[...] * (1.0 + sc) + sh).astype(BF16)

    o_ref[...] = jnp.dot(h_scr[...], w_ref[...], preferred_element_type=F32)


def _inproj(xall, mod, w, n_ctx):
    t, d = xall.shape
    n = w.shape[1]
    tm = _pick(t, (768, 512, 256, 128))
    tn = _pick(n, (512, 256, 128))
    return pl.pallas_call(
        functools.partial(_inproj_kernel, n_ctx=n_ctx, tm=tm),
        grid=(t // tm, n // tn),
        in_specs=[
            pl.BlockSpec((tm, d), lambda i, j: (i, 0), pipeline_mode=pl.Buffered(1)),
            pl.BlockSpec((8, d), lambda i, j: (0, 0)),
            pl.BlockSpec((8, d), lambda i, j: (0, 1)),
            pl.BlockSpec((d, tn), lambda i, j: (0, j)),
        ],
        out_specs=pl.BlockSpec((tm, tn), lambda i, j: (i, j)),
        out_shape=jax.ShapeDtypeStruct((t, n), F32),
        scratch_shapes=[pltpu.VMEM((tm, d), BF16)],
        compiler_params=_cparams(("arbitrary", "arbitrary")),
        name="in_proj",
    )(xall, mod, mod, w)


def _conv_kernel(x_ref, w_ref, b_ref, o_ref, *, n_ctx, tt):
    t = x_ref.shape[0]
    w = w_ref[...]
    bias = b_ref[...]
    halo = 8
    pad = CONV_W // 2

    def body(i, carry):
        t0 = pl.multiple_of(i * tt, tt)
        cur = x_ref[pl.ds(t0, tt), :]
        p0 = pl.multiple_of(jnp.maximum(t0 - halo, 0), halo)
        n0 = pl.multiple_of(jnp.minimum(t0 + tt, t - halo), halo)
        prev = x_ref[pl.ds(p0, halo), :]
        nxt = x_ref[pl.ds(n0, halo), :]
        seg_start = jnp.logical_or(t0 == 0, t0 == n_ctx)
        seg_end = jnp.logical_or(t0 + tt == n_ctx, t0 + tt == t)
        prev = jnp.where(seg_start, 0.0, prev)
        nxt = jnp.where(seg_end, 0.0, nxt)
        win = jnp.concatenate([prev, cur, nxt], axis=0)
        acc = bias + w[0:1, :] * win[halo - pad:halo - pad + tt, :]
        for k in range(1, CONV_W):
            acc = acc + w[k:k + 1, :] * win[halo - pad + k:halo - pad + k + tt, :]
        o_ref[pl.ds(t0, tt), :] = acc * _sigmoid(acc)
        return carry

    lax.fori_loop(0, t // tt, body, 0)


def _conv(proj, w8, b, col0, width, n_ctx):
    t = proj.shape[0]
    tt = _pick(n_ctx, (256, 128))
    assert t % tt == 0 and col0 % LANE == 0 and width % LANE == 0
    cb0 = col0 // LANE
    return pl.pallas_call(
        functools.partial(_conv_kernel, n_ctx=n_ctx, tt=tt),
        grid=(width // LANE,),
        in_specs=[
            pl.BlockSpec((t, LANE), lambda c: (0, cb0 + c)),
            pl.BlockSpec((8, LANE), lambda c: (0, c)),
            pl.BlockSpec((1, LANE), lambda c: (0, c)),
        ],
        out_specs=pl.BlockSpec((t, LANE), lambda c: (0, c)),
        out_shape=jax.ShapeDtypeStruct((t, width), F32),
        compiler_params=_cparams(("arbitrary",)),
        name="dwconv_silu",
    )(proj, w8, b)


def _chunk_order(i, n_chunks, n_ctx_chunks, reverse):
    if not reverse:
        return i
    return jnp.where(i < n_ctx_chunks, n_ctx_chunks - 1 - i, n_chunks - 1 - (i - n_ctx_chunks))


def _tri_mask(reverse):
    row = lax.broadcasted_iota(jnp.int32, (CHUNK, CHUNK), 0)
    col = lax.broadcasted_iota(jnp.int32, (CHUNK, CHUNK), 1)
    return (col >= row) if reverse else (col <= row)


def _ssd_kernel(*refs, reverse, finish, heads, hpg):
    if finish:
        (xs_ref, b_ref, c_ref, sm_ref, bias_ref, alog_ref, exp_ref,
         z_ref, yprev_ref, dskip_ref, g_ref, o_ref, h_scr) = refs
    else:
        (xs_ref, b_ref, c_ref, sm_ref, bias_ref, alog_ref, exp_ref, o_ref, h_scr) = refs
    gw = hpg * SSD_HEADDIM

    @pl.when(pl.program_id(0) == 0)
    def _():
        h_scr[...] = jnp.zeros_like(h_scr)

    mask = _tri_mask(reverse)
    tri = mask.astype(BF16)
    p = sm_ref[...] + bias_ref[...]
    dt = _softplus(p)
    dta = dt * (-jnp.exp(alog_ref[...]))
    cum = _dot01_left(tri, dta)
    cum_t = cum.T
    tot = cum[0:1, :] if reverse else cum[CHUNK - 1:CHUNK, :]
    e01 = exp_ref[...]
    dt_x = _dot01_right(dt, e01)
    in_x = _dot01_right(jnp.exp(cum), e01)
    tail_x = _dot01_right(jnp.exp(tot - cum), e01)
    tot_x = _dot01_right(jnp.broadcast_to(jnp.exp(tot), (8, SMALL_W)), e01)[0:1, :]

    xs = xs_ref[...]
    xdt = xs * dt_x
    xdt_b = xdt.astype(BF16)
    xtail_b = (xdt * tail_x).astype(BF16)
    bm = b_ref[...]
    cm = c_ref[...]
    c0 = heads if reverse else 0
    ys = []
    for g in range(SSD_GROUPS):
        bg = bm[:, g * SSD_STATE:(g + 1) * SSD_STATE]
        cg = cm[:, g * SSD_STATE:(g + 1) * SSD_STATE].astype(BF16)
        cb = lax.dot_general(cg, bg.astype(BF16), (((1,), (1,)), ((), ())),
                             preferred_element_type=F32)
        h_t = h_scr[g]
        y_g = jnp.dot(cg, h_t.astype(BF16), preferred_element_type=F32) * in_x[:, g * gw:(g + 1) * gw]
        parts = []
        for r in range(hpg):
            h = g * hpg + r
            c = c0 + h
            seg = cum[:, c:c + 1] - cum_t[c:c + 1, :]
            decay = jnp.exp(jnp.where(mask, seg, NEG_BIG))
            m = (cb * decay).astype(BF16)
            parts.append(jnp.dot(m, xdt_b[:, h * SSD_HEADDIM:(h + 1) * SSD_HEADDIM],
                                 preferred_element_type=F32))
        ys.append(y_g + jnp.concatenate(parts, axis=1))
        h_scr[g] = h_t * tot_x[:, g * gw:(g + 1) * gw] + jnp.dot(
            bg.T.astype(BF16), xtail_b[:, g * gw:(g + 1) * gw], preferred_element_type=F32)
    y = jnp.concatenate(ys, axis=1)

    if finish:
        z = z_ref[...]
        yt = (yprev_ref[...] + y + dskip_ref[...] * xs) * (z * _sigmoid(z))
        ms = jnp.mean(yt * yt, axis=-1, keepdims=True)
        o_ref[...] = (yt * lax.rsqrt(ms + 1e-6) * g_ref[...]).astype(o_ref.dtype)
    else:
        o_ref[...] = y


def _ssd_pass(conv_out, proj, lay, bias1, alog_row, e01, n_ctx, reverse, extra=None):
    t = proj.shape[0]
    nch, ncc = t // CHUNK, n_ctx // CHUNK
    d_ssd = lay["d_ssd"]
    heads = d_ssd // SSD_HEADDIM
    gs = SSD_GROUPS * SSD_STATE
    order = functools.partial(_chunk_order, n_chunks=nch, n_ctx_chunks=ncc, reverse=reverse)
    const = lambda i: (0, 0)
    in_specs = [
        pl.BlockSpec((CHUNK, d_ssd), lambda i: (order(i), lay["co_xs"] // d_ssd)),
        pl.BlockSpec((CHUNK, gs), lambda i: (order(i), lay["co_b"] // gs)),
        pl.BlockSpec((CHUNK, gs), lambda i: (order(i), lay["co_c"] // gs)),
        pl.BlockSpec((CHUNK, SMALL_W), lambda i: (order(i), lay["sm1"] // SMALL_W)),
        pl.BlockSpec((1, SMALL_W), const),
        pl.BlockSpec((1, SMALL_W), const),
        pl.BlockSpec((SMALL_W, d_ssd), const),
    ]
    args = [conv_out, conv_out, conv_out, proj, bias1, alog_row, e01]
    finish = extra is not None
    if finish:
        y_prev, dskip_x, norm_g = extra
        in_specs += [
            pl.BlockSpec((CHUNK, d_ssd), lambda i: (order(i), lay["z"] // d_ssd)),
            pl.BlockSpec((CHUNK, d_ssd), lambda i: (order(i), 0)),
            pl.BlockSpec((1, d_ssd), const),
            pl.BlockSpec((1, d_ssd), const),
        ]
        args += [proj, y_prev, dskip_x, norm_g]
    return pl.pallas_call(
        functools.partial(_ssd_kernel, reverse=reverse, finish=finish, heads=heads,
                          hpg=heads // SSD_GROUPS),
        grid=(nch,),
        in_specs=in_specs,
        out_specs=pl.BlockSpec((CHUNK, d_ssd), lambda i: (order(i), 0)),
        out_shape=jax.ShapeDtypeStruct((t, d_ssd), BF16 if finish else F32),
        scratch_shapes=[pltpu.VMEM((SSD_GROUPS, SSD_STATE, d_ssd // SSD_GROUPS), F32)],
        compiler_params=_cparams(("arbitrary",)),
        name="ssd_bwd_finish" if finish else "ssd_fwd",
    )(*args)


def _mls_kernel(*refs, reverse, finish, heads):
    if finish:
        (q_ref, k_ref, v_ref, sm1_ref, sm2_ref, b1_ref, b2_ref,
         og_ref, hprev_ref, gain_ref, o_ref, s_scr, m_scr) = refs
    else:
        (q_ref, k_ref, v_ref, sm1_ref, sm2_ref, b1_ref, b2_ref, o_ref, s_scr, m_scr) = refs

    @pl.when(pl.program_id(0) == 0)
    def _():
        s_scr[...] = jnp.zeros_like(s_scr)
        m_scr[...] = jnp.zeros_like(m_scr)

    mask = _tri_mask(reverse)
    tri = mask.astype(BF16)
    lf = -_softplus(-(sm1_ref[...] + b1_ref[...]))
    li = sm2_ref[...] + b2_ref[...]
    b = _dot01_left(tri, lf)
    b_t = b.T
    li_t = li.T
    tot = b[0:1, :] if reverse else b[CHUNK - 1:CHUNK, :]
    m_row = m_scr[...]
    g_all = tot - b + li
    m_new = jnp.maximum(tot + m_row, jnp.max(g_all, axis=0, keepdims=True))
    wk_all = jnp.exp(g_all - m_new)
    decay_row = jnp.exp(tot + m_row - m_new)
    mprev_all = b + m_row
    m_scr[...] = m_new

    q = q_ref[...]
    k = k_ref[...] * (MLS_QK_DIM ** -0.5)
    v = v_ref[...]
    ones_col = (lax.broadcasted_iota(jnp.int32, (CHUNK, LANE), 1) == 0).astype(F32)
    c0 = lay_fcol(heads, reverse)
    outs = []
    for h in range(heads):
        c = c0 + h
        qh = q[:, h * MLS_QK_DIM:(h + 1) * MLS_QK_DIM].astype(BF16)
        kh = k[:, h * MLS_QK_DIM:(h + 1) * MLS_QK_DIM]
        khb = kh.astype(BF16)
        vext = jnp.concatenate([v[:, h * MLS_V_DIM:(h + 1) * MLS_V_DIM], ones_col], axis=1)
        dmat = jnp.where(mask, b[:, c:c + 1] - b_t[c:c + 1, :] + li_t[c:c + 1, :], NEG_BIG)
        m_prev = mprev_all[:, c:c + 1]
        m_t = jnp.maximum(m_prev, jnp.max(dmat, axis=1, keepdims=True))
        qk = lax.dot_general(qh, khb, (((1,), (1,)), ((), ())), preferred_element_type=F32)
        w = jnp.exp(dmat - m_t) * qk
        s_prev = jnp.exp(m_prev - m_t)
        s_h = s_scr[h]
        numx = (jnp.dot(w.astype(BF16), vext.astype(BF16), preferred_element_type=F32)
                + s_prev * jnp.dot(qh, s_h.astype(BF16), preferred_element_type=F32))
        den = numx[:, MLS_V_DIM:MLS_V_DIM + 1]
        outs.append(numx[:, :MLS_V_DIM] / jnp.maximum(jnp.abs(den), jnp.exp(-m_t)))
        s_scr[h] = decay_row[:, c:c + 1] * s_h + jnp.dot(
            kh.T.astype(BF16), (wk_all[:, c:c + 1] * vext).astype(BF16), preferred_element_type=F32)

    if finish:
        hp = hprev_ref[...]
        og = og_ref[...]
        gain = gain_ref[...]
        for h in range(heads):
            sl = slice(h * MLS_V_DIM, (h + 1) * MLS_V_DIM)
            hs = hp[:, sl] + outs[h]
            ms = jnp.mean(hs * hs, axis=-1, keepdims=True)
            o_ref[:, sl] = (_sigmoid(og[:, sl]) * (hs * lax.rsqrt(ms + 1e-6) * gain[:, sl])).astype(o_ref.dtype)
    else:
        for h in range(heads):
            o_ref[:, h * MLS_V_DIM:(h + 1) * MLS_V_DIM] = outs[h]


def lay_fcol(heads, reverse):
    return GATE_COL0 + (heads if reverse else 0)


def _mls_pass(conv_out, proj, lay, bias1, bias2, n_ctx, reverse, extra=None):
    t = proj.shape[0]
    nch, ncc = t // CHUNK, n_ctx // CHUNK
    d_mls = lay["d_mls"]
    heads = d_mls // MLS_V_DIM
    qw = heads * MLS_QK_DIM
    order = functools.partial(_chunk_order, n_chunks=nch, n_ctx_chunks=ncc, reverse=reverse)
    const = lambda i: (0, 0)
    in_specs = [
        pl.BlockSpec((CHUNK, qw), lambda i: (order(i), lay["co_q"] // qw)),
        pl.BlockSpec((CHUNK, qw), lambda i: (order(i), lay["co_k"] // qw)),
        pl.BlockSpec((CHUNK, d_mls), lambda i: (order(i), lay["mv"] // d_mls)),
        pl.BlockSpec((CHUNK, SMALL_W), lambda i: (order(i), lay["sm1"] // SMALL_W)),
        pl.BlockSpec((CHUNK, SMALL_W), lambda i: (order(i), lay["sm2"] // SMALL_W)),
        pl.BlockSpec((1, SMALL_W), const),
        pl.BlockSpec((1, SMALL_W), const),
    ]
    args = [conv_out, conv_out, proj, proj, proj, bias1, bias2]
    finish = extra is not None
    if finish:
        h_prev, gain = extra
        in_specs += [
            pl.BlockSpec((CHUNK, d_mls), lambda i: (order(i), lay["mo"] // d_mls)),
            pl.BlockSpec((CHUNK, d_mls), lambda i: (order(i), 0)),
            pl.BlockSpec((1, d_mls), const),
        ]
        args += [proj, h_prev, gain]
    return pl.pallas_call(
        functools.partial(_mls_kernel, reverse=reverse, finish=finish, heads=heads),
        grid=(nch,),
        in_specs=in_specs,
        out_specs=pl.BlockSpec((CHUNK, d_mls), lambda i: (order(i), 0)),
        out_shape=jax.ShapeDtypeStruct((t, d_mls), BF16 if finish else F32),
        scratch_shapes=[pltpu.VMEM((heads, MLS_QK_DIM, MLS_V_DIM + LANE), F32),
                        pltpu.VMEM((1, SMALL_W), F32)],
        compiler_params=_cparams(("arbitrary",)),
        name="mlstm_bwd_finish" if finish else "mlstm_fwd",
    )(*args)


def _qk_prep_kernel(q_ref, k_ref, v_ref, cos_ref, sin_ref, gq_ref, gk_ref, qo_ref, ko_ref, vo_ref):
    cos = cos_ref[...]
    sin = sin_ref[...]

    even = lax.broadcasted_iota(jnp.int32, cos.shape, 1) % 2 == 0

    def norm_rope(xh, g, scale):
        ms = jnp.mean(xh * xh, axis=-1, keepdims=True)
        xn = xh * lax.rsqrt(ms + 1e-6) * g
        partner = jnp.where(even, pltpu.roll(xn, HEAD_DIM - 1, axis=1), pltpu.roll(xn, 1, axis=1))
        return (xn * cos + partner * sin) * scale

    q = q_ref[...]
    for h in range(q.shape[1] // HEAD_DIM):
        sl = slice(h * HEAD_DIM, (h + 1) * HEAD_DIM)
        qo_ref[:, sl] = norm_rope(q[:, sl], gq_ref[...], LOG2_E * HEAD_DIM ** -0.5).astype(qo_ref.dtype)
    k = k_ref[...]
    for h in range(k.shape[1] // HEAD_DIM):
        sl = slice(h * HEAD_DIM, (h + 1) * HEAD_DIM)
        ko_ref[:, sl] = norm_rope(k[:, sl], gk_ref[...], 1.0).astype(ko_ref.dtype)
    v = v_ref[...]
    ones = jnp.ones((v.shape[0], HEAD_DIM), vo_ref.dtype)
    for h in range(v.shape[1] // HEAD_DIM):
        vo_ref[:, 2 * h * HEAD_DIM:(2 * h + 1) * HEAD_DIM] = v[:, h * HEAD_DIM:(h + 1) * HEAD_DIM].astype(vo_ref.dtype)
        vo_ref[:, (2 * h + 1) * HEAD_DIM:(2 * h + 2) * HEAD_DIM] = ones


def _qk_prep(proj, lay, cos_t, sin_t, gq, gk):
    t = proj.shape[0]
    tm = _pick(t, (256, 128))
    d_att = lay["d_att"]
    kvw = ATT_KV_HEADS * HEAD_DIM
    const = lambda i: (0, 0)
    return pl.pallas_call(
        _qk_prep_kernel,
        grid=(t // tm,),
        in_specs=[
            pl.BlockSpec((tm, d_att), lambda i: (i, lay["aq"] // d_att)),
            pl.BlockSpec((tm, kvw), lambda i: (i, lay["ak"] // kvw)),
            pl.BlockSpec((tm, kvw), lambda i: (i, lay["av"] // kvw)),
            pl.BlockSpec((tm, HEAD_DIM), lambda i: (i, 0)),
            pl.BlockSpec((tm, HEAD_DIM), lambda i: (i, 0)),
            pl.BlockSpec((1, HEAD_DIM), const),
            pl.BlockSpec((1, HEAD_DIM), const),
        ],
        out_specs=[
            pl.BlockSpec((tm, d_att), lambda i: (i, 0)),
            pl.BlockSpec((tm, kvw), lambda i: (i, 0)),
            pl.BlockSpec((tm, 2 * kvw), lambda i: (i, 0)),
        ],
        out_shape=[jax.ShapeDtypeStruct((t, d_att), BF16),
                   jax.ShapeDtypeStruct((t, kvw), BF16),
                   jax.ShapeDtypeStruct((t, 2 * kvw), BF16)],
        compiler_params=_cparams(("arbitrary",)),
        name="qk_norm_rope",
    )(proj, proj, proj, cos_t, sin_t, gq, gk)


FLASH_ROW_BLOCK = 16
FLASH_KV_SPLIT = 256


def _flash_kernel(q_ref, k_ref, v_ref, o_ref, s_scr, p_scr, m_scr, a_scr, acc_scr, *, group, tq):
    j = pl.program_id(2)

    @pl.when(j == 0)
    def _():
        m_scr[...] = jnp.full_like(m_scr, NEG_BIG)
        acc_scr[...] = jnp.zeros_like(acc_scr)

    n_rows = group * tq
    n_split, _, tks = s_scr.shape
    rb = FLASH_ROW_BLOCK
    lane_tiles = [slice(t * LANE, (t + 1) * LANE) for t in range(tks // LANE)]
    for c in range(n_split):
        kc = k_ref[c * tks:(c + 1) * tks, :]
        for h in range(group):
            s_scr[c, h * tq:(h + 1) * tq, :] = lax.dot_general(
                q_ref[:, h * HEAD_DIM:(h + 1) * HEAD_DIM], kc, (((1,), (1,)), ((), ())),
                preferred_element_type=F32)
    for c in range(n_split):
        for b in range(n_rows // rb):
            rows = slice(b * rb, (b + 1) * rb)
            mx = s_scr[c, rows, lane_tiles[0]]
            for tile in lane_tiles[1:]:
                mx = jnp.maximum(mx, s_scr[c, rows, tile])
            m_old = m_scr[rows, :]
            m_new = jnp.maximum(m_old, jnp.broadcast_to(jnp.max(mx, axis=1, keepdims=True), (rb, LANE)))
            m_scr[rows, :] = m_new
            a_scr[rows, :] = jnp.exp2(m_old - m_new)
        for b in range(n_rows // rb):
            rows = slice(b * rb, (b + 1) * rb)
            m = m_scr[rows, :]
            for tile in lane_tiles:
                p_scr[c, rows, tile] = jnp.exp2(s_scr[c, rows, tile] - m).astype(BF16)
            alpha = a_scr[rows, :]
            acc_scr[rows, :] = acc_scr[rows, :] * jnp.concatenate([alpha, alpha], axis=1)
        acc_scr[...] += jnp.dot(p_scr[c], v_ref[c * tks:(c + 1) * tks, :], preferred_element_type=F32)

    @pl.when(j == pl.num_programs(2) - 1)
    def _():
        for h in range(group):
            rows = slice(h * tq, (h + 1) * tq)
            o_ref[:, h * HEAD_DIM:(h + 1) * HEAD_DIM] = (
                acc_scr[rows, :HEAD_DIM] / acc_scr[rows, HEAD_DIM:]).astype(o_ref.dtype)


def _flash(qn, kn, vb, n_k):
    n_q, d_att = qn.shape
    group = d_att // HEAD_DIM // ATT_KV_HEADS
    gw = group * HEAD_DIM
    tq = _pick(n_q, (512, 256, 128))
    tk = _pick(n_k, (768, 512, 256, 128))
    tks = min(tk, FLASH_KV_SPLIT)
    return pl.pallas_call(
        functools.partial(_flash_kernel, group=group, tq=tq),
        grid=(ATT_KV_HEADS, n_q // tq, n_k // tk),
        in_specs=[
            pl.BlockSpec((tq, gw), lambda g, i, j: (i, g)),
            pl.BlockSpec((tk, HEAD_DIM), lambda g, i, j: (j, g)),
            pl.BlockSpec((tk, 2 * HEAD_DIM), lambda g, i, j: (j, g)),
        ],
        out_specs=pl.BlockSpec((tq, gw), lambda g, i, j: (i, g)),
        out_shape=jax.ShapeDtypeStruct((n_q, d_att), BF16),
        scratch_shapes=[pltpu.VMEM((tk // tks, group * tq, tks), F32),
                        pltpu.VMEM((tk // tks, group * tq, tks), BF16),
                        pltpu.VMEM((group * tq, LANE), F32),
                        pltpu.VMEM((group * tq, LANE), F32),
                        pltpu.VMEM((group * tq, 2 * HEAD_DIM), F32)],
        compiler_params=_cparams(("arbitrary", "arbitrary", "arbitrary")),
        name="flash_gqa",
    )(qn, kn, vb)


def _outproj_kernel(a1_ref, a2_ref, a3_ref, w1_ref, w2_ref, w3_ref, o_ref):
    acc = jnp.dot(a1_ref[...], w1_ref[...], preferred_element_type=F32)
    acc = acc + jnp.dot(a2_ref[...], w2_ref[...], preferred_element_type=F32)
    acc = acc + jnp.dot(a3_ref[...], w3_ref[...], preferred_element_type=F32)
    o_ref[...] = acc


def _outproj(a1, a2, a3, w1, w2, w3):
    t = a1.shape[0]
    n = w1.shape[1]
    tm = _pick(t, (768, 512, 256, 128))
    tn = _pick(n, (512, 256, 128))
    lhs = lambda a: pl.BlockSpec((tm, a.shape[1]), lambda i, j: (i, 0))
    rhs = lambda w: pl.BlockSpec((w.shape[0], tn), lambda i, j: (0, j))
    return pl.pallas_call(
        _outproj_kernel,
        grid=(t // tm, n // tn),
        in_specs=[lhs(a1), lhs(a2), lhs(a3), rhs(w1), rhs(w2), rhs(w3)],
        out_specs=pl.BlockSpec((tm, tn), lambda i, j: (i, j)),
        out_shape=jax.ShapeDtypeStruct((t, n), F32),
        compiler_params=_cparams(("arbitrary", "arbitrary")),
        name="out_proj",
    )(a1, a2, a3, w1, w2, w3)


def _topk_route(logits, n_experts):
    lane = lax.broadcasted_iota(jnp.int32, logits.shape, 1)
    l = jnp.where(lane < n_experts, logits, NEG_BIG)
    tops = []
    for _ in range(TOP_K):
        m = jnp.max(l, axis=1, keepdims=True)
        idx = jnp.min(jnp.where(l == m, lane, LANE), axis=1, keepdims=True)
        tops.append((m, idx))
        l = jnp.where(lane == idx, NEG_BIG, l)
    es = [jnp.exp(m - tops[0][0]) for m, _ in tops]
    den = es[0]
    for e in es[1:]:
        den = den + e
    wts = jnp.zeros(logits.shape, F32)
    ids = jnp.zeros(logits.shape, jnp.int32)
    for k, ((m, idx), e) in enumerate(zip(tops, es)):
        wts = jnp.where(lane == k, e / den, wts)
        ids = jnp.where(lane == k, idx, ids)
    return wts, ids


def _ln_route_kernel(x_ref, y_ref, gate_ref, lng_ref, lnb_ref, sh_ref, sc_ref, wr_ref, br_ref,
                     xo_ref, tok_ref, wts_ref, ids_ref, *, n_ctx, tm, alpha, n_experts):
    rows = pl.program_id(0) * tm + lax.broadcasted_iota(jnp.int32, (tm, 1), 0)
    is_ctx = rows < n_ctx
    gate = jnp.where(is_ctx, gate_ref[1:2, :], gate_ref[0:1, :])
    u = alpha * x_ref[...] + gate * y_ref[...]
    mu = jnp.mean(u, axis=-1, keepdims=True)
    uc = u - mu
    var = jnp.mean(uc * uc, axis=-1, keepdims=True)
    xn = uc * lax.rsqrt(var + 1e-5) * lng_ref[...] + lnb_ref[...]
    xo_ref[...] = xn
    sc = jnp.where(is_ctx, sc_ref[1:2, :], sc_ref[0:1, :])
    sh = jnp.where(is_ctx, sh_ref[1:2, :], sh_ref[0:1, :])
    tok = xn * (1.0 + sc) + sh
    tok_ref[...] = _pack_bf16_pairs(tok)
    t_hi, t_lo = _split_bf16(tok, 2)
    w_hi, w_lo = _split_bf16(wr_ref[...], 2)
    logits = (jnp.dot(t_hi, w_hi, preferred_element_type=F32)
              + jnp.dot(t_hi, w_lo, preferred_element_type=F32)
              + jnp.dot(t_lo, w_hi, preferred_element_type=F32)) + br_ref[...]
    wts_ref[...], ids_ref[...] = _topk_route(logits, n_experts)


def _resid_ln_route(xall, y, mod, gate_blk, sh_blk, sc_blk, lng, lnb, w_router, b_router, n_experts, n_ctx, alpha):
    t, d = xall.shape
    tm = _pick(t, (256, 128))
    const = lambda i: (0, 0)
    row = pl.BlockSpec((tm, d), lambda i: (i, 0))
    lanes = pl.BlockSpec((tm, LANE), lambda i: (i, 0))
    mod_blk = lambda blk: pl.BlockSpec((8, d), lambda i: (0, blk))
    return pl.pallas_call(
        functools.partial(_ln_route_kernel, n_ctx=n_ctx, tm=tm, alpha=alpha, n_experts=n_experts),
        grid=(t // tm,),
        in_specs=[row, row, mod_blk(gate_blk), pl.BlockSpec((1, d), const), pl.BlockSpec((1, d), const),
                  mod_blk(sh_blk), mod_blk(sc_blk), pl.BlockSpec((d, LANE), const),
                  pl.BlockSpec((1, LANE), const)],
        out_specs=[row, pl.BlockSpec((tm, d // 2), lambda i: (i, 0)), lanes, lanes],
        out_shape=[jax.ShapeDtypeStruct((t, d), F32), jax.ShapeDtypeStruct((t, d // 2), jnp.uint32),
                   jax.ShapeDtypeStruct((t, LANE), F32), jax.ShapeDtypeStruct((t, LANE), jnp.int32)],
        compiler_params=_cparams(("arbitrary",)),
        name="resid_ln_route",
    )(xall, y, mod, lng, lnb, mod, mod, w_router, b_router)


MOE_ROW_TILE = 256
COMBINE_TILE = 128


def _route_plan(ids, n_experts, tm):
    t = ids.shape[0]
    n_rows = t * TOP_K
    flat = ids[:, :TOP_K].reshape(-1)
    onehot = (flat[:, None] == jnp.arange(n_experts, dtype=jnp.int32)[None, :]).astype(jnp.int32)
    csum = jnp.cumsum(onehot, axis=0)
    rank = jnp.sum((csum - onehot) * onehot, axis=1)
    cnt = csum[-1]
    ptiles = (cnt + tm - 1) // tm
    tile_end = jnp.cumsum(ptiles)
    row_start = (tile_end - ptiles) * tm
    pos = (jnp.sum(onehot * row_start[None, :], axis=1) + rank).astype(jnp.int32)
    n_tiles = (n_rows + n_experts * (tm - 1)) // tm
    row_token = jnp.zeros((n_tiles * tm,), jnp.int32).at[pos].set(
        jnp.arange(n_rows, dtype=jnp.int32) // TOP_K)
    tiles = jnp.arange(n_tiles, dtype=jnp.int32)
    tile_expert = jnp.minimum(jnp.sum((tiles[:, None] >= tile_end[None, :]).astype(jnp.int32), axis=1),
                              n_experts - 1).astype(jnp.int32)
    tile_valid = (tiles < tile_end[-1]).astype(jnp.int32)
    return pos, row_token, tile_expert, tile_valid, n_tiles


HI16 = 0xFFFF0000


def _pack_bf16_pairs(x):
    half = x.shape[1] // 2
    lo = pltpu.bitcast(x[:, :half].astype(BF16).astype(F32), jnp.uint32) >> 16
    hi = pltpu.bitcast(x[:, half:].astype(BF16).astype(F32), jnp.uint32) & jnp.uint32(HI16)
    return hi | lo


def _unpack_bf16_pairs(w):
    return pltpu.bitcast(w << 16, F32), pltpu.bitcast(w & jnp.uint32(HI16), F32)


def _moe_kernel(te_ref, tv_ref, rt_ref, tok_hbm, wgu_ref, bgu_ref, wdn_ref, bdn_ref, sel_ref, o_ref,
                buf, sem, *, tm, n_tiles):
    i = pl.program_id(0)

    def issue(tile, slot):
        def body(r, carry):
            row = rt_ref[tile * tm + r]
            pltpu.make_async_copy(tok_hbm.at[pl.ds(row, 1), :], buf.at[slot, pl.ds(r, 1), :],
                                  sem.at[slot]).start()
            return carry
        lax.fori_loop(0, tm, body, 0, unroll=8)

    @pl.when(jnp.logical_and(i == 0, tv_ref[0] > 0))
    def _():
        issue(0, 0)

    nxt = jnp.minimum(i + 1, n_tiles - 1)

    @pl.when(jnp.logical_and(i + 1 < n_tiles, tv_ref[nxt] > 0))
    def _():
        issue(nxt, nxt % 2)

    slot = i % 2

    @pl.when(tv_ref[i] > 0)
    def _():
        pltpu.make_async_copy(tok_hbm.at[pl.ds(0, tm), :], buf.at[slot], sem.at[slot]).wait()
        x_lo, x_hi = _unpack_bf16_pairs(buf[slot])
        half = x_lo.shape[1]
        gu = (jnp.dot(x_lo.astype(BF16), wgu_ref[:half, :], preferred_element_type=F32)
              + jnp.dot(x_hi.astype(BF16), wgu_ref[half:, :], preferred_element_type=F32)) + bgu_ref[...]
        g = jnp.minimum(gu, SWIGLU_LIMIT)
        a = g * _sigmoid(SWIGLU_ALPHA * g)
        lin1 = jnp.clip(gu, -SWIGLU_LIMIT, SWIGLU_LIMIT) + 1.0
        pair = (a * pltpu.roll(lin1, gu.shape[1] - 1, axis=1)).astype(BF16)
        act = jnp.dot(pair, sel_ref[...], preferred_element_type=F32).astype(BF16)
        o_ref[...] = _pack_bf16_pairs(jnp.dot(act, wdn_ref[...], preferred_element_type=F32) + bdn_ref[...])

    @pl.when(tv_ref[i] == 0)
    def _():
        o_ref[...] = jnp.zeros_like(o_ref)


def _moe_experts(l, tok, plan, wgu, bgu, wdn, bdn):
    _, row_token, tile_expert, tile_valid, n_tiles = plan
    tm = MOE_ROW_TILE
    half = tok.shape[1]
    d = 2 * half
    two_de, de = wgu.shape[3], wdn.shape[2]
    sel = jnp.asarray(np.arange(two_de)[:, None] == 2 * np.arange(de)[None, :], BF16)
    grid_spec = pltpu.PrefetchScalarGridSpec(
        num_scalar_prefetch=3,
        grid=(n_tiles,),
        in_specs=[
            pl.BlockSpec(memory_space=pl.ANY),
            pl.BlockSpec((None, None, d, two_de), lambda i, te, tv, rt: (l, te[i], 0, 0)),
            pl.BlockSpec((None, None, 1, two_de), lambda i, te, tv, rt: (l, te[i], 0, 0)),
            pl.BlockSpec((None, None, de, d), lambda i, te, tv, rt: (l, te[i], 0, 0)),
            pl.BlockSpec((None, None, 1, d), lambda i, te, tv, rt: (l, te[i], 0, 0)),
            pl.BlockSpec((two_de, de), lambda i, te, tv, rt: (0, 0)),
        ],
        out_specs=pl.BlockSpec((tm, half), lambda i, te, tv, rt: (i, 0)),
        scratch_shapes=[pltpu.VMEM((2, tm, half), jnp.uint32), pltpu.SemaphoreType.DMA((2,))],
    )
    return pl.pallas_call(
        functools.partial(_moe_kernel, tm=tm, n_tiles=n_tiles),
        grid_spec=grid_spec,
        out_shape=jax.ShapeDtypeStruct((n_tiles * tm, half), jnp.uint32),
        compiler_params=_cparams(("arbitrary",)),
        name="moe_experts",
    )(tile_expert, tile_valid, row_token, tok, wgu, bgu, wdn, bdn, sel)


def _combine_ln_kernel(pos_ref, y_hbm, x_ref, wts_ref, gate_ref, lng_ref, lnb_ref, xo_ref, buf, sem,
                       *, n_ctx, tm, alpha, n_steps):
    i = pl.program_id(0)

    def issue(tile, slot):
        def body(r, carry):
            for k in range(TOP_K):
                row = pos_ref[(tile * tm + r) * TOP_K + k]
                pltpu.make_async_copy(y_hbm.at[pl.ds(row, 1), :], buf.at[slot, k, pl.ds(r, 1), :],
                                      sem.at[slot]).start()
            return carry
        lax.fori_loop(0, tm, body, 0, unroll=4)

    @pl.when(i == 0)
    def _():
        issue(0, 0)

    @pl.when(i + 1 < n_steps)
    def _():
        issue(i + 1, (i + 1) % 2)

    slot = i % 2
    for k in range(TOP_K):
        pltpu.make_async_copy(y_hbm.at[pl.ds(0, tm), :], buf.at[slot, k], sem.at[slot]).wait()
    wts = wts_ref[...]
    f_lo, f_hi = _unpack_bf16_pairs(buf[slot, 0])
    f_lo, f_hi = wts[:, 0:1] * f_lo, wts[:, 0:1] * f_hi
    for k in range(1, TOP_K):
        y_lo, y_hi = _unpack_bf16_pairs(buf[slot, k])
        f_lo = f_lo + wts[:, k:k + 1] * y_lo
        f_hi = f_hi + wts[:, k:k + 1] * y_hi
    f = jnp.concatenate([f_lo, f_hi], axis=1)
    rows = i * tm + lax.broadcasted_iota(jnp.int32, (tm, 1), 0)
    gate = jnp.where(rows < n_ctx, gate_ref[1:2, :], gate_ref[0:1, :])
    u = alpha * x_ref[...] + gate * f
    mu = jnp.mean(u, axis=-1, keepdims=True)
    uc = u - mu
    var = jnp.mean(uc * uc, axis=-1, keepdims=True)
    xo_ref[...] = uc * lax.rsqrt(var + 1e-5) * lng_ref[...] + lnb_ref[...]


def _combine_ln(x1, y_sorted, pos, wts, mod, gate_blk, lng, lnb, n_ctx, alpha):
    t, d = x1.shape
    tm = COMBINE_TILE
    n_steps = t // tm
    const = lambda i, p: (0, 0)
    grid_spec = pltpu.PrefetchScalarGridSpec(
        num_scalar_prefetch=1,
        grid=(n_steps,),
        in_specs=[
            pl.BlockSpec(memory_space=pl.ANY),
            pl.BlockSpec((tm, d), lambda i, p: (i, 0)),
            pl.BlockSpec((tm, LANE), lambda i, p: (i, 0)),
            pl.BlockSpec((8, d), lambda i, p: (0, gate_blk)),
            pl.BlockSpec((1, d), const),
            pl.BlockSpec((1, d), const),
        ],
        out_specs=pl.BlockSpec((tm, d), lambda i, p: (i, 0)),
        scratch_shapes=[pltpu.VMEM((2, TOP_K, tm, d // 2), jnp.uint32), pltpu.SemaphoreType.DMA((2,))],
    )
    return pl.pallas_call(
        functools.partial(_combine_ln_kernel, n_ctx=n_ctx, tm=tm, alpha=alpha, n_steps=n_steps),
        grid_spec=grid_spec,
        out_shape=jax.ShapeDtypeStruct((t, d), F32),
        compiler_params=_cparams(("arbitrary",)),
        name="moe_combine_ln",
    )(pos, y_sorted, x1, wts, mod, lng, lnb)


GATE_COL0 = 48


def _layout(d_model):
    d_ssd = 3 * d_model // 8
    d_att = d_model // 4
    d_mls = d_model - d_ssd - d_att
    ssd_heads = d_ssd // SSD_HEADDIM
    mls_heads = d_mls // MLS_V_DIM
    gs = SSD_GROUPS * SSD_STATE
    kvw = ATT_KV_HEADS * HEAD_DIM
    qkw = mls_heads * MLS_QK_DIM
    assert 2 * ssd_heads == GATE_COL0 and GATE_COL0 + 2 * mls_heads <= SMALL_W
    sizes = (d_ssd, d_ssd + 2 * gs, 2 * ssd_heads, d_att, kvw, kvw, 2 * qkw, d_mls, d_mls, 4 * mls_heads)
    o = np.concatenate([[0], np.cumsum(sizes)])
    src = dict(z=o[0], xbc=o[1], dt=o[2], aq=o[3], ak=o[4], av=o[5], mqk=o[6], mv=o[7], mo=o[8], mg=o[9])
    lay = dict(d_ssd=d_ssd, d_att=d_att, d_mls=d_mls, ssd_heads=ssd_heads, mls_heads=mls_heads,
               src={k: int(v) for k, v in src.items()}, gs=gs, kvw=kvw, qkw=qkw)
    cols = []

    def put(name, idx):
        lay[name] = len(cols)
        cols.extend(int(v) for v in idx)

    put("z", src["z"] + np.arange(d_ssd))
    put("mv", src["mv"] + np.arange(d_mls))
    put("mo", src["mo"] + np.arange(d_mls))
    put("mqk", src["mqk"] + np.arange(2 * qkw))
    put("xbc", src["xbc"] + np.arange(d_ssd + 2 * gs))
    put("ak", src["ak"] + np.arange(kvw))
    put("av", src["av"] + np.arange(kvw))
    put("aq", src["aq"] + np.arange(d_att))
    sm1 = -np.ones(SMALL_W, np.int64)
    sm1[:2 * ssd_heads] = src["dt"] + np.arange(2 * ssd_heads)
    sm2 = -np.ones(SMALL_W, np.int64)
    for direction in range(2):
        c = GATE_COL0 + direction * mls_heads
        sm1[c:c + mls_heads] = src["mg"] + (2 * direction + 1) * mls_heads + np.arange(mls_heads)
        sm2[c:c + mls_heads] = src["mg"] + (2 * direction) * mls_heads + np.arange(mls_heads)
    put("sm1", sm1)
    put("sm2", sm2)
    n_pad = -len(cols) % 512
    cols.extend([-1] * n_pad)
    lay["cols"] = np.asarray(cols, np.int64)
    lay["conv0"] = lay["mqk"]
    lay["conv_w"] = 2 * qkw + d_ssd + 2 * gs
    lay["co_q"], lay["co_k"], lay["co_xs"] = 0, qkw, 2 * qkw
    lay["co_b"], lay["co_c"] = 2 * qkw + d_ssd, 2 * qkw + d_ssd + gs
    for name, width in (("z", d_ssd), ("mv", d_mls), ("mo", d_mls), ("aq", d_att), ("ak", kvw), ("av", kvw),
                        ("sm1", SMALL_W), ("sm2", SMALL_W), ("conv0", LANE)):
        assert lay[name] % width == 0, name
    assert lay["co_k"] % qkw == 0 and lay["co_xs"] % d_ssd == 0 and lay["co_b"] % gs == 0 and lay["co_c"] % gs == 0
    return lay


def _relayout_w_in(w, lay):
    d = w.shape[0]
    s = lay["src"]
    sh, mh = lay["ssd_heads"], lay["mls_heads"]

    def seg(name, width):
        return w[:, s[name]:s[name] + width]

    zeros = lambda n: jnp.zeros((d, n), w.dtype)
    mg = seg("mg", 4 * mh).reshape(d, 4, mh)
    tail = SMALL_W - GATE_COL0 - 2 * mh
    parts = [seg("z", lay["d_ssd"]), seg("mv", lay["d_mls"]), seg("mo", lay["d_mls"]),
             seg("mqk", 2 * lay["qkw"]), seg("xbc", lay["d_ssd"] + 2 * lay["gs"]),
             seg("ak", lay["kvw"]), seg("av", lay["kvw"]), seg("aq", lay["d_att"]),
             seg("dt", 2 * sh), mg[:, 1], mg[:, 3], zeros(tail),
             zeros(GATE_COL0), mg[:, 0], mg[:, 2], zeros(tail)]
    parts.append(zeros(len(lay["cols"]) - sum(a.shape[1] for a in parts)))
    return jnp.concatenate(parts, axis=1).astype(BF16)


def _rope_tables(n_ctx, seq):
    rows = seq // GRID_W
    row = jnp.repeat(jnp.arange(rows), GRID_W).astype(F32)
    col = jnp.tile(jnp.arange(GRID_W), rows).astype(F32)
    n_freq = HEAD_DIM // 4
    inv = ROPE_THETA ** (-jnp.arange(n_freq, dtype=F32) / n_freq)
    ang = jnp.concatenate([row[:, None] * inv, col[:, None] * inv], axis=-1)
    cos, sin = jnp.cos(ang), jnp.sin(ang)
    cos_pairs = jnp.repeat(cos, 2, axis=-1)
    sin_pairs = jnp.stack([-sin, sin], axis=-1).reshape(seq, HEAD_DIM)
    cos_t = jnp.concatenate([jnp.ones((n_ctx, HEAD_DIM), F32), cos_pairs], axis=0)
    sin_t = jnp.concatenate([jnp.zeros((n_ctx, HEAD_DIM), F32), sin_pairs], axis=0)
    return cos_t, sin_t


def _pad_row(v, width=SMALL_W, at=0):
    out = jnp.zeros((1, width), F32)
    return out.at[0, at:at + v.shape[0]].set(v.astype(F32))


def _layer(l, xall, mod, lay, tables, n_ctx, alpha, p):
    d = xall.shape[1]
    proj = _inproj(xall, mod, _relayout_w_in(p["w_in"][l], lay), n_ctx)

    conv_w = jnp.concatenate([p["mls_conv_w"][l], p["ssd_conv_w"][l]], axis=1)
    conv_w8 = jnp.concatenate([conv_w, jnp.zeros((8 - CONV_W, conv_w.shape[1]), F32)], axis=0)
    conv_b = jnp.concatenate([p["mls_conv_b"][l], p["ssd_conv_b"][l]])[None, :]
    conv_out = _conv(proj, conv_w8, conv_b, lay["conv0"], lay["conv_w"], n_ctx)

    sh, mh = lay["ssd_heads"], lay["mls_heads"]
    gate_b = p["mls_gate_b"][l]
    bias1 = (_pad_row(p["ssd_dt_bias"][l].reshape(-1))
             + _pad_row(gate_b[1], at=GATE_COL0) + _pad_row(gate_b[3], at=GATE_COL0 + mh))
    bias2 = _pad_row(gate_b[0], at=GATE_COL0) + _pad_row(gate_b[2], at=GATE_COL0 + mh)
    alog_row = _pad_row(p["ssd_A_log"][l].reshape(-1))
    hcol = np.repeat(np.arange(sh), SSD_HEADDIM)
    e_f = jnp.asarray(np.arange(SMALL_W)[:, None] == hcol[None, :], BF16)
    e_b = jnp.asarray(np.arange(SMALL_W)[:, None] == (hcol + sh)[None, :], BF16)
    dskip_x = jnp.repeat(p["ssd_D"][l], SSD_HEADDIM)[None, :]
    y_f = _ssd_pass(conv_out, proj, lay, bias1, alog_row, e_f, n_ctx, False)
    ssd = _ssd_pass(conv_out, proj, lay, bias1, alog_row, e_b, n_ctx, True,
                    extra=(y_f, dskip_x, p["ssd_norm_g"][l][None, :]))

    h_f = _mls_pass(conv_out, proj, lay, bias1, bias2, n_ctx, False)
    mls = _mls_pass(conv_out, proj, lay, bias1, bias2, n_ctx, True, extra=(h_f, p["mls_norm_g"][l][None, :]))

    gq = p["att_q_norm_g"][l][None, :]
    gk = p["att_k_norm_g"][l][None, :]
    qn, kn, vb = _qk_prep(proj, lay, tables[0], tables[1], gq, gk)
    t = xall.shape[0]
    att_c = _flash(qn[:n_ctx], kn, vb, n_ctx)
    att_l = _flash(qn[n_ctx:], kn, vb, t)
    att = jnp.concatenate([att_c, att_l], axis=0)

    w_out = p["w_out"][l].astype(BF16)
    d_ssd, d_att = lay["d_ssd"], lay["d_att"]
    mix = _outproj(ssd, att, mls, w_out[:d_ssd], w_out[d_ssd:d_ssd + d_att], w_out[d_ssd + d_att:])

    ne = p["w_router"].shape[2]
    w_router = jnp.zeros((d, LANE), F32).at[:, :ne].set(p["w_router"][l])
    b_router = _pad_row(p["b_router"][l], LANE)
    x1, tok, wts, ids = _resid_ln_route(xall, mix, mod, 2, 3, 4, p["ln1_g"][l][None, :], p["ln1_b"][l][None, :],
                                        w_router, b_router, ne, n_ctx, alpha)

    plan = _route_plan(ids, ne, MOE_ROW_TILE)
    y_sorted = _moe_experts(l, tok, plan, p["w_gu_bf16"], p["b_gu"][:, :, None, :], p["w_dn_bf16"],
                            p["b_dn"][:, :, None, :])
    x2 = _combine_ln(x1, y_sorted, plan[0], wts, mod, 5, p["ln2_g"][l][None, :], p["ln2_b"][l][None, :],
                     n_ctx, alpha)
    aux = dict(proj=proj, conv_out=conv_out, ssd=ssd, mls=mls, att=att, mix=mix, x1=x1, tok=tok, wts=wts,
               ids=ids, y_sorted=y_sorted, pos=plan[0])
    return x2, aux


def kernel(x, c, ctx, c_ctx, w_ada, b_ada, w_in, ssd_conv_w, ssd_conv_b, ssd_A_log, ssd_dt_bias, ssd_D,
           ssd_norm_g, att_q_norm_g, att_k_norm_g, mls_conv_w, mls_conv_b, mls_gate_b, mls_norm_g, w_out,
           ln1_g, ln1_b, w_router, b_router, w_gu, b_gu, w_dn, b_dn, ln2_g, ln2_b):
    p = dict(w_in=w_in, ssd_conv_w=ssd_conv_w, ssd_conv_b=ssd_conv_b, ssd_A_log=ssd_A_log,
             ssd_dt_bias=ssd_dt_bias, ssd_D=ssd_D, ssd_norm_g=ssd_norm_g, att_q_norm_g=att_q_norm_g,
             att_k_norm_g=att_k_norm_g, mls_conv_w=mls_conv_w, mls_conv_b=mls_conv_b, mls_gate_b=mls_gate_b,
             mls_norm_g=mls_norm_g, w_out=w_out, ln1_g=ln1_g, ln1_b=ln1_b, w_router=w_router,
             b_router=b_router, w_gu=w_gu, b_gu=b_gu, w_dn=w_dn, b_dn=b_dn, ln2_g=ln2_g, ln2_b=ln2_b)
    p["w_gu_bf16"] = w_gu.astype(BF16)
    p["w_dn_bf16"] = w_dn.astype(BF16)
    batch, seq, d = x.shape
    assert batch == 1
    n_ctx = ctx.shape[1]
    depth = w_ada.shape[0]
    alpha = (2 * depth) ** 0.25
    lay = _layout(d)
    tables = _rope_tables(n_ctx, seq)
    cvec = jnp.zeros((8, d), F32).at[0].set(c[0]).at[1].set(c_ctx)
    mods = _ada(cvec, w_ada, b_ada)
    xall = jnp.concatenate([ctx[0], x[0]], axis=0)
    for l in range(depth):
        xall, _ = _layer(l, xall, mods[l], lay, tables, n_ctx, alpha, p)
    return xall[n_ctx:][None]
```

```python
import functools

import numpy as np
import jax
import jax.numpy as jnp
from jax import lax
from jax.experimental import pallas as pl
from jax.experimental.pallas import tpu as pltpu

F32 = jnp.float32
BF16 = jnp.bfloat16

GRID_W = 64
CHUNK = 128
CONV_W = 5
SSD_HEADDIM = 64
SSD_GROUPS = 4
SSD_STATE = 128
HEAD_DIM = 128
ATT_KV_HEADS = 2
ROPE_THETA = 10000.0
MLS_V_DIM = 256
MLS_QK_DIM = 128
TOP_K = 4
SWIGLU_LIMIT = 7.0
SWIGLU_ALPHA = 1.702

LANE = 128
SMALL_W = LANE
NEG_BIG = -1e30
LOG2_E = 1.4426950408889634
VMEM_LIMIT = 56 * 1024 * 1024


def _cparams(sem, vmem=VMEM_LIMIT):
    return pltpu.CompilerParams(dimension_semantics=sem, vmem_limit_bytes=vmem)


def _pick(n, cands):
    for c in cands:
        if n % c == 0:
            return c
    raise ValueError(f"no tile in {cands} divides {n}")


def _sigmoid(x):
    return 1.0 / (1.0 + jnp.exp(-x))


def _softplus(x):
    return jnp.maximum(x, 0.0) + jnp.log(1.0 + jnp.exp(-jnp.abs(x)))


def _split_bf16(a, n):
    parts = []
    r = a
    for _ in range(n):
        p = r.astype(BF16)
        parts.append(p)
        r = r - p.astype(F32)
    return parts


def _dot01_left(m01, a, n=3):
    out = None
    for p in _split_bf16(a, n):
        t = jnp.dot(m01, p, preferred_element_type=F32)
        out = t if out is None else out + t
    return out


def _dot01_right(a, m01, n=2):
    out = None
    for p in _split_bf16(a, n):
        t = jnp.dot(p, m01, preferred_element_type=F32)
        out = t if out is None else out + t
    return out


def _ada_kernel(c_ref, w_ref, b_ref, o_ref):
    c = c_ref[...]
    s = (c * _sigmoid(c)).astype(BF16)
    o_ref[...] = jnp.dot(s, w_ref[...].astype(BF16), preferred_element_type=F32) + b_ref[...]


def _ada(cvec, w_ada, b_ada):
    depth, d, n = w_ada.shape
    tn = _pick(n, (512, 256, 128))
    return pl.pallas_call(
        _ada_kernel,
        grid=(depth, n // tn),
        in_specs=[
            pl.BlockSpec((8, d), lambda l, j: (0, 0)),
            pl.BlockSpec((None, d, tn), lambda l, j: (l, 0, j)),
            pl.BlockSpec((None, 1, tn), lambda l, j: (l, 0, j)),
        ],
        out_specs=pl.BlockSpec((None, 8, tn), lambda l, j: (l, 0, j)),
        out_shape=jax.ShapeDtypeStruct((depth, 8, n), F32),
        compiler_params=_cparams(("arbitrary", "arbitrary")),
        name="ada_mod",
    )(cvec, w_ada, b_ada.reshape(depth, 1, n))


def _inproj_kernel(x_ref, sh_ref, sc_ref, w_ref, o_ref, h_scr, *, n_ctx, tm):
    i = pl.program_id(0)
    j = pl.program_id(1)

    @pl.when(j == 0)
    def _():
        rows = i * tm + lax.broadcasted_iota(jnp.int32, (tm, 1), 0)
        is_ctx = rows < n_ctx
        sc = jnp.where(is_ctx, sc_ref[1:2, :], sc_ref[0:1, :])
        sh = jnp.where(is_ctx, sh_ref[1:2, :], sh_ref[0:1, :])
        h_scr[...] = (x_ref[...] * (1.0 + sc) + sh).astype(BF16)

    o_ref[...] = jnp.dot(h_scr[...], w_ref[...], preferred_element_type=F32)


def _inproj(xall, mod, w, n_ctx):
    t, d = xall.shape
    n = w.shape[1]
    tm = _pick(t, (768, 512, 256, 128))
    tn = _pick(n, (768, 512, 256, 128))
    return pl.pallas_call(
        functools.partial(_inproj_kernel, n_ctx=n_ctx, tm=tm),
        grid=(t // tm, n // tn),
        in_specs=[
            pl.BlockSpec((tm, d), lambda i, j: (i, 0), pipeline_mode=pl.Buffered(1)),
            pl.BlockSpec((8, d), lambda i, j: (0, 0)),
            pl.BlockSpec((8, d), lambda i, j: (0, 1)),
            pl.BlockSpec((d, tn), lambda i, j: (0, j)),
        ],
        out_specs=pl.BlockSpec((tm, tn), lambda i, j: (i, j)),
        out_shape=jax.ShapeDtypeStruct((t, n), F32),
        scratch_shapes=[pltpu.VMEM((tm, d), BF16)],
        compiler_params=_cparams(("arbitrary", "arbitrary")),
        name="in_proj",
    )(xall, mod, mod, w)


def _conv_kernel(x_ref, w_ref, b_ref, o_ref, *, n_ctx, tt):
    t = x_ref.shape[0]
    w = w_ref[...]
    bias = b_ref[...]
    halo = 8
    pad = CONV_W // 2

    def body(i, carry):
        t0 = pl.multiple_of(i * tt, tt)
        cur = x_ref[pl.ds(t0, tt), :]
        p0 = pl.multiple_of(jnp.maximum(t0 - halo, 0), halo)
        n0 = pl.multiple_of(jnp.minimum(t0 + tt, t - halo), halo)
        prev = x_ref[pl.ds(p0, halo), :]
        nxt = x_ref[pl.ds(n0, halo), :]
        seg_start = jnp.logical_or(t0 == 0, t0 == n_ctx)
        seg_end = jnp.logical_or(t0 + tt == n_ctx, t0 + tt == t)
        prev = jnp.where(seg_start, 0.0, prev)
        nxt = jnp.where(seg_end, 0.0, nxt)
        win = jnp.concatenate([prev, cur, nxt], axis=0)
        acc = bias + w[0:1, :] * win[halo - pad:halo - pad + tt, :]
        for k in range(1, CONV_W):
            acc = acc + w[k:k + 1, :] * win[halo - pad + k:halo - pad + k + tt, :]
        o_ref[pl.ds(t0, tt), :] = acc * _sigmoid(acc)
        return carry

    lax.fori_loop(0, t // tt, body, 0)


def _conv(proj, w8, b, first, second, n_ctx):
    t = proj.shape[0]
    tt = _pick(n_ctx, (256, 128))
    assert t % tt == 0 and all(v % LANE == 0 for v in first + second)
    n_first = first[1] // LANE
    cb_first, cb_second = first[0] // LANE, second[0] // LANE
    width = first[1] + second[1]
    return pl.pallas_call(
        functools.partial(_conv_kernel, n_ctx=n_ctx, tt=tt),
        grid=(width // LANE,),
        in_specs=[
            pl.BlockSpec((t, LANE), lambda c: (0, jnp.where(c < n_first, cb_first + c, cb_second + c - n_first))),
            pl.BlockSpec((8, LANE), lambda c: (0, c)),
            pl.BlockSpec((1, LANE), lambda c: (0, c)),
        ],
        out_specs=pl.BlockSpec((t, LANE), lambda c: (0, c)),
        out_shape=jax.ShapeDtypeStruct((t, width), F32),
        compiler_params=_cparams(("arbitrary",)),
        name="dwconv_silu",
    )(proj, w8, b)


def _chunk_order(i, n_chunks, n_ctx_chunks, reverse):
    if not reverse:
        return i
    return jnp.where(i < n_ctx_chunks, n_ctx_chunks - 1 - i, n_chunks - 1 - (i - n_ctx_chunks))


def _tri_mask(reverse):
    row = lax.broadcasted_iota(jnp.int32, (CHUNK, CHUNK), 0)
    col = lax.broadcasted_iota(jnp.int32, (CHUNK, CHUNK), 1)
    return (col >= row) if reverse else (col <= row)


def _ssd_kernel(*refs, reverse, finish, heads, hpg):
    if finish:
        (xs_ref, b_ref, c_ref, sm_ref, bias_ref, alog_ref, exp_ref,
         z_ref, yprev_ref, dskip_ref, g_ref, o_ref, h_scr) = refs
    else:
        (xs_ref, b_ref, c_ref, sm_ref, bias_ref, alog_ref, exp_ref, o_ref, h_scr) = refs
    gw = hpg * SSD_HEADDIM

    @pl.when(pl.program_id(0) == 0)
    def _():
        h_scr[...] = jnp.zeros_like(h_scr)

    mask = _tri_mask(reverse)
    tri = mask.astype(BF16)
    p = sm_ref[...] + bias_ref[...]
    dt = _softplus(p)
    dta = dt * (-jnp.exp(alog_ref[...]))
    cum = _dot01_left(tri, dta)
    cum_t = cum.T
    tot = cum[0:1, :] if reverse else cum[CHUNK - 1:CHUNK, :]
    e01 = exp_ref[...]
    dt_x = _dot01_right(dt, e01)
    in_x = _dot01_right(jnp.exp(cum), e01)
    tail_x = _dot01_right(jnp.exp(tot - cum), e01)
    tot_x = _dot01_right(jnp.broadcast_to(jnp.exp(tot), (8, SMALL_W)), e01)[0:1, :]

    xs = xs_ref[...]
    xdt = xs * dt_x
    xdt_b = xdt.astype(BF16)
    xtail_b = (xdt * tail_x).astype(BF16)
    bm = b_ref[...]
    cm = c_ref[...]
    c0 = heads if reverse else 0
    ys = []
    for g in range(SSD_GROUPS):
        bg = bm[:, g * SSD_STATE:(g + 1) * SSD_STATE]
        cg = cm[:, g * SSD_STATE:(g + 1) * SSD_STATE].astype(BF16)
        cb = lax.dot_general(cg, bg.astype(BF16), (((1,), (1,)), ((), ())),
                             preferred_element_type=F32)
        h_t = h_scr[g]
        y_g = jnp.dot(cg, h_t.astype(BF16), preferred_element_type=F32) * in_x[:, g * gw:(g + 1) * gw]
        parts = []
        for r in range(hpg):
            h = g * hpg + r
            c = c0 + h
            seg = cum[:, c:c + 1] - cum_t[c:c + 1, :]
            decay = jnp.exp(jnp.where(mask, seg, NEG_BIG))
            m = (cb * decay).astype(BF16)
            parts.append(jnp.dot(m, xdt_b[:, h * SSD_HEADDIM:(h + 1) * SSD_HEADDIM],
                                 preferred_element_type=F32))
        ys.append(y_g + jnp.concatenate(parts, axis=1))
        h_scr[g] = h_t * tot_x[:, g * gw:(g + 1) * gw] + jnp.dot(
            bg.T.astype(BF16), xtail_b[:, g * gw:(g + 1) * gw], preferred_element_type=F32)
    y = jnp.concatenate(ys, axis=1)

    if finish:
        z = z_ref[...]
        yt = (yprev_ref[...] + y + dskip_ref[...] * xs) * (z * _sigmoid(z))
        ms = jnp.mean(yt * yt, axis=-1, keepdims=True)
        o_ref[...] = (yt * lax.rsqrt(ms + 1e-6) * g_ref[...]).astype(o_ref.dtype)
    else:
        o_ref[...] = y


def _ssd_pass(conv_out, proj, lay, bias1, alog_row, e01, n_ctx, reverse, extra=None):
    t = proj.shape[0]
    nch, ncc = t // CHUNK, n_ctx // CHUNK
    d_ssd = lay["d_ssd"]
    heads = d_ssd // SSD_HEADDIM
    gs = SSD_GROUPS * SSD_STATE
    order = functools.partial(_chunk_order, n_chunks=nch, n_ctx_chunks=ncc, reverse=reverse)
    const = lambda i: (0, 0)
    in_specs = [
        pl.BlockSpec((CHUNK, d_ssd), lambda i: (order(i), lay["co_xs"] // d_ssd)),
        pl.BlockSpec((CHUNK, gs), lambda i: (order(i), lay["co_b"] // gs)),
        pl.BlockSpec((CHUNK, gs), lambda i: (order(i), lay["co_c"] // gs)),
        pl.BlockSpec((CHUNK, SMALL_W), lambda i: (order(i), lay["sm1"] // SMALL_W)),
        pl.BlockSpec((1, SMALL_W), const),
        pl.BlockSpec((1, SMALL_W), const),
        pl.BlockSpec((SMALL_W, d_ssd), const),
    ]
    args = [conv_out, conv_out, conv_out, proj, bias1, alog_row, e01]
    finish = extra is not None
    if finish:
        y_prev, dskip_x, norm_g = extra
        in_specs += [
            pl.BlockSpec((CHUNK, d_ssd), lambda i: (order(i), lay["z"] // d_ssd)),
            pl.BlockSpec((CHUNK, d_ssd), lambda i: (order(i), 0)),
            pl.BlockSpec((1, d_ssd), const),
            pl.BlockSpec((1, d_ssd), const),
        ]
        args += [proj, y_prev, dskip_x, norm_g]
    return pl.pallas_call(
        functools.partial(_ssd_kernel, reverse=reverse, finish=finish, heads=heads,
                          hpg=heads // SSD_GROUPS),
        grid=(nch,),
        in_specs=in_specs,
        out_specs=pl.BlockSpec((CHUNK, d_ssd), lambda i: (order(i), 0)),
        out_shape=jax.ShapeDtypeStruct((t, d_ssd), BF16 if finish else F32),
        scratch_shapes=[pltpu.VMEM((SSD_GROUPS, SSD_STATE, d_ssd // SSD_GROUPS), F32)],
        compiler_params=_cparams(("arbitrary",)),
        name="ssd_bwd_finish" if finish else "ssd_fwd",
    )(*args)


def _mls_kernel(*refs, reverse, finish, heads, n_io):
    q_ref, k_ref = refs[:2]
    v_refs = refs[2:2 + n_io]
    sm1_ref, sm2_ref, b1_ref, b2_ref = refs[2 + n_io:6 + n_io]
    if finish:
        og_refs = refs[6 + n_io:6 + 2 * n_io]
        hprev_ref, gain_ref, o_ref, s_scr, m_scr = refs[6 + 2 * n_io:]
    else:
        o_ref, s_scr, m_scr = refs[6 + n_io:]

    @pl.when(pl.program_id(0) == 0)
    def _():
        s_scr[...] = jnp.zeros_like(s_scr)
        m_scr[...] = jnp.zeros_like(m_scr)

    mask = _tri_mask(reverse)
    tri = mask.astype(BF16)
    lf = -_softplus(-(sm1_ref[...] + b1_ref[...]))
    li = sm2_ref[...] + b2_ref[...]
    b = _dot01_left(tri, lf)
    b_t = b.T
    li_t = li.T
    tot = b[0:1, :] if reverse else b[CHUNK - 1:CHUNK, :]
    m_row = m_scr[...]
    g_all = tot - b + li
    m_new = jnp.maximum(tot + m_row, jnp.max(g_all, axis=0, keepdims=True))
    wk_all = jnp.exp(g_all - m_new)
    decay_row = jnp.exp(tot + m_row - m_new)
    mprev_all = b + m_row
    m_scr[...] = m_new

    q = q_ref[...]
    k = k_ref[...] * (MLS_QK_DIM ** -0.5)
    v = jnp.concatenate([r[...] for r in v_refs], axis=1)
    ones_col = (lax.broadcasted_iota(jnp.int32, (CHUNK, LANE), 1) == 0).astype(F32)
    c0 = lay_fcol(heads, reverse)
    outs = []
    for h in range(heads):
        c = c0 + h
        qh = q[:, h * MLS_QK_DIM:(h + 1) * MLS_QK_DIM].astype(BF16)
        kh = k[:, h * MLS_QK_DIM:(h + 1) * MLS_QK_DIM]
        khb = kh.astype(BF16)
        vext = jnp.concatenate([v[:, h * MLS_V_DIM:(h + 1) * MLS_V_DIM], ones_col], axis=1)
        dmat = jnp.where(mask, b[:, c:c + 1] - b_t[c:c + 1, :] + li_t[c:c + 1, :], NEG_BIG)
        m_prev = mprev_all[:, c:c + 1]
        m_t = jnp.maximum(m_prev, jnp.max(dmat, axis=1, keepdims=True))
        qk = lax.dot_general(qh, khb, (((1,), (1,)), ((), ())), preferred_element_type=F32)
        w = jnp.exp(dmat - m_t) * qk
        s_prev = jnp.exp(m_prev - m_t)
        s_h = s_scr[h]
        numx = (jnp.dot(w.astype(BF16), vext.astype(BF16), preferred_element_type=F32)
                + s_prev * jnp.dot(qh, s_h.astype(BF16), preferred_element_type=F32))
        den = numx[:, MLS_V_DIM:MLS_V_DIM + 1]
        outs.append(numx[:, :MLS_V_DIM] / jnp.maximum(jnp.abs(den), jnp.exp(-m_t)))
        s_scr[h] = decay_row[:, c:c + 1] * s_h + jnp.dot(
            kh.T.astype(BF16), (wk_all[:, c:c + 1] * vext).astype(BF16), preferred_element_type=F32)

    if finish:
        hp = hprev_ref[...]
        og = jnp.concatenate([r[...] for r in og_refs], axis=1)
        gain = gain_ref[...]
        for h in range(heads):
            sl = slice(h * MLS_V_DIM, (h + 1) * MLS_V_DIM)
            hs = hp[:, sl] + outs[h]
            ms = jnp.mean(hs * hs, axis=-1, keepdims=True)
            o_ref[:, sl] = (_sigmoid(og[:, sl]) * (hs * lax.rsqrt(ms + 1e-6) * gain[:, sl])).astype(o_ref.dtype)
    else:
        for h in range(heads):
            o_ref[:, h * MLS_V_DIM:(h + 1) * MLS_V_DIM] = outs[h]


def lay_fcol(heads, reverse):
    return GATE_COL0 + (heads if reverse else 0)


def _mls_pass(conv_out, proj, lay, bias1, bias2, n_ctx, reverse, extra=None):
    t = proj.shape[0]
    nch, ncc = t // CHUNK, n_ctx // CHUNK
    d_mls = lay["d_mls"]
    heads = d_mls // MLS_V_DIM
    qw = heads * MLS_QK_DIM
    order = functools.partial(_chunk_order, n_chunks=nch, n_ctx_chunks=ncc, reverse=reverse)
    const = lambda i: (0, 0)
    n_io = d_mls // MLS_IO_BLOCK

    def io_blocks(col0):
        return [pl.BlockSpec((CHUNK, MLS_IO_BLOCK), functools.partial(lambda i, b: (order(i), b),
                                                                       b=col0 // MLS_IO_BLOCK + j))
                for j in range(n_io)]

    in_specs = [
        pl.BlockSpec((CHUNK, qw), lambda i: (order(i), lay["co_q"] // qw)),
        pl.BlockSpec((CHUNK, qw), lambda i: (order(i), lay["co_k"] // qw)),
        *io_blocks(lay["mv"]),
        pl.BlockSpec((CHUNK, SMALL_W), lambda i: (order(i), lay["sm1"] // SMALL_W)),
        pl.BlockSpec((CHUNK, SMALL_W), lambda i: (order(i), lay["sm2"] // SMALL_W)),
        pl.BlockSpec((1, SMALL_W), const),
        pl.BlockSpec((1, SMALL_W), const),
    ]
    args = [conv_out, conv_out] + [proj] * n_io + [proj, proj, bias1, bias2]
    finish = extra is not None
    if finish:
        h_prev, gain = extra
        in_specs += [
            *io_blocks(lay["mo"]),
            pl.BlockSpec((CHUNK, d_mls), lambda i: (order(i), 0)),
            pl.BlockSpec((1, d_mls), const),
        ]
        args += [proj] * n_io + [h_prev, gain]
    return pl.pallas_call(
        functools.partial(_mls_kernel, reverse=reverse, finish=finish, heads=heads, n_io=n_io),
        grid=(nch,),
        in_specs=in_specs,
        out_specs=pl.BlockSpec((CHUNK, d_mls), lambda i: (order(i), 0)),
        out_shape=jax.ShapeDtypeStruct((t, d_mls), BF16 if finish else F32),
        scratch_shapes=[pltpu.VMEM((heads, MLS_QK_DIM, MLS_V_DIM + LANE), F32),
                        pltpu.VMEM((1, SMALL_W), F32)],
        compiler_params=_cparams(("arbitrary",)),
        name="mlstm_bwd_finish" if finish else "mlstm_fwd",
    )(*args)


def _qk_prep_kernel(q_ref, k_ref, v_ref, cos_ref, sin_ref, gq_ref, gk_ref, qo_ref, ko_ref, vo_ref):
    cos = cos_ref[...]
    sin = sin_ref[...]

    even = lax.broadcasted_iota(jnp.int32, cos.shape, 1) % 2 == 0

    def norm_rope(xh, g, scale):
        ms = jnp.mean(xh * xh, axis=-1, keepdims=True)
        xn = xh * lax.rsqrt(ms + 1e-6) * g
        partner = jnp.where(even, pltpu.roll(xn, HEAD_DIM - 1, axis=1), pltpu.roll(xn, 1, axis=1))
        return (xn * cos + partner * sin) * scale

    q = q_ref[...]
    for h in range(q.shape[1] // HEAD_DIM):
        sl = slice(h * HEAD_DIM, (h + 1) * HEAD_DIM)
        qo_ref[:, sl] = norm_rope(q[:, sl], gq_ref[...], LOG2_E * HEAD_DIM ** -0.5).astype(qo_ref.dtype)
    k = k_ref[...]
    for h in range(k.shape[1] // HEAD_DIM):
        sl = slice(h * HEAD_DIM, (h + 1) * HEAD_DIM)
        ko_ref[:, sl] = norm_rope(k[:, sl], gk_ref[...], 1.0).astype(ko_ref.dtype)
    v = v_ref[...]
    ones = jnp.ones((v.shape[0], HEAD_DIM), vo_ref.dtype)
    for h in range(v.shape[1] // HEAD_DIM):
        vo_ref[:, 2 * h * HEAD_DIM:(2 * h + 1) * HEAD_DIM] = v[:, h * HEAD_DIM:(h + 1) * HEAD_DIM].astype(vo_ref.dtype)
        vo_ref[:, (2 * h + 1) * HEAD_DIM:(2 * h + 2) * HEAD_DIM] = ones


def _qk_prep(proj, lay, cos_t, sin_t, gq, gk):
    t = proj.shape[0]
    tm = _pick(t, (256, 128))
    d_att = lay["d_att"]
    kvw = ATT_KV_HEADS * HEAD_DIM
    const = lambda i: (0, 0)
    return pl.pallas_call(
        _qk_prep_kernel,
        grid=(t // tm,),
        in_specs=[
            pl.BlockSpec((tm, d_att), lambda i: (i, lay["aq"] // d_att)),
            pl.BlockSpec((tm, kvw), lambda i: (i, lay["ak"] // kvw)),
            pl.BlockSpec((tm, kvw), lambda i: (i, lay["av"] // kvw)),
            pl.BlockSpec((tm, HEAD_DIM), lambda i: (i, 0)),
            pl.BlockSpec((tm, HEAD_DIM), lambda i: (i, 0)),
            pl.BlockSpec((1, HEAD_DIM), const),
            pl.BlockSpec((1, HEAD_DIM), const),
        ],
        out_specs=[
            pl.BlockSpec((tm, d_att), lambda i: (i, 0)),
            pl.BlockSpec((tm, kvw), lambda i: (i, 0)),
            pl.BlockSpec((tm, 2 * kvw), lambda i: (i, 0)),
        ],
        out_shape=[jax.ShapeDtypeStruct((t, d_att), BF16),
                   jax.ShapeDtypeStruct((t, kvw), BF16),
                   jax.ShapeDtypeStruct((t, 2 * kvw), BF16)],
        compiler_params=_cparams(("arbitrary",)),
        name="qk_norm_rope",
    )(proj, proj, proj, cos_t, sin_t, gq, gk)


FLASH_ROW_BLOCK = 16
FLASH_KV_SPLIT = 256


def _flash_kernel(q_ref, k_ref, v_ref, o_ref, s_scr, p_scr, m_scr, a_scr, acc_scr, *, group, tq):
    j = pl.program_id(2)

    @pl.when(j == 0)
    def _():
        m_scr[...] = jnp.full_like(m_scr, NEG_BIG)
        acc_scr[...] = jnp.zeros_like(acc_scr)

    n_rows = group * tq
    n_split, _, tks = s_scr.shape
    rb = FLASH_ROW_BLOCK
    lane_tiles = [slice(t * LANE, (t + 1) * LANE) for t in range(tks // LANE)]
    for c in range(n_split):
        kc = k_ref[c * tks:(c + 1) * tks, :]
        for h in range(group):
            s_scr[c, h * tq:(h + 1) * tq, :] = lax.dot_general(
                q_ref[:, h * HEAD_DIM:(h + 1) * HEAD_DIM], kc, (((1,), (1,)), ((), ())),
                preferred_element_type=F32)
    for c in range(n_split):
        for b in range(n_rows // rb):
            rows = slice(b * rb, (b + 1) * rb)
            mx = s_scr[c, rows, lane_tiles[0]]
            for tile in lane_tiles[1:]:
                mx = jnp.maximum(mx, s_scr[c, rows, tile])
            m_old = m_scr[rows, :]
            m_new = jnp.maximum(m_old, jnp.broadcast_to(jnp.max(mx, axis=1, keepdims=True), (rb, LANE)))
            m_scr[rows, :] = m_new
            a_scr[rows, :] = jnp.exp2(m_old - m_new)
        for b in range(n_rows // rb):
            rows = slice(b * rb, (b + 1) * rb)
            m = m_scr[rows, :]
            for tile in lane_tiles:
                p_scr[c, rows, tile] = jnp.exp2(s_scr[c, rows, tile] - m).astype(BF16)
            alpha = a_scr[rows, :]
            acc_scr[rows, :] = acc_scr[rows, :] * jnp.concatenate([alpha, alpha], axis=1)
        acc_scr[...] += jnp.dot(p_scr[c], v_ref[c * tks:(c + 1) * tks, :], preferred_element_type=F32)

    @pl.when(j == pl.num_programs(2) - 1)
    def _():
        for h in range(group):
            rows = slice(h * tq, (h + 1) * tq)
            o_ref[:, h * HEAD_DIM:(h + 1) * HEAD_DIM] = (
                acc_scr[rows, :HEAD_DIM] / acc_scr[rows, HEAD_DIM:]).astype(o_ref.dtype)


def _flash(qn, kn, vb, n_k):
    n_q, d_att = qn.shape
    group = d_att // HEAD_DIM // ATT_KV_HEADS
    gw = group * HEAD_DIM
    tq = _pick(n_q, (512, 256, 128))
    tk = _pick(n_k, (768, 512, 256, 128))
    tks = min(tk, FLASH_KV_SPLIT)
    return pl.pallas_call(
        functools.partial(_flash_kernel, group=group, tq=tq),
        grid=(ATT_KV_HEADS, n_q // tq, n_k // tk),
        in_specs=[
            pl.BlockSpec((tq, gw), lambda g, i, j: (i, g)),
            pl.BlockSpec((tk, HEAD_DIM), lambda g, i, j: (j, g)),
            pl.BlockSpec((tk, 2 * HEAD_DIM), lambda g, i, j: (j, g)),
        ],
        out_specs=pl.BlockSpec((tq, gw), lambda g, i, j: (i, g)),
        out_shape=jax.ShapeDtypeStruct((n_q, d_att), BF16),
        scratch_shapes=[pltpu.VMEM((tk // tks, group * tq, tks), F32),
                        pltpu.VMEM((tk // tks, group * tq, tks), BF16),
                        pltpu.VMEM((group * tq, LANE), F32),
                        pltpu.VMEM((group * tq, LANE), F32),
                        pltpu.VMEM((group * tq, 2 * HEAD_DIM), F32)],
        compiler_params=_cparams(("arbitrary", "arbitrary", "arbitrary")),
        name="flash_gqa",
    )(qn, kn, vb)


def _outproj_kernel(a1_ref, a2_ref, a3_ref, w1_ref, w2_ref, w3_ref, o_ref):
    acc = jnp.dot(a1_ref[...], w1_ref[...], preferred_element_type=F32)
    acc = acc + jnp.dot(a2_ref[...], w2_ref[...], preferred_element_type=F32)
    acc = acc + jnp.dot(a3_ref[...], w3_ref[...], preferred_element_type=F32)
    o_ref[...] = acc


def _outproj(a1, a2, a3, w1, w2, w3):
    t = a1.shape[0]
    n = w1.shape[1]
    tm = _pick(t, (768, 512, 256, 128))
    tn = _pick(n, (512, 256, 128))
    lhs = lambda a: pl.BlockSpec((tm, a.shape[1]), lambda i, j: (i, 0))
    rhs = lambda w: pl.BlockSpec((w.shape[0], tn), lambda i, j: (0, j))
    return pl.pallas_call(
        _outproj_kernel,
        grid=(t // tm, n // tn),
        in_specs=[lhs(a1), lhs(a2), lhs(a3), rhs(w1), rhs(w2), rhs(w3)],
        out_specs=pl.BlockSpec((tm, tn), lambda i, j: (i, j)),
        out_shape=jax.ShapeDtypeStruct((t, n), F32),
        compiler_params=_cparams(("arbitrary", "arbitrary")),
        name="out_proj",
    )(a1, a2, a3, w1, w2, w3)


def _topk_route(logits, n_experts):
    lane = lax.broadcasted_iota(jnp.int32, logits.shape, 1)
    l = jnp.where(lane < n_experts, logits, NEG_BIG)
    tops = []
    for _ in range(TOP_K):
        m = jnp.max(l, axis=1, keepdims=True)
        idx = jnp.min(jnp.where(l == m, lane, LANE), axis=1, keepdims=True)
        tops.append((m, idx))
        l = jnp.where(lane == idx, NEG_BIG, l)
    es = [jnp.exp(m - tops[0][0]) for m, _ in tops]
    den = es[0]
    for e in es[1:]:
        den = den + e
    wts = jnp.zeros(logits.shape, F32)
    ids = jnp.zeros(logits.shape, jnp.int32)
    for k, ((m, idx), e) in enumerate(zip(tops, es)):
        wts = jnp.where(lane == k, e / den, wts)
        ids = jnp.where(lane == k, idx, ids)
    return wts, ids


def _ln_route_kernel(x_ref, y_ref, gate_ref, lng_ref, lnb_ref, sh_ref, sc_ref, wr_ref, br_ref,
                     xo_ref, tok_ref, wts_ref, ids_ref, *, n_ctx, tm, alpha, n_experts):
    rows = pl.program_id(0) * tm + lax.broadcasted_iota(jnp.int32, (tm, 1), 0)
    is_ctx = rows < n_ctx
    gate = jnp.where(is_ctx, gate_ref[1:2, :], gate_ref[0:1, :])
    u = alpha * x_ref[...] + gate * y_ref[...]
    mu = jnp.mean(u, axis=-1, keepdims=True)
    uc = u - mu
    var = jnp.mean(uc * uc, axis=-1, keepdims=True)
    xn = uc * lax.rsqrt(var + 1e-5) * lng_ref[...] + lnb_ref[...]
    xo_ref[...] = xn
    sc = jnp.where(is_ctx, sc_ref[1:2, :], sc_ref[0:1, :])
    sh = jnp.where(is_ctx, sh_ref[1:2, :], sh_ref[0:1, :])
    tok = xn * (1.0 + sc) + sh
    tok_ref[...] = _pack_bf16_pairs(tok)
    t_hi, t_lo = _split_bf16(tok, 2)
    w_hi, w_lo = _split_bf16(wr_ref[...], 2)
    logits = (jnp.dot(t_hi, w_hi, preferred_element_type=F32)
              + jnp.dot(t_hi, w_lo, preferred_element_type=F32)
              + jnp.dot(t_lo, w_hi, preferred_element_type=F32)) + br_ref[...]
    wts_ref[...], ids_ref[...] = _topk_route(logits, n_experts)


def _resid_ln_route(xall, y, mod, gate_blk, sh_blk, sc_blk, lng, lnb, w_router, b_router, n_experts, n_ctx, alpha):
    t, d = xall.shape
    tm = _pick(t, (256, 128))
    const = lambda i: (0, 0)
    row = pl.BlockSpec((tm, d), lambda i: (i, 0))
    lanes = pl.BlockSpec((tm, LANE), lambda i: (i, 0))
    mod_blk = lambda blk: pl.BlockSpec((8, d), lambda i: (0, blk))
    return pl.pallas_call(
        functools.partial(_ln_route_kernel, n_ctx=n_ctx, tm=tm, alpha=alpha, n_experts=n_experts),
        grid=(t // tm,),
        in_specs=[row, row, mod_blk(gate_blk), pl.BlockSpec((1, d), const), pl.BlockSpec((1, d), const),
                  mod_blk(sh_blk), mod_blk(sc_blk), pl.BlockSpec((d, LANE), const),
                  pl.BlockSpec((1, LANE), const)],
        out_specs=[row, pl.BlockSpec((tm, d // 2), lambda i: (i, 0)), lanes, lanes],
        out_shape=[jax.ShapeDtypeStruct((t, d), F32), jax.ShapeDtypeStruct((t, d // 2), jnp.uint32),
                   jax.ShapeDtypeStruct((t, LANE), F32), jax.ShapeDtypeStruct((t, LANE), jnp.int32)],
        compiler_params=_cparams(("arbitrary",)),
        name="resid_ln_route",
    )(xall, y, mod, lng, lnb, mod, mod, w_router, b_router)


MOE_ROW_TILE = 256
COMBINE_TILE = 128


def _route_plan(ids, n_experts, tm):
    t = ids.shape[0]
    n_rows = t * TOP_K
    flat = ids[:, :TOP_K].reshape(-1)
    onehot = (flat[:, None] == jnp.arange(n_experts, dtype=jnp.int32)[None, :]).astype(jnp.int32)
    csum = jnp.cumsum(onehot, axis=0)
    rank = jnp.sum((csum - onehot) * onehot, axis=1)
    cnt = csum[-1]
    ptiles = (cnt + tm - 1) // tm
    tile_end = jnp.cumsum(ptiles)
    row_start = (tile_end - ptiles) * tm
    pos = (jnp.sum(onehot * row_start[None, :], axis=1) + rank).astype(jnp.int32)
    n_tiles = (n_rows + n_experts * (tm - 1)) // tm
    row_token = jnp.zeros((n_tiles * tm,), jnp.int32).at[pos].set(
        jnp.arange(n_rows, dtype=jnp.int32) // TOP_K)
    tiles = jnp.arange(n_tiles, dtype=jnp.int32)
    tile_expert = jnp.minimum(jnp.sum((tiles[:, None] >= tile_end[None, :]).astype(jnp.int32), axis=1),
                              n_experts - 1).astype(jnp.int32)
    tile_valid = (tiles < tile_end[-1]).astype(jnp.int32)
    return pos, row_token, tile_expert, tile_valid, n_tiles


HI16 = 0xFFFF0000


def _pack_bf16_pairs(x):
    half = x.shape[1] // 2
    lo = pltpu.bitcast(x[:, :half].astype(BF16).astype(F32), jnp.uint32) >> 16
    hi = pltpu.bitcast(x[:, half:].astype(BF16).astype(F32), jnp.uint32) & jnp.uint32(HI16)
    return hi | lo


def _unpack_bf16_pairs(w):
    return pltpu.bitcast(w << 16, F32), pltpu.bitcast(w & jnp.uint32(HI16), F32)


def _moe_kernel(te_ref, tv_ref, rt_ref, tok_hbm, wgu_ref, bgu_ref, wdn_ref, bdn_ref, sel_ref, o_ref,
                buf, sem, *, tm, n_tiles):
    i = pl.program_id(0)

    def issue(tile, slot):
        def body(r, carry):
            row = rt_ref[tile * tm + r]
            pltpu.make_async_copy(tok_hbm.at[pl.ds(row, 1), :], buf.at[slot, pl.ds(r, 1), :],
                                  sem.at[slot]).start()
            return carry
        lax.fori_loop(0, tm, body, 0, unroll=8)

    @pl.when(jnp.logical_and(i == 0, tv_ref[0] > 0))
    def _():
        issue(0, 0)

    nxt = jnp.minimum(i + 1, n_tiles - 1)

    @pl.when(jnp.logical_and(i + 1 < n_tiles, tv_ref[nxt] > 0))
    def _():
        issue(nxt, nxt % 2)

    slot = i % 2

    @pl.when(tv_ref[i] > 0)
    def _():
        pltpu.make_async_copy(tok_hbm.at[pl.ds(0, tm), :], buf.at[slot], sem.at[slot]).wait()
        x_lo, x_hi = _unpack_bf16_pairs(buf[slot])
        half = x_lo.shape[1]
        gu = (jnp.dot(x_lo.astype(BF16), wgu_ref[:half, :], preferred_element_type=F32)
              + jnp.dot(x_hi.astype(BF16), wgu_ref[half:, :], preferred_element_type=F32)) + bgu_ref[...]
        g = jnp.minimum(gu, SWIGLU_LIMIT)
        a = g * _sigmoid(SWIGLU_ALPHA * g)
        lin1 = jnp.clip(gu, -SWIGLU_LIMIT, SWIGLU_LIMIT) + 1.0
        pair = (a * pltpu.roll(lin1, gu.shape[1] - 1, axis=1)).astype(BF16)
        act = jnp.dot(pair, sel_ref[...], preferred_element_type=F32).astype(BF16)
        o_ref[...] = _pack_bf16_pairs(jnp.dot(act, wdn_ref[...], preferred_element_type=F32) + bdn_ref[...])

    @pl.when(tv_ref[i] == 0)
    def _():
        o_ref[...] = jnp.zeros_like(o_ref)


def _moe_experts(l, tok, plan, wgu, bgu, wdn, bdn):
    _, row_token, tile_expert, tile_valid, n_tiles = plan
    tm = MOE_ROW_TILE
    half = tok.shape[1]
    d = 2 * half
    two_de, de = wgu.shape[3], wdn.shape[2]
    sel = jnp.asarray(np.arange(two_de)[:, None] == 2 * np.arange(de)[None, :], BF16)
    grid_spec = pltpu.PrefetchScalarGridSpec(
        num_scalar_prefetch=3,
        grid=(n_tiles,),
        in_specs=[
            pl.BlockSpec(memory_space=pl.ANY),
            pl.BlockSpec((None, None, d, two_de), lambda i, te, tv, rt: (l, te[i], 0, 0)),
            pl.BlockSpec((None, None, 1, two_de), lambda i, te, tv, rt: (l, te[i], 0, 0)),
            pl.BlockSpec((None, None, de, d), lambda i, te, tv, rt: (l, te[i], 0, 0)),
            pl.BlockSpec((None, None, 1, d), lambda i, te, tv, rt: (l, te[i], 0, 0)),
            pl.BlockSpec((two_de, de), lambda i, te, tv, rt: (0, 0)),
        ],
        out_specs=pl.BlockSpec((tm, half), lambda i, te, tv, rt: (i, 0)),
        scratch_shapes=[pltpu.VMEM((2, tm, half), jnp.uint32), pltpu.SemaphoreType.DMA((2,))],
    )
    return pl.pallas_call(
        functools.partial(_moe_kernel, tm=tm, n_tiles=n_tiles),
        grid_spec=grid_spec,
        out_shape=jax.ShapeDtypeStruct((n_tiles * tm, half), jnp.uint32),
        compiler_params=_cparams(("arbitrary",)),
        name="moe_experts",
    )(tile_expert, tile_valid, row_token, tok, wgu, bgu, wdn, bdn, sel)


def _combine_ln_kernel(pos_ref, y_hbm, x_ref, wts_ref, gate_ref, lng_ref, lnb_ref, xo_ref, buf, sem,
                       *, n_ctx, tm, alpha, n_steps):
    i = pl.program_id(0)

    def issue(tile, slot):
        def body(r, carry):
            for k in range(TOP_K):
                row = pos_ref[(tile * tm + r) * TOP_K + k]
                pltpu.make_async_copy(y_hbm.at[pl.ds(row, 1), :], buf.at[slot, k, pl.ds(r, 1), :],
                                      sem.at[slot]).start()
            return carry
        lax.fori_loop(0, tm, body, 0, unroll=4)

    @pl.when(i == 0)
    def _():
        issue(0, 0)

    @pl.when(i + 1 < n_steps)
    def _():
        issue(i + 1, (i + 1) % 2)

    slot = i % 2
    for k in range(TOP_K):
        pltpu.make_async_copy(y_hbm.at[pl.ds(0, tm), :], buf.at[slot, k], sem.at[slot]).wait()
    wts = wts_ref[...]
    f_lo, f_hi = _unpack_bf16_pairs(buf[slot, 0])
    f_lo, f_hi = wts[:, 0:1] * f_lo, wts[:, 0:1] * f_hi
    for k in range(1, TOP_K):
        y_lo, y_hi = _unpack_bf16_pairs(buf[slot, k])
        f_lo = f_lo + wts[:, k:k + 1] * y_lo
        f_hi = f_hi + wts[:, k:k + 1] * y_hi
    f = jnp.concatenate([f_lo, f_hi], axis=1)
    rows = i * tm + lax.broadcasted_iota(jnp.int32, (tm, 1), 0)
    gate = jnp.where(rows < n_ctx, gate_ref[1:2, :], gate_ref[0:1, :])
    u = alpha * x_ref[...] + gate * f
    mu = jnp.mean(u, axis=-1, keepdims=True)
    uc = u - mu
    var = jnp.mean(uc * uc, axis=-1, keepdims=True)
    xo_ref[...] = uc * lax.rsqrt(var + 1e-5) * lng_ref[...] + lnb_ref[...]


def _combine_ln(x1, y_sorted, pos, wts, mod, gate_blk, lng, lnb, n_ctx, alpha):
    t, d = x1.shape
    tm = COMBINE_TILE
    n_steps = t // tm
    const = lambda i, p: (0, 0)
    grid_spec = pltpu.PrefetchScalarGridSpec(
        num_scalar_prefetch=1,
        grid=(n_steps,),
        in_specs=[
            pl.BlockSpec(memory_space=pl.ANY),
            pl.BlockSpec((tm, d), lambda i, p: (i, 0)),
            pl.BlockSpec((tm, LANE), lambda i, p: (i, 0)),
            pl.BlockSpec((8, d), lambda i, p: (0, gate_blk)),
            pl.BlockSpec((1, d), const),
            pl.BlockSpec((1, d), const),
        ],
        out_specs=pl.BlockSpec((tm, d), lambda i, p: (i, 0)),
        scratch_shapes=[pltpu.VMEM((2, TOP_K, tm, d // 2), jnp.uint32), pltpu.SemaphoreType.DMA((2,))],
    )
    return pl.pallas_call(
        functools.partial(_combine_ln_kernel, n_ctx=n_ctx, tm=tm, alpha=alpha, n_steps=n_steps),
        grid_spec=grid_spec,
        out_shape=jax.ShapeDtypeStruct((t, d), F32),
        compiler_params=_cparams(("arbitrary",)),
        name="moe_combine_ln",
    )(pos, y_sorted, x1, wts, mod, lng, lnb)


GATE_COL0 = 48
MLS_IO_BLOCK = 512


def _layout(d_model):
    d_ssd = 3 * d_model // 8
    d_att = d_model // 4
    d_mls = d_model - d_ssd - d_att
    ssd_heads = d_ssd // SSD_HEADDIM
    mls_heads = d_mls // MLS_V_DIM
    gs = SSD_GROUPS * SSD_STATE
    kvw = ATT_KV_HEADS * HEAD_DIM
    qkw = mls_heads * MLS_QK_DIM
    assert 2 * ssd_heads == GATE_COL0 and GATE_COL0 + 2 * mls_heads <= SMALL_W
    sizes = (d_ssd, d_ssd + 2 * gs, 2 * ssd_heads, d_att, kvw, kvw, 2 * qkw, d_mls, d_mls, 4 * mls_heads)
    o = np.concatenate([[0], np.cumsum(sizes)])
    src = dict(z=o[0], xbc=o[1], dt=o[2], aq=o[3], ak=o[4], av=o[5], mqk=o[6], mv=o[7], mo=o[8], mg=o[9])
    lay = dict(d_ssd=d_ssd, d_att=d_att, d_mls=d_mls, ssd_heads=ssd_heads, mls_heads=mls_heads,
               src={k: int(v) for k, v in src.items()}, gs=gs, kvw=kvw, qkw=qkw)
    cols = []

    def put(name, idx):
        lay[name] = len(cols)
        cols.extend(int(v) for v in idx)

    put("z", src["z"] + np.arange(d_ssd))
    put("xbc", src["xbc"] + np.arange(d_ssd + 2 * gs))
    put("aq", src["aq"] + np.arange(d_att))
    put("ak", src["ak"] + np.arange(kvw))
    put("av", src["av"] + np.arange(kvw))
    put("mqk", src["mqk"] + np.arange(2 * qkw))
    put("mv", src["mv"] + np.arange(d_mls))
    put("mo", src["mo"] + np.arange(d_mls))
    sm1 = -np.ones(SMALL_W, np.int64)
    sm1[:2 * ssd_heads] = src["dt"] + np.arange(2 * ssd_heads)
    sm2 = -np.ones(SMALL_W, np.int64)
    for direction in range(2):
        c = GATE_COL0 + direction * mls_heads
        sm1[c:c + mls_heads] = src["mg"] + (2 * direction + 1) * mls_heads + np.arange(mls_heads)
        sm2[c:c + mls_heads] = src["mg"] + (2 * direction) * mls_heads + np.arange(mls_heads)
    put("sm1", sm1)
    put("sm2", sm2)
    n_pad = -len(cols) % 512
    cols.extend([-1] * n_pad)
    lay["cols"] = np.asarray(cols, np.int64)
    lay["co_q"], lay["co_k"], lay["co_xs"] = 0, qkw, 2 * qkw
    lay["co_b"], lay["co_c"] = 2 * qkw + d_ssd, 2 * qkw + d_ssd + gs
    for name, width in (("z", d_ssd), ("mv", MLS_IO_BLOCK), ("mo", MLS_IO_BLOCK), ("aq", d_att), ("ak", kvw),
                        ("av", kvw), ("sm1", SMALL_W), ("sm2", SMALL_W), ("mqk", LANE), ("xbc", LANE)):
        assert lay[name] % width == 0, name
    assert d_mls % MLS_IO_BLOCK == 0
    assert lay["co_k"] % qkw == 0 and lay["co_xs"] % d_ssd == 0 and lay["co_b"] % gs == 0 and lay["co_c"] % gs == 0
    return lay


def _relayout_w_in(w, lay):
    d = w.shape[0]
    s = lay["src"]
    sh, mh = lay["ssd_heads"], lay["mls_heads"]

    def seg(name, width):
        return w[:, s[name]:s[name] + width]

    zeros = lambda n: jnp.zeros((d, n), w.dtype)
    mg = seg("mg", 4 * mh).reshape(d, 4, mh)
    tail = SMALL_W - GATE_COL0 - 2 * mh
    parts = [w[:, s["z"]:s["dt"]], w[:, s["aq"]:s["mg"]],
             seg("dt", 2 * sh), mg[:, 1], mg[:, 3], zeros(tail),
             zeros(GATE_COL0), mg[:, 0], mg[:, 2], zeros(tail)]
    parts.append(zeros(len(lay["cols"]) - sum(a.shape[1] for a in parts)))
    return jnp.concatenate(parts, axis=1).astype(BF16)


def _rope_tables(n_ctx, seq):
    rows = seq // GRID_W
    row = jnp.repeat(jnp.arange(rows), GRID_W).astype(F32)
    col = jnp.tile(jnp.arange(GRID_W), rows).astype(F32)
    n_freq = HEAD_DIM // 4
    inv = ROPE_THETA ** (-jnp.arange(n_freq, dtype=F32) / n_freq)
    ang = jnp.concatenate([row[:, None] * inv, col[:, None] * inv], axis=-1)
    cos, sin = jnp.cos(ang), jnp.sin(ang)
    cos_pairs = jnp.repeat(cos, 2, axis=-1)
    sin_pairs = jnp.stack([-sin, sin], axis=-1).reshape(seq, HEAD_DIM)
    cos_t = jnp.concatenate([jnp.ones((n_ctx, HEAD_DIM), F32), cos_pairs], axis=0)
    sin_t = jnp.concatenate([jnp.zeros((n_ctx, HEAD_DIM), F32), sin_pairs], axis=0)
    return cos_t, sin_t


def _pad_row(v, width=SMALL_W, at=0):
    out = jnp.zeros((1, width), F32)
    return out.at[0, at:at + v.shape[0]].set(v.astype(F32))


def _layer(l, xall, mod, lay, tables, n_ctx, alpha, p):
    d = xall.shape[1]
    proj = _inproj(xall, mod, _relayout_w_in(p["w_in"][l], lay), n_ctx)

    conv_w = jnp.concatenate([p["mls_conv_w"][l], p["ssd_conv_w"][l]], axis=1)
    conv_w8 = jnp.concatenate([conv_w, jnp.zeros((8 - CONV_W, conv_w.shape[1]), F32)], axis=0)
    conv_b = jnp.concatenate([p["mls_conv_b"][l], p["ssd_conv_b"][l]])[None, :]
    conv_out = _conv(proj, conv_w8, conv_b, (lay["mqk"], 2 * lay["qkw"]),
                     (lay["xbc"], lay["d_ssd"] + 2 * lay["gs"]), n_ctx)

    sh, mh = lay["ssd_heads"], lay["mls_heads"]
    gate_b = p["mls_gate_b"][l]
    bias1 = (_pad_row(p["ssd_dt_bias"][l].reshape(-1))
             + _pad_row(gate_b[1], at=GATE_COL0) + _pad_row(gate_b[3], at=GATE_COL0 + mh))
    bias2 = _pad_row(gate_b[0], at=GATE_COL0) + _pad_row(gate_b[2], at=GATE_COL0 + mh)
    alog_row = _pad_row(p["ssd_A_log"][l].reshape(-1))
    hcol = np.repeat(np.arange(sh), SSD_HEADDIM)
    e_f = jnp.asarray(np.arange(SMALL_W)[:, None] == hcol[None, :], BF16)
    e_b = jnp.asarray(np.arange(SMALL_W)[:, None] == (hcol + sh)[None, :], BF16)
    dskip_x = jnp.repeat(p["ssd_D"][l], SSD_HEADDIM)[None, :]
    y_f = _ssd_pass(conv_out, proj, lay, bias1, alog_row, e_f, n_ctx, False)
    ssd = _ssd_pass(conv_out, proj, lay, bias1, alog_row, e_b, n_ctx, True,
                    extra=(y_f, dskip_x, p["ssd_norm_g"][l][None, :]))

    h_f = _mls_pass(conv_out, proj, lay, bias1, bias2, n_ctx, False)
    mls = _mls_pass(conv_out, proj, lay, bias1, bias2, n_ctx, True, extra=(h_f, p["mls_norm_g"][l][None, :]))

    gq = p["att_q_norm_g"][l][None, :]
    gk = p["att_k_norm_g"][l][None, :]
    qn, kn, vb = _qk_prep(proj, lay, tables[0], tables[1], gq, gk)
    t = xall.shape[0]
    att_c = _flash(qn[:n_ctx], kn, vb, n_ctx)
    att_l = _flash(qn[n_ctx:], kn, vb, t)
    att = jnp.concatenate([att_c, att_l], axis=0)

    w_out = p["w_out"][l].astype(BF16)
    d_ssd, d_att = lay["d_ssd"], lay["d_att"]
    mix = _outproj(ssd, att, mls, w_out[:d_ssd], w_out[d_ssd:d_ssd + d_att], w_out[d_ssd + d_att:])

    ne = p["w_router"].shape[2]
    w_router = jnp.zeros((d, LANE), F32).at[:, :ne].set(p["w_router"][l])
    b_router = _pad_row(p["b_router"][l], LANE)
    x1, tok, wts, ids = _resid_ln_route(xall, mix, mod, 2, 3, 4, p["ln1_g"][l][None, :], p["ln1_b"][l][None, :],
                                        w_router, b_router, ne, n_ctx, alpha)

    plan = _route_plan(ids, ne, MOE_ROW_TILE)
    y_sorted = _moe_experts(l, tok, plan, p["w_gu_bf16"], p["b_gu"][:, :, None, :], p["w_dn_bf16"],
                            p["b_dn"][:, :, None, :])
    x2 = _combine_ln(x1, y_sorted, plan[0], wts, mod, 5, p["ln2_g"][l][None, :], p["ln2_b"][l][None, :],
                     n_ctx, alpha)
    aux = dict(proj=proj, conv_out=conv_out, ssd=ssd, mls=mls, att=att, mix=mix, x1=x1, tok=tok, wts=wts,
               ids=ids, y_sorted=y_sorted, pos=plan[0])
    return x2, aux


def kernel(x, c, ctx, c_ctx, w_ada, b_ada, w_in, ssd_conv_w, ssd_conv_b, ssd_A_log, ssd_dt_bias, ssd_D,
           ssd_norm_g, att_q_norm_g, att_k_norm_g, mls_conv_w, mls_conv_b, mls_gate_b, mls_norm_g, w_out,
           ln1_g, ln1_b, w_router, b_router, w_gu, b_gu, w_dn, b_dn, ln2_g, ln2_b):
    p = dict(w_in=w_in, ssd_conv_w=ssd_conv_w, ssd_conv_b=ssd_conv_b, ssd_A_log=ssd_A_log,
             ssd_dt_bias=ssd_dt_bias, ssd_D=ssd_D, ssd_norm_g=ssd_norm_g, att_q_norm_g=att_q_norm_g,
             att_k_norm_g=att_k_norm_g, mls_conv_w=mls_conv_w, mls_conv_b=mls_conv_b, mls_gate_b=mls_gate_b,
             mls_norm_g=mls_norm_g, w_out=w_out, ln1_g=ln1_g, ln1_b=ln1_b, w_router=w_router,
             b_router=b_router, w_gu=w_gu, b_gu=b_gu, w_dn=w_dn, b_dn=b_dn, ln2_g=ln2_g, ln2_b=ln2_b)
    p["w_gu_bf16"] = w_gu.astype(BF16)
    p["w_dn_bf16"] = w_dn.astype(BF16)
    batch, seq, d = x.shape
    assert batch == 1
    n_ctx = ctx.shape[1]
    depth = w_ada.shape[0]
    alpha = (2 * depth) ** 0.25
    lay = _layout(d)
    tables = _rope_tables(n_ctx, seq)
    cvec = jnp.zeros((8, d), F32).at[0].set(c[0]).at[1].set(c_ctx)
    mods = _ada(cvec, w_ada, b_ada)
    xall = jnp.concatenate([ctx[0], x[0]], axis=0)
    for l in range(depth):
        xall, _ = _layer(l, xall, mods[l], lay, tables, n_ctx, alpha, p)
    return xall[n_ctx:][None]
```

```python
import functools

import numpy as np
import jax
import jax.numpy as jnp
from jax import lax
from jax.experimental import pallas as pl
from jax.experimental.pallas import tpu as pltpu

F32 = jnp.float32
BF16 = jnp.bfloat16

GRID_W = 64
CHUNK = 128
CONV_W = 5
SSD_HEADDIM = 64
SSD_GROUPS = 4
SSD_STATE = 128
HEAD_DIM = 128
ATT_KV_HEADS = 2
ROPE_THETA = 10000.0
MLS_V_DIM = 256
MLS_QK_DIM = 128
TOP_K = 4
SWIGLU_LIMIT = 7.0
SWIGLU_ALPHA = 1.702

LANE = 128
SMALL_W = LANE
NEG_BIG = -1e30
LOG2_E = 1.4426950408889634
VMEM_LIMIT = 56 * 1024 * 1024


def _cparams(sem, vmem=VMEM_LIMIT):
    return pltpu.CompilerParams(dimension_semantics=sem, vmem_limit_bytes=vmem)


def _pick(n, cands):
    for c in cands:
        if n % c == 0:
            return c
    raise ValueError(f"no tile in {cands} divides {n}")


def _sigmoid(x):
    return 1.0 / (1.0 + jnp.exp(-x))


def _softplus(x):
    return jnp.maximum(x, 0.0) + jnp.log(1.0 + jnp.exp(-jnp.abs(x)))


def _split_bf16(a, n):
    parts = []
    r = a
    for _ in range(n):
        p = r.astype(BF16)
        parts.append(p)
        r = r - p.astype(F32)
    return parts


def _dot01_left(m01, a, n=3):
    out = None
    for p in _split_bf16(a, n):
        t = jnp.dot(m01, p, preferred_element_type=F32)
        out = t if out is None else out + t
    return out


def _dot01_right(a, m01, n=2):
    out = None
    for p in _split_bf16(a, n):
        t = jnp.dot(p, m01, preferred_element_type=F32)
        out = t if out is None else out + t
    return out


def _ada_kernel(c_ref, w_ref, b_ref, o_ref):
    c = c_ref[...]
    s = (c * _sigmoid(c)).astype(BF16)
    o_ref[...] = jnp.dot(s, w_ref[...].astype(BF16), preferred_element_type=F32) + b_ref[...]


def _ada(cvec, w_ada, b_ada):
    depth, d, n = w_ada.shape
    tn = _pick(n, (512, 256, 128))
    return pl.pallas_call(
        _ada_kernel,
        grid=(depth, n // tn),
        in_specs=[
            pl.BlockSpec((8, d), lambda l, j: (0, 0)),
            pl.BlockSpec((None, d, tn), lambda l, j: (l, 0, j)),
            pl.BlockSpec((None, 1, tn), lambda l, j: (l, 0, j)),
        ],
        out_specs=pl.BlockSpec((None, 8, tn), lambda l, j: (l, 0, j)),
        out_shape=jax.ShapeDtypeStruct((depth, 8, n), F32),
        compiler_params=_cparams(("arbitrary", "arbitrary")),
        name="ada_mod",
    )(cvec, w_ada, b_ada.reshape(depth, 1, n))


def _inproj_kernel(x_ref, sh_ref, sc_ref, w_ref, o_ref, h_scr, *, n_ctx, tm):
    i = pl.program_id(0)
    j = pl.program_id(1)

    @pl.when(j == 0)
    def _():
        rows = i * tm + lax.broadcasted_iota(jnp.int32, (tm, 1), 0)
        is_ctx = rows < n_ctx
        sc = jnp.where(is_ctx, sc_ref[1:2, :], sc_ref[0:1, :])
        sh = jnp.where(is_ctx, sh_ref[1:2, :], sh_ref[0:1, :])
        h_scr[...] = (x_ref[...] * (1.0 + sc) + sh).astype(BF16)

    o_ref[...] = jnp.dot(h_scr[...], w_ref[...], preferred_element_type=F32)


def _inproj(xall, mod, w, n_ctx):
    t, d = xall.shape
    n = w.shape[1]
    tm = _pick(t, (768, 512, 256, 128))
    tn = _pick(n, (768, 512, 256, 128))
    return pl.pallas_call(
        functools.partial(_inproj_kernel, n_ctx=n_ctx, tm=tm),
        grid=(t // tm, n // tn),
        in_specs=[
            pl.BlockSpec((tm, d), lambda i, j: (i, 0), pipeline_mode=pl.Buffered(1)),
            pl.BlockSpec((8, d), lambda i, j: (0, 0)),
            pl.BlockSpec((8, d), lambda i, j: (0, 1)),
            pl.BlockSpec((d, tn), lambda i, j: (0, j)),
        ],
        out_specs=pl.BlockSpec((tm, tn), lambda i, j: (i, j)),
        out_shape=jax.ShapeDtypeStruct((t, n), F32),
        scratch_shapes=[pltpu.VMEM((tm, d), BF16)],
        compiler_params=_cparams(("arbitrary", "arbitrary")),
        name="in_proj",
    )(xall, mod, mod, w)


def _conv_kernel(x_ref, w_ref, b_ref, o_ref, *, n_ctx, tt):
    t = x_ref.shape[0]
    w = w_ref[...]
    bias = b_ref[...]
    halo = 8
    pad = CONV_W // 2

    def body(i, carry):
        t0 = pl.multiple_of(i * tt, tt)
        cur = x_ref[pl.ds(t0, tt), :]
        p0 = pl.multiple_of(jnp.maximum(t0 - halo, 0), halo)
        n0 = pl.multiple_of(jnp.minimum(t0 + tt, t - halo), halo)
        prev = x_ref[pl.ds(p0, halo), :]
        nxt = x_ref[pl.ds(n0, halo), :]
        seg_start = jnp.logical_or(t0 == 0, t0 == n_ctx)
        seg_end = jnp.logical_or(t0 + tt == n_ctx, t0 + tt == t)
        prev = jnp.where(seg_start, 0.0, prev)
        nxt = jnp.where(seg_end, 0.0, nxt)
        win = jnp.concatenate([prev, cur, nxt], axis=0)
        acc = bias + w[0:1, :] * win[halo - pad:halo - pad + tt, :]
        for k in range(1, CONV_W):
            acc = acc + w[k:k + 1, :] * win[halo - pad + k:halo - pad + k + tt, :]
        o_ref[pl.ds(t0, tt), :] = acc * _sigmoid(acc)
        return carry

    lax.fori_loop(0, t // tt, body, 0)


def _conv(proj, w8, b, first, second, n_ctx):
    t = proj.shape[0]
    tt = _pick(n_ctx, (256, 128))
    assert t % tt == 0 and all(v % LANE == 0 for v in first + second)
    n_first = first[1] // LANE
    cb_first, cb_second = first[0] // LANE, second[0] // LANE
    width = first[1] + second[1]
    return pl.pallas_call(
        functools.partial(_conv_kernel, n_ctx=n_ctx, tt=tt),
        grid=(width // LANE,),
        in_specs=[
            pl.BlockSpec((t, LANE), lambda c: (0, jnp.where(c < n_first, cb_first + c, cb_second + c - n_first))),
            pl.BlockSpec((8, LANE), lambda c: (0, c)),
            pl.BlockSpec((1, LANE), lambda c: (0, c)),
        ],
        out_specs=pl.BlockSpec((t, LANE), lambda c: (0, c)),
        out_shape=jax.ShapeDtypeStruct((t, width), F32),
        compiler_params=_cparams(("arbitrary",)),
        name="dwconv_silu",
    )(proj, w8, b)


def _chunk_order(i, n_chunks, n_ctx_chunks, reverse):
    if not reverse:
        return i
    return jnp.where(i < n_ctx_chunks, n_ctx_chunks - 1 - i, n_chunks - 1 - (i - n_ctx_chunks))


def _tri_mask(reverse):
    row = lax.broadcasted_iota(jnp.int32, (CHUNK, CHUNK), 0)
    col = lax.broadcasted_iota(jnp.int32, (CHUNK, CHUNK), 1)
    return (col >= row) if reverse else (col <= row)


def _ssd_kernel(*refs, reverse, finish, heads, hpg):
    if finish:
        (xs_ref, b_ref, c_ref, sm_ref, bias_ref, alog_ref, exp_ref,
         z_ref, yprev_ref, dskip_ref, g_ref, o_ref, h_scr) = refs
    else:
        (xs_ref, b_ref, c_ref, sm_ref, bias_ref, alog_ref, exp_ref, o_ref, h_scr) = refs
    gw = hpg * SSD_HEADDIM

    @pl.when(pl.program_id(0) == 0)
    def _():
        h_scr[...] = jnp.zeros_like(h_scr)

    mask = _tri_mask(reverse)
    tri = mask.astype(BF16)
    p = sm_ref[...] + bias_ref[...]
    dt = _softplus(p)
    dta = dt * (-jnp.exp(alog_ref[...]))
    cum = _dot01_left(tri, dta)
    cum_t = cum.T
    tot = cum[0:1, :] if reverse else cum[CHUNK - 1:CHUNK, :]
    e01 = exp_ref[...]
    dt_x = _dot01_right(dt, e01)
    in_x = _dot01_right(jnp.exp(cum), e01)
    tail_x = _dot01_right(jnp.exp(tot - cum), e01)
    tot_x = _dot01_right(jnp.broadcast_to(jnp.exp(tot), (8, SMALL_W)), e01)[0:1, :]

    xs = xs_ref[...]
    xdt = xs * dt_x
    xdt_b = xdt.astype(BF16)
    xtail_b = (xdt * tail_x).astype(BF16)
    bm = b_ref[...]
    cm = c_ref[...]
    c0 = heads if reverse else 0
    ys = []
    for g in range(SSD_GROUPS):
        bg = bm[:, g * SSD_STATE:(g + 1) * SSD_STATE]
        cg = cm[:, g * SSD_STATE:(g + 1) * SSD_STATE].astype(BF16)
        cb = lax.dot_general(cg, bg.astype(BF16), (((1,), (1,)), ((), ())),
                             preferred_element_type=F32)
        h_t = h_scr[g]
        y_g = jnp.dot(cg, h_t.astype(BF16), preferred_element_type=F32) * in_x[:, g * gw:(g + 1) * gw]
        parts = []
        for r in range(hpg):
            h = g * hpg + r
            c = c0 + h
            seg = cum[:, c:c + 1] - cum_t[c:c + 1, :]
            decay = jnp.exp(jnp.where(mask, seg, NEG_BIG))
            m = (cb * decay).astype(BF16)
            parts.append(jnp.dot(m, xdt_b[:, h * SSD_HEADDIM:(h + 1) * SSD_HEADDIM],
                                 preferred_element_type=F32))
        ys.append(y_g + jnp.concatenate(parts, axis=1))
        h_scr[g] = h_t * tot_x[:, g * gw:(g + 1) * gw] + jnp.dot(
            bg.T.astype(BF16), xtail_b[:, g * gw:(g + 1) * gw], preferred_element_type=F32)
    y = jnp.concatenate(ys, axis=1)

    if finish:
        z = z_ref[...]
        yt = (yprev_ref[...] + y + dskip_ref[...] * xs) * (z * _sigmoid(z))
        ms = jnp.mean(yt * yt, axis=-1, keepdims=True)
        o_ref[...] = (yt * lax.rsqrt(ms + 1e-6) * g_ref[...]).astype(o_ref.dtype)
    else:
        o_ref[...] = y


def _ssd_pass(conv_out, proj, lay, bias1, alog_row, e01, n_ctx, reverse, extra=None):
    t = proj.shape[0]
    nch, ncc = t // CHUNK, n_ctx // CHUNK
    d_ssd = lay["d_ssd"]
    heads = d_ssd // SSD_HEADDIM
    gs = SSD_GROUPS * SSD_STATE
    order = functools.partial(_chunk_order, n_chunks=nch, n_ctx_chunks=ncc, reverse=reverse)
    const = lambda i: (0, 0)
    in_specs = [
        pl.BlockSpec((CHUNK, d_ssd), lambda i: (order(i), lay["co_xs"] // d_ssd)),
        pl.BlockSpec((CHUNK, gs), lambda i: (order(i), lay["co_b"] // gs)),
        pl.BlockSpec((CHUNK, gs), lambda i: (order(i), lay["co_c"] // gs)),
        pl.BlockSpec((CHUNK, SMALL_W), lambda i: (order(i), lay["sm1"] // SMALL_W)),
        pl.BlockSpec((1, SMALL_W), const),
        pl.BlockSpec((1, SMALL_W), const),
        pl.BlockSpec((SMALL_W, d_ssd), const),
    ]
    args = [conv_out, conv_out, conv_out, proj, bias1, alog_row, e01]
    finish = extra is not None
    if finish:
        y_prev, dskip_x, norm_g = extra
        in_specs += [
            pl.BlockSpec((CHUNK, d_ssd), lambda i: (order(i), lay["z"] // d_ssd)),
            pl.BlockSpec((CHUNK, d_ssd), lambda i: (order(i), 0)),
            pl.BlockSpec((1, d_ssd), const),
            pl.BlockSpec((1, d_ssd), const),
        ]
        args += [proj, y_prev, dskip_x, norm_g]
    return pl.pallas_call(
        functools.partial(_ssd_kernel, reverse=reverse, finish=finish, heads=heads,
                          hpg=heads // SSD_GROUPS),
        grid=(nch,),
        in_specs=in_specs,
        out_specs=pl.BlockSpec((CHUNK, d_ssd), lambda i: (order(i), 0)),
        out_shape=jax.ShapeDtypeStruct((t, d_ssd), BF16 if finish else F32),
        scratch_shapes=[pltpu.VMEM((SSD_GROUPS, SSD_STATE, d_ssd // SSD_GROUPS), F32)],
        compiler_params=_cparams(("arbitrary",)),
        name="ssd_bwd_finish" if finish else "ssd_fwd",
    )(*args)


def _mls_kernel(*refs, reverse, finish, heads, n_io):
    q_ref, k_ref = refs[:2]
    v_refs = refs[2:2 + n_io]
    sm1_ref, sm2_ref, b1_ref, b2_ref = refs[2 + n_io:6 + n_io]
    if finish:
        og_refs = refs[6 + n_io:6 + 2 * n_io]
        hprev_ref, gain_ref, o_ref, s_scr, m_scr = refs[6 + 2 * n_io:]
    else:
        o_ref, s_scr, m_scr = refs[6 + n_io:]

    @pl.when(pl.program_id(0) == 0)
    def _():
        s_scr[...] = jnp.zeros_like(s_scr)
        m_scr[...] = jnp.zeros_like(m_scr)

    mask = _tri_mask(reverse)
    tri = mask.astype(BF16)
    lf = -_softplus(-(sm1_ref[...] + b1_ref[...]))
    li = sm2_ref[...] + b2_ref[...]
    b = _dot01_left(tri, lf)
    b_t = b.T
    li_t = li.T
    tot = b[0:1, :] if reverse else b[CHUNK - 1:CHUNK, :]
    m_row = m_scr[...]
    g_all = tot - b + li
    m_new = jnp.maximum(tot + m_row, jnp.max(g_all, axis=0, keepdims=True))
    wk_all = jnp.exp(g_all - m_new)
    decay_row = jnp.exp(tot + m_row - m_new)
    mprev_all = b + m_row
    m_scr[...] = m_new

    q = q_ref[...]
    k = k_ref[...] * (MLS_QK_DIM ** -0.5)
    v = jnp.concatenate([r[...] for r in v_refs], axis=1)
    ones_col = (lax.broadcasted_iota(jnp.int32, (CHUNK, LANE), 1) == 0).astype(F32)
    c0 = lay_fcol(heads, reverse)
    outs = []
    for h in range(heads):
        c = c0 + h
        qh = q[:, h * MLS_QK_DIM:(h + 1) * MLS_QK_DIM].astype(BF16)
        kh = k[:, h * MLS_QK_DIM:(h + 1) * MLS_QK_DIM]
        khb = kh.astype(BF16)
        vext = jnp.concatenate([v[:, h * MLS_V_DIM:(h + 1) * MLS_V_DIM], ones_col], axis=1)
        dmat = jnp.where(mask, b[:, c:c + 1] - b_t[c:c + 1, :] + li_t[c:c + 1, :], NEG_BIG)
        m_prev = mprev_all[:, c:c + 1]
        m_t = jnp.maximum(m_prev, jnp.max(dmat, axis=1, keepdims=True))
        qk = lax.dot_general(qh, khb, (((1,), (1,)), ((), ())), preferred_element_type=F32)
        w = jnp.exp(dmat - m_t) * qk
        s_prev = jnp.exp(m_prev - m_t)
        s_h = s_scr[h]
        numx = (jnp.dot(w.astype(BF16), vext.astype(BF16), preferred_element_type=F32)
                + s_prev * jnp.dot(qh, s_h.astype(BF16), preferred_element_type=F32))
        den = numx[:, MLS_V_DIM:MLS_V_DIM + 1]
        outs.append(numx[:, :MLS_V_DIM] / jnp.maximum(jnp.abs(den), jnp.exp(-m_t)))
        s_scr[h] = decay_row[:, c:c + 1] * s_h + jnp.dot(
            kh.T.astype(BF16), (wk_all[:, c:c + 1] * vext).astype(BF16), preferred_element_type=F32)

    if finish:
        hp = hprev_ref[...]
        og = jnp.concatenate([r[...] for r in og_refs], axis=1)
        gain = gain_ref[...]
        for h in range(heads):
            sl = slice(h * MLS_V_DIM, (h + 1) * MLS_V_DIM)
            hs = hp[:, sl] + outs[h]
            ms = jnp.mean(hs * hs, axis=-1, keepdims=True)
            o_ref[:, sl] = (_sigmoid(og[:, sl]) * (hs * lax.rsqrt(ms + 1e-6) * gain[:, sl])).astype(o_ref.dtype)
    else:
        for h in range(heads):
            o_ref[:, h * MLS_V_DIM:(h + 1) * MLS_V_DIM] = outs[h]


def lay_fcol(heads, reverse):
    return GATE_COL0 + (heads if reverse else 0)


def _mls_pass(conv_out, proj, lay, bias1, bias2, n_ctx, reverse, extra=None):
    t = proj.shape[0]
    nch, ncc = t // CHUNK, n_ctx // CHUNK
    d_mls = lay["d_mls"]
    heads = d_mls // MLS_V_DIM
    qw = heads * MLS_QK_DIM
    order = functools.partial(_chunk_order, n_chunks=nch, n_ctx_chunks=ncc, reverse=reverse)
    const = lambda i: (0, 0)
    n_io = d_mls // MLS_IO_BLOCK

    def io_blocks(col0):
        return [pl.BlockSpec((CHUNK, MLS_IO_BLOCK), functools.partial(lambda i, b: (order(i), b),
                                                                       b=col0 // MLS_IO_BLOCK + j))
                for j in range(n_io)]

    in_specs = [
        pl.BlockSpec((CHUNK, qw), lambda i: (order(i), lay["co_q"] // qw)),
        pl.BlockSpec((CHUNK, qw), lambda i: (order(i), lay["co_k"] // qw)),
        *io_blocks(lay["mv"]),
        pl.BlockSpec((CHUNK, SMALL_W), lambda i: (order(i), lay["sm1"] // SMALL_W)),
        pl.BlockSpec((CHUNK, SMALL_W), lambda i: (order(i), lay["sm2"] // SMALL_W)),
        pl.BlockSpec((1, SMALL_W), const),
        pl.BlockSpec((1, SMALL_W), const),
    ]
    args = [conv_out, conv_out] + [proj] * n_io + [proj, proj, bias1, bias2]
    finish = extra is not None
    if finish:
        h_prev, gain = extra
        in_specs += [
            *io_blocks(lay["mo"]),
            pl.BlockSpec((CHUNK, d_mls), lambda i: (order(i), 0)),
            pl.BlockSpec((1, d_mls), const),
        ]
        args += [proj] * n_io + [h_prev, gain]
    return pl.pallas_call(
        functools.partial(_mls_kernel, reverse=reverse, finish=finish, heads=heads, n_io=n_io),
        grid=(nch,),
        in_specs=in_specs,
        out_specs=pl.BlockSpec((CHUNK, d_mls), lambda i: (order(i), 0)),
        out_shape=jax.ShapeDtypeStruct((t, d_mls), BF16 if finish else F32),
        scratch_shapes=[pltpu.VMEM((heads, MLS_QK_DIM, MLS_V_DIM + LANE), F32),
                        pltpu.VMEM((1, SMALL_W), F32)],
        compiler_params=_cparams(("arbitrary",)),
        name="mlstm_bwd_finish" if finish else "mlstm_fwd",
    )(*args)


def _qk_prep_kernel(q_ref, k_ref, v_ref, cos_ref, sin_ref, gq_ref, gk_ref, qo_ref, ko_ref, vo_ref):
    cos = cos_ref[...]
    sin = sin_ref[...]

    even = lax.broadcasted_iota(jnp.int32, cos.shape, 1) % 2 == 0

    def norm_rope(xh, g, scale):
        ms = jnp.mean(xh * xh, axis=-1, keepdims=True)
        xn = xh * lax.rsqrt(ms + 1e-6) * g
        partner = jnp.where(even, pltpu.roll(xn, HEAD_DIM - 1, axis=1), pltpu.roll(xn, 1, axis=1))
        return (xn * cos + partner * sin) * scale

    q = q_ref[...]
    for h in range(q.shape[1] // HEAD_DIM):
        sl = slice(h * HEAD_DIM, (h + 1) * HEAD_DIM)
        qo_ref[:, sl] = norm_rope(q[:, sl], gq_ref[...], LOG2_E * HEAD_DIM ** -0.5).astype(qo_ref.dtype)
    k = k_ref[...]
    for h in range(k.shape[1] // HEAD_DIM):
        sl = slice(h * HEAD_DIM, (h + 1) * HEAD_DIM)
        ko_ref[:, sl] = norm_rope(k[:, sl], gk_ref[...], 1.0).astype(ko_ref.dtype)
    v = v_ref[...]
    ones = jnp.ones((v.shape[0], HEAD_DIM), vo_ref.dtype)
    for h in range(v.shape[1] // HEAD_DIM):
        vo_ref[:, 2 * h * HEAD_DIM:(2 * h + 1) * HEAD_DIM] = v[:, h * HEAD_DIM:(h + 1) * HEAD_DIM].astype(vo_ref.dtype)
        vo_ref[:, (2 * h + 1) * HEAD_DIM:(2 * h + 2) * HEAD_DIM] = ones


def _qk_prep(proj, lay, cos_t, sin_t, gq, gk):
    t = proj.shape[0]
    tm = _pick(t, (256, 128))
    d_att = lay["d_att"]
    kvw = ATT_KV_HEADS * HEAD_DIM
    const = lambda i: (0, 0)
    return pl.pallas_call(
        _qk_prep_kernel,
        grid=(t // tm,),
        in_specs=[
            pl.BlockSpec((tm, d_att), lambda i: (i, lay["aq"] // d_att)),
            pl.BlockSpec((tm, kvw), lambda i: (i, lay["ak"] // kvw)),
            pl.BlockSpec((tm, kvw), lambda i: (i, lay["av"] // kvw)),
            pl.BlockSpec((tm, HEAD_DIM), lambda i: (i, 0)),
            pl.BlockSpec((tm, HEAD_DIM), lambda i: (i, 0)),
            pl.BlockSpec((1, HEAD_DIM), const),
            pl.BlockSpec((1, HEAD_DIM), const),
        ],
        out_specs=[
            pl.BlockSpec((tm, d_att), lambda i: (i, 0)),
            pl.BlockSpec((tm, kvw), lambda i: (i, 0)),
            pl.BlockSpec((tm, 2 * kvw), lambda i: (i, 0)),
        ],
        out_shape=[jax.ShapeDtypeStruct((t, d_att), BF16),
                   jax.ShapeDtypeStruct((t, kvw), BF16),
                   jax.ShapeDtypeStruct((t, 2 * kvw), BF16)],
        compiler_params=_cparams(("arbitrary",)),
        name="qk_norm_rope",
    )(proj, proj, proj, cos_t, sin_t, gq, gk)


FLASH_ROW_BLOCK = 16
FLASH_KV_SPLIT = 256


def _flash_kernel(q_ref, k_ref, v_ref, o_ref, s_scr, p_scr, m_scr, a_scr, acc_scr, *, group, tq):
    j = pl.program_id(2)

    @pl.when(j == 0)
    def _():
        m_scr[...] = jnp.full_like(m_scr, NEG_BIG)
        acc_scr[...] = jnp.zeros_like(acc_scr)

    n_rows = group * tq
    n_split, _, tks = s_scr.shape
    rb = FLASH_ROW_BLOCK
    lane_tiles = [slice(t * LANE, (t + 1) * LANE) for t in range(tks // LANE)]
    for c in range(n_split):
        kc = k_ref[c * tks:(c + 1) * tks, :]
        for h in range(group):
            s_scr[c, h * tq:(h + 1) * tq, :] = lax.dot_general(
                q_ref[:, h * HEAD_DIM:(h + 1) * HEAD_DIM], kc, (((1,), (1,)), ((), ())),
                preferred_element_type=F32)
    for c in range(n_split):
        for b in range(n_rows // rb):
            rows = slice(b * rb, (b + 1) * rb)
            mx = s_scr[c, rows, lane_tiles[0]]
            for tile in lane_tiles[1:]:
                mx = jnp.maximum(mx, s_scr[c, rows, tile])
            m_old = m_scr[rows, :]
            m_new = jnp.maximum(m_old, jnp.broadcast_to(jnp.max(mx, axis=1, keepdims=True), (rb, LANE)))
            m_scr[rows, :] = m_new
            a_scr[rows, :] = jnp.exp2(m_old - m_new)
        for b in range(n_rows // rb):
            rows = slice(b * rb, (b + 1) * rb)
            m = m_scr[rows, :]
            for tile in lane_tiles:
                p_scr[c, rows, tile] = jnp.exp2(s_scr[c, rows, tile] - m).astype(BF16)
            alpha = a_scr[rows, :]
            acc_scr[rows, :] = acc_scr[rows, :] * jnp.concatenate([alpha, alpha], axis=1)
        acc_scr[...] += jnp.dot(p_scr[c], v_ref[c * tks:(c + 1) * tks, :], preferred_element_type=F32)

    @pl.when(j == pl.num_programs(2) - 1)
    def _():
        for h in range(group):
            rows = slice(h * tq, (h + 1) * tq)
            o_ref[:, h * HEAD_DIM:(h + 1) * HEAD_DIM] = (
                acc_scr[rows, :HEAD_DIM] / acc_scr[rows, HEAD_DIM:]).astype(o_ref.dtype)


def _flash(qn, kn, vb, n_k):
    n_q, d_att = qn.shape
    group = d_att // HEAD_DIM // ATT_KV_HEADS
    gw = group * HEAD_DIM
    tq = _pick(n_q, (512, 256, 128))
    tk = _pick(n_k, (768, 512, 256, 128))
    tks = min(tk, FLASH_KV_SPLIT)
    return pl.pallas_call(
        functools.partial(_flash_kernel, group=group, tq=tq),
        grid=(ATT_KV_HEADS, n_q // tq, n_k // tk),
        in_specs=[
            pl.BlockSpec((tq, gw), lambda g, i, j: (i, g)),
            pl.BlockSpec((tk, HEAD_DIM), lambda g, i, j: (j, g)),
            pl.BlockSpec((tk, 2 * HEAD_DIM), lambda g, i, j: (j, g)),
        ],
        out_specs=pl.BlockSpec((tq, gw), lambda g, i, j: (i, g)),
        out_shape=jax.ShapeDtypeStruct((n_q, d_att), BF16),
        scratch_shapes=[pltpu.VMEM((tk // tks, group * tq, tks), F32),
                        pltpu.VMEM((tk // tks, group * tq, tks), BF16),
                        pltpu.VMEM((group * tq, LANE), F32),
                        pltpu.VMEM((group * tq, LANE), F32),
                        pltpu.VMEM((group * tq, 2 * HEAD_DIM), F32)],
        compiler_params=_cparams(("arbitrary", "arbitrary", "arbitrary")),
        name="flash_gqa",
    )(qn, kn, vb)


def _outproj_kernel(a1_ref, a2_ref, a3_ref, w_ref, o_ref):
    k1 = a1_ref.shape[1]
    k2 = k1 + a2_ref.shape[1]
    acc = jnp.dot(a1_ref[...], w_ref[:k1, :], preferred_element_type=F32)
    acc = acc + jnp.dot(a2_ref[...], w_ref[k1:k2, :], preferred_element_type=F32)
    acc = acc + jnp.dot(a3_ref[...], w_ref[k2:, :], preferred_element_type=F32)
    o_ref[...] = acc


def _outproj(l, a1, a2, a3, w):
    t = a1.shape[0]
    _, k, n = w.shape
    assert a1.shape[1] + a2.shape[1] + a3.shape[1] == k
    tm = _pick(t, (768, 512, 256, 128))
    tn = _pick(n, (512, 256, 128))
    lhs = lambda a: pl.BlockSpec((tm, a.shape[1]), lambda i, j: (i, 0))
    return pl.pallas_call(
        _outproj_kernel,
        grid=(t // tm, n // tn),
        in_specs=[lhs(a1), lhs(a2), lhs(a3), pl.BlockSpec((None, k, tn), lambda i, j: (l, 0, j))],
        out_specs=pl.BlockSpec((tm, tn), lambda i, j: (i, j)),
        out_shape=jax.ShapeDtypeStruct((t, n), F32),
        compiler_params=_cparams(("arbitrary", "arbitrary")),
        name="out_proj",
    )(a1, a2, a3, w)


def _topk_route(logits, n_experts):
    lane = lax.broadcasted_iota(jnp.int32, logits.shape, 1)
    l = jnp.where(lane < n_experts, logits, NEG_BIG)
    tops = []
    for _ in range(TOP_K):
        m = jnp.max(l, axis=1, keepdims=True)
        idx = jnp.min(jnp.where(l == m, lane, LANE), axis=1, keepdims=True)
        tops.append((m, idx))
        l = jnp.where(lane == idx, NEG_BIG, l)
    es = [jnp.exp(m - tops[0][0]) for m, _ in tops]
    den = es[0]
    for e in es[1:]:
        den = den + e
    wts = jnp.zeros(logits.shape, F32)
    ids = jnp.zeros(logits.shape, jnp.int32)
    for k, ((m, idx), e) in enumerate(zip(tops, es)):
        wts = jnp.where(lane == k, e / den, wts)
        ids = jnp.where(lane == k, idx, ids)
    return wts, ids


def _ln_route_kernel(x_ref, y_ref, gate_ref, lng_ref, lnb_ref, sh_ref, sc_ref, wr_ref, br_ref,
                     xo_ref, tok_ref, wts_ref, ids_ref, *, n_ctx, tm, alpha, n_experts):
    rows = pl.program_id(0) * tm + lax.broadcasted_iota(jnp.int32, (tm, 1), 0)
    is_ctx = rows < n_ctx
    gate = jnp.where(is_ctx, gate_ref[1:2, :], gate_ref[0:1, :])
    u = alpha * x_ref[...] + gate * y_ref[...]
    mu = jnp.mean(u, axis=-1, keepdims=True)
    uc = u - mu
    var = jnp.mean(uc * uc, axis=-1, keepdims=True)
    xn = uc * lax.rsqrt(var + 1e-5) * lng_ref[...] + lnb_ref[...]
    xo_ref[...] = xn
    sc = jnp.where(is_ctx, sc_ref[1:2, :], sc_ref[0:1, :])
    sh = jnp.where(is_ctx, sh_ref[1:2, :], sh_ref[0:1, :])
    tok = xn * (1.0 + sc) + sh
    tok_ref[...] = _pack_bf16_pairs(tok)
    t_hi, t_lo = _split_bf16(tok, 2)
    w_hi, w_lo = _split_bf16(wr_ref[...], 2)
    logits = (jnp.dot(t_hi, w_hi, preferred_element_type=F32)
              + jnp.dot(t_hi, w_lo, preferred_element_type=F32)
              + jnp.dot(t_lo, w_hi, preferred_element_type=F32)) + br_ref[...]
    wts_ref[...], ids_ref[...] = _topk_route(logits, n_experts)


def _resid_ln_route(xall, y, mod, gate_blk, sh_blk, sc_blk, lng, lnb, w_router, b_router, n_experts, n_ctx, alpha):
    t, d = xall.shape
    tm = _pick(t, (256, 128))
    const = lambda i: (0, 0)
    row = pl.BlockSpec((tm, d), lambda i: (i, 0))
    lanes = pl.BlockSpec((tm, LANE), lambda i: (i, 0))
    mod_blk = lambda blk: pl.BlockSpec((8, d), lambda i: (0, blk))
    return pl.pallas_call(
        functools.partial(_ln_route_kernel, n_ctx=n_ctx, tm=tm, alpha=alpha, n_experts=n_experts),
        grid=(t // tm,),
        in_specs=[row, row, mod_blk(gate_blk), pl.BlockSpec((1, d), const), pl.BlockSpec((1, d), const),
                  mod_blk(sh_blk), mod_blk(sc_blk), pl.BlockSpec((d, LANE), const),
                  pl.BlockSpec((1, LANE), const)],
        out_specs=[row, pl.BlockSpec((tm, d // 2), lambda i: (i, 0)), lanes, lanes],
        out_shape=[jax.ShapeDtypeStruct((t, d), F32), jax.ShapeDtypeStruct((t, d // 2), jnp.uint32),
                   jax.ShapeDtypeStruct((t, LANE), F32), jax.ShapeDtypeStruct((t, LANE), jnp.int32)],
        compiler_params=_cparams(("arbitrary",)),
        name="resid_ln_route",
    )(xall, y, mod, lng, lnb, mod, mod, w_router, b_router)


MOE_ROW_TILE = 256
COMBINE_TILE = 128


def _route_plan(ids, n_experts, tm):
    t = ids.shape[0]
    n_rows = t * TOP_K
    flat = ids[:, :TOP_K].reshape(-1)
    onehot = (flat[:, None] == jnp.arange(n_experts, dtype=jnp.int32)[None, :]).astype(jnp.int32)
    csum = jnp.cumsum(onehot, axis=0)
    rank = jnp.sum((csum - onehot) * onehot, axis=1)
    cnt = csum[-1]
    ptiles = (cnt + tm - 1) // tm
    tile_end = jnp.cumsum(ptiles)
    row_start = (tile_end - ptiles) * tm
    pos = (jnp.sum(onehot * row_start[None, :], axis=1) + rank).astype(jnp.int32)
    n_tiles = (n_rows + n_experts * (tm - 1)) // tm
    row_token = jnp.zeros((n_tiles * tm,), jnp.int32).at[pos].set(
        jnp.arange(n_rows, dtype=jnp.int32) // TOP_K)
    tiles = jnp.arange(n_tiles, dtype=jnp.int32)
    tile_expert = jnp.minimum(jnp.sum((tiles[:, None] >= tile_end[None, :]).astype(jnp.int32), axis=1),
                              n_experts - 1).astype(jnp.int32)
    tile_valid = (tiles < tile_end[-1]).astype(jnp.int32)
    return pos, row_token, tile_expert, tile_valid, n_tiles


HI16 = 0xFFFF0000


def _pack_bf16_pairs(x):
    half = x.shape[1] // 2
    lo = pltpu.bitcast(x[:, :half].astype(BF16).astype(F32), jnp.uint32) >> 16
    hi = pltpu.bitcast(x[:, half:].astype(BF16).astype(F32), jnp.uint32) & jnp.uint32(HI16)
    return hi | lo


def _unpack_bf16_pairs(w):
    return pltpu.bitcast(w << 16, F32), pltpu.bitcast(w & jnp.uint32(HI16), F32)


def _moe_kernel(te_ref, tv_ref, rt_ref, tok_hbm, wgu_ref, bgu_ref, wdn_ref, bdn_ref, sel_ref, o_ref,
                buf, sem, *, tm, n_tiles):
    i = pl.program_id(0)

    def issue(tile, slot):
        def body(r, carry):
            row = rt_ref[tile * tm + r]
            pltpu.make_async_copy(tok_hbm.at[pl.ds(row, 1), :], buf.at[slot, pl.ds(r, 1), :],
                                  sem.at[slot]).start()
            return carry
        lax.fori_loop(0, tm, body, 0, unroll=8)

    @pl.when(jnp.logical_and(i == 0, tv_ref[0] > 0))
    def _():
        issue(0, 0)

    nxt = jnp.minimum(i + 1, n_tiles - 1)

    @pl.when(jnp.logical_and(i + 1 < n_tiles, tv_ref[nxt] > 0))
    def _():
        issue(nxt, nxt % 2)

    slot = i % 2

    @pl.when(tv_ref[i] > 0)
    def _():
        pltpu.make_async_copy(tok_hbm.at[pl.ds(0, tm), :], buf.at[slot], sem.at[slot]).wait()
        x_lo, x_hi = _unpack_bf16_pairs(buf[slot])
        half = x_lo.shape[1]
        gu = (jnp.dot(x_lo.astype(BF16), wgu_ref[:half, :], preferred_element_type=F32)
              + jnp.dot(x_hi.astype(BF16), wgu_ref[half:, :], preferred_element_type=F32)) + bgu_ref[...]
        g = jnp.minimum(gu, SWIGLU_LIMIT)
        a = g * _sigmoid(SWIGLU_ALPHA * g)
        lin1 = jnp.clip(gu, -SWIGLU_LIMIT, SWIGLU_LIMIT) + 1.0
        pair = (a * pltpu.roll(lin1, gu.shape[1] - 1, axis=1)).astype(BF16)
        act = jnp.dot(pair, sel_ref[...], preferred_element_type=F32).astype(BF16)
        o_ref[...] = _pack_bf16_pairs(jnp.dot(act, wdn_ref[...], preferred_element_type=F32) + bdn_ref[...])

    @pl.when(tv_ref[i] == 0)
    def _():
        o_ref[...] = jnp.zeros_like(o_ref)


def _moe_experts(l, tok, plan, wgu, bgu, wdn, bdn):
    _, row_token, tile_expert, tile_valid, n_tiles = plan
    tm = MOE_ROW_TILE
    half = tok.shape[1]
    d = 2 * half
    two_de, de = wgu.shape[3], wdn.shape[2]
    sel = jnp.asarray(np.arange(two_de)[:, None] == 2 * np.arange(de)[None, :], BF16)
    grid_spec = pltpu.PrefetchScalarGridSpec(
        num_scalar_prefetch=3,
        grid=(n_tiles,),
        in_specs=[
            pl.BlockSpec(memory_space=pl.ANY),
            pl.BlockSpec((None, None, d, two_de), lambda i, te, tv, rt: (l, te[i], 0, 0)),
            pl.BlockSpec((None, None, 1, two_de), lambda i, te, tv, rt: (l, te[i], 0, 0)),
            pl.BlockSpec((None, None, de, d), lambda i, te, tv, rt: (l, te[i], 0, 0)),
            pl.BlockSpec((None, None, 1, d), lambda i, te, tv, rt: (l, te[i], 0, 0)),
            pl.BlockSpec((two_de, de), lambda i, te, tv, rt: (0, 0)),
        ],
        out_specs=pl.BlockSpec((tm, half), lambda i, te, tv, rt: (i, 0)),
        scratch_shapes=[pltpu.VMEM((2, tm, half), jnp.uint32), pltpu.SemaphoreType.DMA((2,))],
    )
    return pl.pallas_call(
        functools.partial(_moe_kernel, tm=tm, n_tiles=n_tiles),
        grid_spec=grid_spec,
        out_shape=jax.ShapeDtypeStruct((n_tiles * tm, half), jnp.uint32),
        compiler_params=_cparams(("arbitrary",)),
        name="moe_experts",
    )(tile_expert, tile_valid, row_token, tok, wgu, bgu, wdn, bdn, sel)


def _combine_ln_kernel(pos_ref, y_hbm, x_ref, wts_ref, gate_ref, lng_ref, lnb_ref, xo_ref, buf, sem,
                       *, n_ctx, tm, alpha, n_steps):
    i = pl.program_id(0)

    def issue(tile, slot):
        def body(r, carry):
            for k in range(TOP_K):
                row = pos_ref[(tile * tm + r) * TOP_K + k]
                pltpu.make_async_copy(y_hbm.at[pl.ds(row, 1), :], buf.at[slot, k, pl.ds(r, 1), :],
                                      sem.at[slot]).start()
            return carry
        lax.fori_loop(0, tm, body, 0, unroll=4)

    @pl.when(i == 0)
    def _():
        issue(0, 0)

    @pl.when(i + 1 < n_steps)
    def _():
        issue(i + 1, (i + 1) % 2)

    slot = i % 2
    for k in range(TOP_K):
        pltpu.make_async_copy(y_hbm.at[pl.ds(0, tm), :], buf.at[slot, k], sem.at[slot]).wait()
    wts = wts_ref[...]
    f_lo, f_hi = _unpack_bf16_pairs(buf[slot, 0])
    f_lo, f_hi = wts[:, 0:1] * f_lo, wts[:, 0:1] * f_hi
    for k in range(1, TOP_K):
        y_lo, y_hi = _unpack_bf16_pairs(buf[slot, k])
        f_lo = f_lo + wts[:, k:k + 1] * y_lo
        f_hi = f_hi + wts[:, k:k + 1] * y_hi
    f = jnp.concatenate([f_lo, f_hi], axis=1)
    rows = i * tm + lax.broadcasted_iota(jnp.int32, (tm, 1), 0)
    gate = jnp.where(rows < n_ctx, gate_ref[1:2, :], gate_ref[0:1, :])
    u = alpha * x_ref[...] + gate * f
    mu = jnp.mean(u, axis=-1, keepdims=True)
    uc = u - mu
    var = jnp.mean(uc * uc, axis=-1, keepdims=True)
    xo_ref[...] = uc * lax.rsqrt(var + 1e-5) * lng_ref[...] + lnb_ref[...]


def _combine_ln(x1, y_sorted, pos, wts, mod, gate_blk, lng, lnb, n_ctx, alpha):
    t, d = x1.shape
    tm = COMBINE_TILE
    n_steps = t // tm
    const = lambda i, p: (0, 0)
    grid_spec = pltpu.PrefetchScalarGridSpec(
        num_scalar_prefetch=1,
        grid=(n_steps,),
        in_specs=[
            pl.BlockSpec(memory_space=pl.ANY),
            pl.BlockSpec((tm, d), lambda i, p: (i, 0)),
            pl.BlockSpec((tm, LANE), lambda i, p: (i, 0)),
            pl.BlockSpec((8, d), lambda i, p: (0, gate_blk)),
            pl.BlockSpec((1, d), const),
            pl.BlockSpec((1, d), const),
        ],
        out_specs=pl.BlockSpec((tm, d), lambda i, p: (i, 0)),
        scratch_shapes=[pltpu.VMEM((2, TOP_K, tm, d // 2), jnp.uint32), pltpu.SemaphoreType.DMA((2,))],
    )
    return pl.pallas_call(
        functools.partial(_combine_ln_kernel, n_ctx=n_ctx, tm=tm, alpha=alpha, n_steps=n_steps),
        grid_spec=grid_spec,
        out_shape=jax.ShapeDtypeStruct((t, d), F32),
        compiler_params=_cparams(("arbitrary",)),
        name="moe_combine_ln",
    )(pos, y_sorted, x1, wts, mod, lng, lnb)


GATE_COL0 = 48
MLS_IO_BLOCK = 512


def _layout(d_model):
    d_ssd = 3 * d_model // 8
    d_att = d_model // 4
    d_mls = d_model - d_ssd - d_att
    ssd_heads = d_ssd // SSD_HEADDIM
    mls_heads = d_mls // MLS_V_DIM
    gs = SSD_GROUPS * SSD_STATE
    kvw = ATT_KV_HEADS * HEAD_DIM
    qkw = mls_heads * MLS_QK_DIM
    assert 2 * ssd_heads == GATE_COL0 and GATE_COL0 + 2 * mls_heads <= SMALL_W
    sizes = (d_ssd, d_ssd + 2 * gs, 2 * ssd_heads, d_att, kvw, kvw, 2 * qkw, d_mls, d_mls, 4 * mls_heads)
    o = np.concatenate([[0], np.cumsum(sizes)])
    src = dict(z=o[0], xbc=o[1], dt=o[2], aq=o[3], ak=o[4], av=o[5], mqk=o[6], mv=o[7], mo=o[8], mg=o[9])
    lay = dict(d_ssd=d_ssd, d_att=d_att, d_mls=d_mls, ssd_heads=ssd_heads, mls_heads=mls_heads,
               src={k: int(v) for k, v in src.items()}, gs=gs, kvw=kvw, qkw=qkw)
    cols = []

    def put(name, idx):
        lay[name] = len(cols)
        cols.extend(int(v) for v in idx)

    put("z", src["z"] + np.arange(d_ssd))
    put("xbc", src["xbc"] + np.arange(d_ssd + 2 * gs))
    put("aq", src["aq"] + np.arange(d_att))
    put("ak", src["ak"] + np.arange(kvw))
    put("av", src["av"] + np.arange(kvw))
    put("mqk", src["mqk"] + np.arange(2 * qkw))
    put("mv", src["mv"] + np.arange(d_mls))
    put("mo", src["mo"] + np.arange(d_mls))
    sm1 = -np.ones(SMALL_W, np.int64)
    sm1[:2 * ssd_heads] = src["dt"] + np.arange(2 * ssd_heads)
    sm2 = -np.ones(SMALL_W, np.int64)
    for direction in range(2):
        c = GATE_COL0 + direction * mls_heads
        sm1[c:c + mls_heads] = src["mg"] + (2 * direction + 1) * mls_heads + np.arange(mls_heads)
        sm2[c:c + mls_heads] = src["mg"] + (2 * direction) * mls_heads + np.arange(mls_heads)
    put("sm1", sm1)
    put("sm2", sm2)
    n_pad = -len(cols) % 512
    cols.extend([-1] * n_pad)
    lay["cols"] = np.asarray(cols, np.int64)
    lay["co_q"], lay["co_k"], lay["co_xs"] = 0, qkw, 2 * qkw
    lay["co_b"], lay["co_c"] = 2 * qkw + d_ssd, 2 * qkw + d_ssd + gs
    for name, width in (("z", d_ssd), ("mv", MLS_IO_BLOCK), ("mo", MLS_IO_BLOCK), ("aq", d_att), ("ak", kvw),
                        ("av", kvw), ("sm1", SMALL_W), ("sm2", SMALL_W), ("mqk", LANE), ("xbc", LANE)):
        assert lay[name] % width == 0, name
    assert d_mls % MLS_IO_BLOCK == 0
    assert lay["co_k"] % qkw == 0 and lay["co_xs"] % d_ssd == 0 and lay["co_b"] % gs == 0 and lay["co_c"] % gs == 0
    return lay


RELAYOUT_BLOCK = 512


def _w_in_relayout_kernel(b0, b1, b2, b3, b4, tail_ref, o_ref, *, n_a, n_b, shift):
    j = pl.program_id(0)

    @pl.when(j < n_a)
    def _():
        o_ref[...] = jnp.concatenate([b0[...], b1[...], b2[...], b3[...]], axis=1).astype(BF16)

    @pl.when(jnp.logical_and(j >= n_a, j < n_a + n_b))
    def _():
        win = jnp.concatenate([b0[...], b1[...], b2[...], b3[...], b4[...]], axis=1)
        o_ref[...] = win[:, shift:shift + RELAYOUT_BLOCK].astype(BF16)

    @pl.when(j >= n_a + n_b)
    def _():
        o_ref[...] = tail_ref[...].astype(BF16)


def _relayout_w_in(l, w_in, lay):
    _, d, _ = w_in.shape
    s = lay["src"]
    sh, mh = lay["ssd_heads"], lay["mls_heads"]
    blk = RELAYOUT_BLOCK
    per = blk // LANE
    n_out = len(lay["cols"]) // blk
    shift = s["aq"] % LANE
    assert (s["dt"] - s["z"]) % blk == 0 and (s["mg"] - s["aq"]) % blk == 0 and s["z"] == 0
    n_a = (s["dt"] - s["z"]) // blk
    n_b = (s["mg"] - s["aq"]) // blk - 1
    base_b = (s["aq"] - shift) // LANE
    w = w_in[l]
    zeros = lambda n: jnp.zeros((d, n), w.dtype)
    mg = w[:, s["mg"]:s["mg"] + 4 * mh].reshape(d, 4, mh)
    pad = SMALL_W - GATE_COL0 - 2 * mh
    parts = [w[:, s["mg"] - blk:s["mg"]],
             w[:, s["dt"]:s["dt"] + 2 * sh], mg[:, 1], mg[:, 3], zeros(pad),
             zeros(GATE_COL0), mg[:, 0], mg[:, 2], zeros(pad)]
    parts.append(zeros((n_out - n_a - n_b) * blk - sum(a.shape[1] for a in parts)))
    tail = jnp.concatenate(parts, axis=1)

    def src_block(k):
        def index(j):
            shifted = base_b + per * (j - n_a) + k
            return (l, 0, jnp.where(j < n_a, per * j + k, jnp.where(j < n_a + n_b, shifted, 0)))
        return pl.BlockSpec((None, d, LANE), index)

    return pl.pallas_call(
        functools.partial(_w_in_relayout_kernel, n_a=n_a, n_b=n_b, shift=shift),
        grid=(n_out,),
        in_specs=[src_block(k) for k in range(per + 1)]
        + [pl.BlockSpec((d, blk), lambda j: (0, jnp.maximum(j - n_a - n_b, 0)))],
        out_specs=pl.BlockSpec((d, blk), lambda j: (0, j)),
        out_shape=jax.ShapeDtypeStruct((d, n_out * blk), BF16),
        compiler_params=_cparams(("arbitrary",)),
        name="w_in_relayout",
    )(*([w_in] * (per + 1)), tail)


def _rope_tables(n_ctx, seq):
    rows = seq // GRID_W
    row = jnp.repeat(jnp.arange(rows), GRID_W).astype(F32)
    col = jnp.tile(jnp.arange(GRID_W), rows).astype(F32)
    n_freq = HEAD_DIM // 4
    inv = ROPE_THETA ** (-jnp.arange(n_freq, dtype=F32) / n_freq)
    ang = jnp.concatenate([row[:, None] * inv, col[:, None] * inv], axis=-1)
    cos, sin = jnp.cos(ang), jnp.sin(ang)
    cos_pairs = jnp.repeat(cos, 2, axis=-1)
    sin_pairs = jnp.stack([-sin, sin], axis=-1).reshape(seq, HEAD_DIM)
    cos_t = jnp.concatenate([jnp.ones((n_ctx, HEAD_DIM), F32), cos_pairs], axis=0)
    sin_t = jnp.concatenate([jnp.zeros((n_ctx, HEAD_DIM), F32), sin_pairs], axis=0)
    return cos_t, sin_t


def _pad_row(v, width=SMALL_W, at=0):
    out = jnp.zeros((1, width), F32)
    return out.at[0, at:at + v.shape[0]].set(v.astype(F32))


def _layer(l, xall, mod, lay, tables, n_ctx, alpha, p):
    d = xall.shape[1]
    proj = _inproj(xall, mod, _relayout_w_in(l, p["w_in"], lay), n_ctx)

    conv_w = jnp.concatenate([p["mls_conv_w"][l], p["ssd_conv_w"][l]], axis=1)
    conv_w8 = jnp.concatenate([conv_w, jnp.zeros((8 - CONV_W, conv_w.shape[1]), F32)], axis=0)
    conv_b = jnp.concatenate([p["mls_conv_b"][l], p["ssd_conv_b"][l]])[None, :]
    conv_out = _conv(proj, conv_w8, conv_b, (lay["mqk"], 2 * lay["qkw"]),
                     (lay["xbc"], lay["d_ssd"] + 2 * lay["gs"]), n_ctx)

    sh, mh = lay["ssd_heads"], lay["mls_heads"]
    gate_b = p["mls_gate_b"][l]
    bias1 = (_pad_row(p["ssd_dt_bias"][l].reshape(-1))
             + _pad_row(gate_b[1], at=GATE_COL0) + _pad_row(gate_b[3], at=GATE_COL0 + mh))
    bias2 = _pad_row(gate_b[0], at=GATE_COL0) + _pad_row(gate_b[2], at=GATE_COL0 + mh)
    alog_row = _pad_row(p["ssd_A_log"][l].reshape(-1))
    hcol = np.repeat(np.arange(sh), SSD_HEADDIM)
    e_f = jnp.asarray(np.arange(SMALL_W)[:, None] == hcol[None, :], BF16)
    e_b = jnp.asarray(np.arange(SMALL_W)[:, None] == (hcol + sh)[None, :], BF16)
    dskip_x = jnp.repeat(p["ssd_D"][l], SSD_HEADDIM)[None, :]
    y_f = _ssd_pass(conv_out, proj, lay, bias1, alog_row, e_f, n_ctx, False)
    ssd = _ssd_pass(conv_out, proj, lay, bias1, alog_row, e_b, n_ctx, True,
                    extra=(y_f, dskip_x, p["ssd_norm_g"][l][None, :]))

    h_f = _mls_pass(conv_out, proj, lay, bias1, bias2, n_ctx, False)
    mls = _mls_pass(conv_out, proj, lay, bias1, bias2, n_ctx, True, extra=(h_f, p["mls_norm_g"][l][None, :]))

    gq = p["att_q_norm_g"][l][None, :]
    gk = p["att_k_norm_g"][l][None, :]
    qn, kn, vb = _qk_prep(proj, lay, tables[0], tables[1], gq, gk)
    t = xall.shape[0]
    att_c = _flash(qn[:n_ctx], kn, vb, n_ctx)
    att_l = _flash(qn[n_ctx:], kn, vb, t)
    att = jnp.concatenate([att_c, att_l], axis=0)

    mix = _outproj(l, ssd, att, mls, p["w_out_bf16"])

    ne = p["w_router"].shape[2]
    w_router = jnp.zeros((d, LANE), F32).at[:, :ne].set(p["w_router"][l])
    b_router = _pad_row(p["b_router"][l], LANE)
    x1, tok, wts, ids = _resid_ln_route(xall, mix, mod, 2, 3, 4, p["ln1_g"][l][None, :], p["ln1_b"][l][None, :],
                                        w_router, b_router, ne, n_ctx, alpha)

    plan = _route_plan(ids, ne, MOE_ROW_TILE)
    y_sorted = _moe_experts(l, tok, plan, p["w_gu_bf16"], p["b_gu"][:, :, None, :], p["w_dn_bf16"],
                            p["b_dn"][:, :, None, :])
    x2 = _combine_ln(x1, y_sorted, plan[0], wts, mod, 5, p["ln2_g"][l][None, :], p["ln2_b"][l][None, :],
                     n_ctx, alpha)
    aux = dict(proj=proj, conv_out=conv_out, ssd=ssd, mls=mls, att=att, mix=mix, x1=x1, tok=tok, wts=wts,
               ids=ids, y_sorted=y_sorted, pos=plan[0])
    return x2, aux


def kernel(x, c, ctx, c_ctx, w_ada, b_ada, w_in, ssd_conv_w, ssd_conv_b, ssd_A_log, ssd_dt_bias, ssd_D,
           ssd_norm_g, att_q_norm_g, att_k_norm_g, mls_conv_w, mls_conv_b, mls_gate_b, mls_norm_g, w_out,
           ln1_g, ln1_b, w_router, b_router, w_gu, b_gu, w_dn, b_dn, ln2_g, ln2_b):
    p = dict(w_in=w_in, ssd_conv_w=ssd_conv_w, ssd_conv_b=ssd_conv_b, ssd_A_log=ssd_A_log,
             ssd_dt_bias=ssd_dt_bias, ssd_D=ssd_D, ssd_norm_g=ssd_norm_g, att_q_norm_g=att_q_norm_g,
             att_k_norm_g=att_k_norm_g, mls_conv_w=mls_conv_w, mls_conv_b=mls_conv_b, mls_gate_b=mls_gate_b,
             mls_norm_g=mls_norm_g, w_out=w_out, ln1_g=ln1_g, ln1_b=ln1_b, w_router=w_router,
             b_router=b_router, w_gu=w_gu, b_gu=b_gu, w_dn=w_dn, b_dn=b_dn, ln2_g=ln2_g, ln2_b=ln2_b)
    p["w_gu_bf16"] = w_gu.astype(BF16)
    p["w_dn_bf16"] = w_dn.astype(BF16)
    p["w_out_bf16"] = w_out.astype(BF16)
    batch, seq, d = x.shape
    assert batch == 1
    n_ctx = ctx.shape[1]
    depth = w_ada.shape[0]
    alpha = (2 * depth) ** 0.25
    lay = _layout(d)
    tables = _rope_tables(n_ctx, seq)
    cvec = jnp.zeros((8, d), F32).at[0].set(c[0]).at[1].set(c_ctx)
    mods = _ada(cvec, w_ada, b_ada)
    xall = jnp.concatenate([ctx[0], x[0]], axis=0)
    for l in range(depth):
        xall, _ = _layer(l, xall, mods[l], lay, tables, n_ctx, alpha, p)
    return xall[n_ctx:][None]
```

```python
import functools

import numpy as np
import jax
import jax.numpy as jnp
from jax import lax
from jax.experimental import pallas as pl
from jax.experimental.pallas import tpu as pltpu

F32 = jnp.float32
BF16 = jnp.bfloat16

GRID_W = 64
CHUNK = 128
CONV_W = 5
SSD_HEADDIM = 64
SSD_GROUPS = 4
SSD_STATE = 128
HEAD_DIM = 128
ATT_KV_HEADS = 2
ROPE_THETA = 10000.0
MLS_V_DIM = 256
MLS_QK_DIM = 128
TOP_K = 4
SWIGLU_LIMIT = 7.0
SWIGLU_ALPHA = 1.702

LANE = 128
SMALL_W = LANE
NEG_BIG = -1e30
LOG2_E = 1.4426950408889634
VMEM_LIMIT = 56 * 1024 * 1024


def _cparams(sem, vmem=VMEM_LIMIT):
    return pltpu.CompilerParams(dimension_semantics=sem, vmem_limit_bytes=vmem)


def _pick(n, cands):
    for c in cands:
        if n % c == 0:
            return c
    raise ValueError(f"no tile in {cands} divides {n}")


def _sigmoid(x):
    return 1.0 / (1.0 + jnp.exp(-x))


def _softplus(x):
    return jnp.maximum(x, 0.0) + jnp.log(1.0 + jnp.exp(-jnp.abs(x)))


def _split_bf16(a, n):
    parts = []
    r = a
    for _ in range(n):
        p = r.astype(BF16)
        parts.append(p)
        r = r - p.astype(F32)
    return parts


def _dot01_left(m01, a, n=3):
    out = None
    for p in _split_bf16(a, n):
        t = jnp.dot(m01, p, preferred_element_type=F32)
        out = t if out is None else out + t
    return out


def _dot01_right(a, m01, n=2):
    out = None
    for p in _split_bf16(a, n):
        t = jnp.dot(p, m01, preferred_element_type=F32)
        out = t if out is None else out + t
    return out


def _ada_kernel(c_ref, w_ref, b_ref, o_ref):
    c = c_ref[...]
    s = (c * _sigmoid(c)).astype(BF16)
    o_ref[...] = jnp.dot(s, w_ref[...].astype(BF16), preferred_element_type=F32) + b_ref[...]


def _ada(cvec, w_ada, b_ada):
    depth, d, n = w_ada.shape
    tn = _pick(n, (512, 256, 128))
    return pl.pallas_call(
        _ada_kernel,
        grid=(depth, n // tn),
        in_specs=[
            pl.BlockSpec((8, d), lambda l, j: (0, 0)),
            pl.BlockSpec((None, d, tn), lambda l, j: (l, 0, j)),
            pl.BlockSpec((None, 1, tn), lambda l, j: (l, 0, j)),
        ],
        out_specs=pl.BlockSpec((None, 8, tn), lambda l, j: (l, 0, j)),
        out_shape=jax.ShapeDtypeStruct((depth, 8, n), F32),
        compiler_params=_cparams(("arbitrary", "arbitrary")),
        name="ada_mod",
    )(cvec, w_ada, b_ada.reshape(depth, 1, n))


def _inproj_kernel(x_ref, sh_ref, sc_ref, w_ref, o_ref, h_scr, *, n_ctx, tm):
    i = pl.program_id(0)
    j = pl.program_id(1)

    @pl.when(j == 0)
    def _():
        rows = i * tm + lax.broadcasted_iota(jnp.int32, (tm, 1), 0)
        is_ctx = rows < n_ctx
        sc = jnp.where(is_ctx, sc_ref[1:2, :], sc_ref[0:1, :])
        sh = jnp.where(is_ctx, sh_ref[1:2, :], sh_ref[0:1, :])
        h_scr[...] = (x_ref[...] * (1.0 + sc) + sh).astype(BF16)

    o_ref[...] = jnp.dot(h_scr[...], w_ref[...], preferred_element_type=F32)


def _inproj(xall, mod, w, n_ctx):
    t, d = xall.shape
    n = w.shape[1]
    tm = _pick(t, (768, 512, 256, 128))
    tn = _pick(n, (768, 512, 256, 128))
    return pl.pallas_call(
        functools.partial(_inproj_kernel, n_ctx=n_ctx, tm=tm),
        grid=(t // tm, n // tn),
        in_specs=[
            pl.BlockSpec((tm, d), lambda i, j: (i, 0), pipeline_mode=pl.Buffered(1)),
            pl.BlockSpec((8, d), lambda i, j: (0, 0)),
            pl.BlockSpec((8, d), lambda i, j: (0, 1)),
            pl.BlockSpec((d, tn), lambda i, j: (0, j)),
        ],
        out_specs=pl.BlockSpec((tm, tn), lambda i, j: (i, j)),
        out_shape=jax.ShapeDtypeStruct((t, n), F32),
        scratch_shapes=[pltpu.VMEM((tm, d), BF16)],
        compiler_params=_cparams(("arbitrary", "arbitrary")),
        name="in_proj",
    )(xall, mod, mod, w)


def _conv_kernel(x_ref, w_ref, b_ref, o_ref, *, n_ctx, tt):
    t = x_ref.shape[0]
    w = w_ref[...]
    bias = b_ref[...]
    halo = 8
    pad = CONV_W // 2

    def body(i, carry):
        t0 = pl.multiple_of(i * tt, tt)
        cur = x_ref[pl.ds(t0, tt), :]
        p0 = pl.multiple_of(jnp.maximum(t0 - halo, 0), halo)
        n0 = pl.multiple_of(jnp.minimum(t0 + tt, t - halo), halo)
        prev = x_ref[pl.ds(p0, halo), :]
        nxt = x_ref[pl.ds(n0, halo), :]
        seg_start = jnp.logical_or(t0 == 0, t0 == n_ctx)
        seg_end = jnp.logical_or(t0 + tt == n_ctx, t0 + tt == t)
        prev = jnp.where(seg_start, 0.0, prev)
        nxt = jnp.where(seg_end, 0.0, nxt)
        win = jnp.concatenate([prev, cur, nxt], axis=0)
        acc = bias + w[0:1, :] * win[halo - pad:halo - pad + tt, :]
        for k in range(1, CONV_W):
            acc = acc + w[k:k + 1, :] * win[halo - pad + k:halo - pad + k + tt, :]
        o_ref[pl.ds(t0, tt), :] = acc * _sigmoid(acc)
        return carry

    lax.fori_loop(0, t // tt, body, 0)


def _conv(proj, w8, b, first, second, n_ctx):
    t = proj.shape[0]
    tt = _pick(n_ctx, (256, 128))
    assert t % tt == 0 and all(v % LANE == 0 for v in first + second)
    n_first = first[1] // LANE
    cb_first, cb_second = first[0] // LANE, second[0] // LANE
    width = first[1] + second[1]
    return pl.pallas_call(
        functools.partial(_conv_kernel, n_ctx=n_ctx, tt=tt),
        grid=(width // LANE,),
        in_specs=[
            pl.BlockSpec((t, LANE), lambda c: (0, jnp.where(c < n_first, cb_first + c, cb_second + c - n_first))),
            pl.BlockSpec((8, LANE), lambda c: (0, c)),
            pl.BlockSpec((1, LANE), lambda c: (0, c)),
        ],
        out_specs=pl.BlockSpec((t, LANE), lambda c: (0, c)),
        out_shape=jax.ShapeDtypeStruct((t, width), F32),
        compiler_params=_cparams(("arbitrary",)),
        name="dwconv_silu",
    )(proj, w8, b)


def _chunk_order(i, n_chunks, n_ctx_chunks, reverse):
    if not reverse:
        return i
    return jnp.where(i < n_ctx_chunks, n_ctx_chunks - 1 - i, n_chunks - 1 - (i - n_ctx_chunks))


def _tri_mask(reverse):
    row = lax.broadcasted_iota(jnp.int32, (CHUNK, CHUNK), 0)
    col = lax.broadcasted_iota(jnp.int32, (CHUNK, CHUNK), 1)
    return (col >= row) if reverse else (col <= row)


def _ssd_kernel(*refs, reverse, finish, heads, hpg):
    if finish:
        (xs_ref, b_ref, c_ref, sm_ref, bias_ref, alog_ref, exp_ref,
         z_ref, yprev_ref, dskip_ref, g_ref, o_ref, h_scr) = refs
    else:
        (xs_ref, b_ref, c_ref, sm_ref, bias_ref, alog_ref, exp_ref, o_ref, h_scr) = refs
    gw = hpg * SSD_HEADDIM

    @pl.when(pl.program_id(0) == 0)
    def _():
        h_scr[...] = jnp.zeros_like(h_scr)

    mask = _tri_mask(reverse)
    tri = mask.astype(BF16)
    p = sm_ref[...] + bias_ref[...]
    dt = _softplus(p)
    dta = dt * (-jnp.exp(alog_ref[...]))
    cum = _dot01_left(tri, dta)
    cum_t = cum.T
    tot = cum[0:1, :] if reverse else cum[CHUNK - 1:CHUNK, :]
    e01 = exp_ref[...]
    dt_x = _dot01_right(dt, e01)
    in_x = _dot01_right(jnp.exp(cum), e01)
    tail_x = _dot01_right(jnp.exp(tot - cum), e01)
    tot_x = _dot01_right(jnp.broadcast_to(jnp.exp(tot), (8, SMALL_W)), e01)[0:1, :]

    xs = xs_ref[...]
    xdt = xs * dt_x
    xdt_b = xdt.astype(BF16)
    xtail_b = (xdt * tail_x).astype(BF16)
    bm = b_ref[...]
    cm = c_ref[...]
    c0 = heads if reverse else 0
    ys = []
    for g in range(SSD_GROUPS):
        bg = bm[:, g * SSD_STATE:(g + 1) * SSD_STATE]
        cg = cm[:, g * SSD_STATE:(g + 1) * SSD_STATE].astype(BF16)
        cb = lax.dot_general(cg, bg.astype(BF16), (((1,), (1,)), ((), ())),
                             preferred_element_type=F32)
        h_t = h_scr[g]
        y_g = jnp.dot(cg, h_t.astype(BF16), preferred_element_type=F32) * in_x[:, g * gw:(g + 1) * gw]
        parts = []
        for r in range(hpg):
            h = g * hpg + r
            c = c0 + h
            seg = cum[:, c:c + 1] - cum_t[c:c + 1, :]
            decay = jnp.exp(jnp.where(mask, seg, NEG_BIG))
            m = (cb * decay).astype(BF16)
            parts.append(jnp.dot(m, xdt_b[:, h * SSD_HEADDIM:(h + 1) * SSD_HEADDIM],
                                 preferred_element_type=F32))
        ys.append(y_g + jnp.concatenate(parts, axis=1))
        h_scr[g] = h_t * tot_x[:, g * gw:(g + 1) * gw] + jnp.dot(
            bg.T.astype(BF16), xtail_b[:, g * gw:(g + 1) * gw], preferred_element_type=F32)
    y = jnp.concatenate(ys, axis=1)

    if finish:
        z = z_ref[...]
        yt = (yprev_ref[...] + y + dskip_ref[...] * xs) * (z * _sigmoid(z))
        ms = jnp.mean(yt * yt, axis=-1, keepdims=True)
        o_ref[...] = (yt * lax.rsqrt(ms + 1e-6) * g_ref[...]).astype(o_ref.dtype)
    else:
        o_ref[...] = y


def _ssd_pass(conv_out, proj, lay, bias1, alog_row, e01, n_ctx, reverse, extra=None):
    t = proj.shape[0]
    nch, ncc = t // CHUNK, n_ctx // CHUNK
    d_ssd = lay["d_ssd"]
    heads = d_ssd // SSD_HEADDIM
    gs = SSD_GROUPS * SSD_STATE
    order = functools.partial(_chunk_order, n_chunks=nch, n_ctx_chunks=ncc, reverse=reverse)
    const = lambda i: (0, 0)
    in_specs = [
        pl.BlockSpec((CHUNK, d_ssd), lambda i: (order(i), lay["co_xs"] // d_ssd)),
        pl.BlockSpec((CHUNK, gs), lambda i: (order(i), lay["co_b"] // gs)),
        pl.BlockSpec((CHUNK, gs), lambda i: (order(i), lay["co_c"] // gs)),
        pl.BlockSpec((CHUNK, SMALL_W), lambda i: (order(i), lay["sm1"] // SMALL_W)),
        pl.BlockSpec((1, SMALL_W), const),
        pl.BlockSpec((1, SMALL_W), const),
        pl.BlockSpec((SMALL_W, d_ssd), const),
    ]
    args = [conv_out, conv_out, conv_out, proj, bias1, alog_row, e01]
    finish = extra is not None
    if finish:
        y_prev, dskip_x, norm_g = extra
        in_specs += [
            pl.BlockSpec((CHUNK, d_ssd), lambda i: (order(i), lay["z"] // d_ssd)),
            pl.BlockSpec((CHUNK, d_ssd), lambda i: (order(i), 0)),
            pl.BlockSpec((1, d_ssd), const),
            pl.BlockSpec((1, d_ssd), const),
        ]
        args += [proj, y_prev, dskip_x, norm_g]
    return pl.pallas_call(
        functools.partial(_ssd_kernel, reverse=reverse, finish=finish, heads=heads,
                          hpg=heads // SSD_GROUPS),
        grid=(nch,),
        in_specs=in_specs,
        out_specs=pl.BlockSpec((CHUNK, d_ssd), lambda i: (order(i), 0)),
        out_shape=jax.ShapeDtypeStruct((t, d_ssd), BF16 if finish else F32),
        scratch_shapes=[pltpu.VMEM((SSD_GROUPS, SSD_STATE, d_ssd // SSD_GROUPS), F32)],
        compiler_params=_cparams(("arbitrary",)),
        name="ssd_bwd_finish" if finish else "ssd_fwd",
    )(*args)


def _mls_kernel(*refs, reverse, finish, heads, n_io):
    q_ref, k_ref = refs[:2]
    v_refs = refs[2:2 + n_io]
    sm1_ref, sm2_ref, b1_ref, b2_ref = refs[2 + n_io:6 + n_io]
    if finish:
        og_refs = refs[6 + n_io:6 + 2 * n_io]
        hprev_ref, gain_ref, o_ref, s_scr, m_scr = refs[6 + 2 * n_io:]
    else:
        o_ref, s_scr, m_scr = refs[6 + n_io:]

    @pl.when(pl.program_id(0) == 0)
    def _():
        s_scr[...] = jnp.zeros_like(s_scr)
        m_scr[...] = jnp.zeros_like(m_scr)

    mask = _tri_mask(reverse)
    tri = mask.astype(BF16)
    lf = -_softplus(-(sm1_ref[...] + b1_ref[...]))
    li = sm2_ref[...] + b2_ref[...]
    b = _dot01_left(tri, lf)
    b_t = b.T
    li_t = li.T
    tot = b[0:1, :] if reverse else b[CHUNK - 1:CHUNK, :]
    m_row = m_scr[...]
    g_all = tot - b + li
    m_new = jnp.maximum(tot + m_row, jnp.max(g_all, axis=0, keepdims=True))
    wk_all = jnp.exp(g_all - m_new)
    decay_row = jnp.exp(tot + m_row - m_new)
    mprev_all = b + m_row
    m_scr[...] = m_new

    q = q_ref[...]
    k = k_ref[...] * (MLS_QK_DIM ** -0.5)
    v = jnp.concatenate([r[...] for r in v_refs], axis=1)
    ones_col = (lax.broadcasted_iota(jnp.int32, (CHUNK, LANE), 1) == 0).astype(F32)
    c0 = lay_fcol(heads, reverse)
    outs = []
    for h in range(heads):
        c = c0 + h
        qh = q[:, h * MLS_QK_DIM:(h + 1) * MLS_QK_DIM].astype(BF16)
        kh = k[:, h * MLS_QK_DIM:(h + 1) * MLS_QK_DIM]
        khb = kh.astype(BF16)
        vext = jnp.concatenate([v[:, h * MLS_V_DIM:(h + 1) * MLS_V_DIM], ones_col], axis=1)
        dmat = jnp.where(mask, b[:, c:c + 1] - b_t[c:c + 1, :] + li_t[c:c + 1, :], NEG_BIG)
        m_prev = mprev_all[:, c:c + 1]
        m_t = jnp.maximum(m_prev, jnp.max(dmat, axis=1, keepdims=True))
        qk = lax.dot_general(qh, khb, (((1,), (1,)), ((), ())), preferred_element_type=F32)
        w = jnp.exp(dmat - m_t) * qk
        s_prev = jnp.exp(m_prev - m_t)
        s_h = s_scr[h]
        numx = (jnp.dot(w.astype(BF16), vext.astype(BF16), preferred_element_type=F32)
                + s_prev * jnp.dot(qh, s_h.astype(BF16), preferred_element_type=F32))
        den = numx[:, MLS_V_DIM:MLS_V_DIM + 1]
        outs.append(numx[:, :MLS_V_DIM] / jnp.maximum(jnp.abs(den), jnp.exp(-m_t)))
        s_scr[h] = decay_row[:, c:c + 1] * s_h + jnp.dot(
            kh.T.astype(BF16), (wk_all[:, c:c + 1] * vext).astype(BF16), preferred_element_type=F32)

    if finish:
        hp = hprev_ref[...]
        og = jnp.concatenate([r[...] for r in og_refs], axis=1)
        gain = gain_ref[...]
        for h in range(heads):
            sl = slice(h * MLS_V_DIM, (h + 1) * MLS_V_DIM)
            hs = hp[:, sl] + outs[h]
            ms = jnp.mean(hs * hs, axis=-1, keepdims=True)
            o_ref[:, sl] = (_sigmoid(og[:, sl]) * (hs * lax.rsqrt(ms + 1e-6) * gain[:, sl])).astype(o_ref.dtype)
    else:
        for h in range(heads):
            o_ref[:, h * MLS_V_DIM:(h + 1) * MLS_V_DIM] = outs[h]


def lay_fcol(heads, reverse):
    return GATE_COL0 + (heads if reverse else 0)


def _mls_pass(conv_out, proj, lay, bias1, bias2, n_ctx, reverse, extra=None):
    t = proj.shape[0]
    nch, ncc = t // CHUNK, n_ctx // CHUNK
    d_mls = lay["d_mls"]
    heads = d_mls // MLS_V_DIM
    qw = heads * MLS_QK_DIM
    order = functools.partial(_chunk_order, n_chunks=nch, n_ctx_chunks=ncc, reverse=reverse)
    const = lambda i: (0, 0)
    n_io = d_mls // MLS_IO_BLOCK

    def io_blocks(col0):
        return [pl.BlockSpec((CHUNK, MLS_IO_BLOCK), functools.partial(lambda i, b: (order(i), b),
                                                                       b=col0 // MLS_IO_BLOCK + j))
                for j in range(n_io)]

    in_specs = [
        pl.BlockSpec((CHUNK, qw), lambda i: (order(i), lay["co_q"] // qw)),
        pl.BlockSpec((CHUNK, qw), lambda i: (order(i), lay["co_k"] // qw)),
        *io_blocks(lay["mv"]),
        pl.BlockSpec((CHUNK, SMALL_W), lambda i: (order(i), lay["sm1"] // SMALL_W)),
        pl.BlockSpec((CHUNK, SMALL_W), lambda i: (order(i), lay["sm2"] // SMALL_W)),
        pl.BlockSpec((1, SMALL_W), const),
        pl.BlockSpec((1, SMALL_W), const),
    ]
    args = [conv_out, conv_out] + [proj] * n_io + [proj, proj, bias1, bias2]
    finish = extra is not None
    if finish:
        h_prev, gain = extra
        in_specs += [
            *io_blocks(lay["mo"]),
            pl.BlockSpec((CHUNK, d_mls), lambda i: (order(i), 0)),
            pl.BlockSpec((1, d_mls), const),
        ]
        args += [proj] * n_io + [h_prev, gain]
    return pl.pallas_call(
        functools.partial(_mls_kernel, reverse=reverse, finish=finish, heads=heads, n_io=n_io),
        grid=(nch,),
        in_specs=in_specs,
        out_specs=pl.BlockSpec((CHUNK, d_mls), lambda i: (order(i), 0)),
        out_shape=jax.ShapeDtypeStruct((t, d_mls), BF16 if finish else F32),
        scratch_shapes=[pltpu.VMEM((heads, MLS_QK_DIM, MLS_V_DIM + LANE), F32),
                        pltpu.VMEM((1, SMALL_W), F32)],
        compiler_params=_cparams(("arbitrary",)),
        name="mlstm_bwd_finish" if finish else "mlstm_fwd",
    )(*args)


def _qk_prep_kernel(q_ref, k_ref, v_ref, cos_ref, sin_ref, gq_ref, gk_ref, qo_ref, ko_ref, vo_ref):
    cos = cos_ref[...]
    sin = sin_ref[...]

    even = lax.broadcasted_iota(jnp.int32, cos.shape, 1) % 2 == 0

    def norm_rope(xh, g, scale):
        ms = jnp.mean(xh * xh, axis=-1, keepdims=True)
        xn = xh * lax.rsqrt(ms + 1e-6) * g
        partner = jnp.where(even, pltpu.roll(xn, HEAD_DIM - 1, axis=1), pltpu.roll(xn, 1, axis=1))
        return (xn * cos + partner * sin) * scale

    q = q_ref[...]
    for h in range(q.shape[1] // HEAD_DIM):
        sl = slice(h * HEAD_DIM, (h + 1) * HEAD_DIM)
        qo_ref[:, sl] = norm_rope(q[:, sl], gq_ref[...], LOG2_E * HEAD_DIM ** -0.5).astype(qo_ref.dtype)
    k = k_ref[...]
    for h in range(k.shape[1] // HEAD_DIM):
        sl = slice(h * HEAD_DIM, (h + 1) * HEAD_DIM)
        ko_ref[:, sl] = norm_rope(k[:, sl], gk_ref[...], 1.0).astype(ko_ref.dtype)
    v = v_ref[...]
    ones = jnp.ones((v.shape[0], HEAD_DIM), vo_ref.dtype)
    for h in range(v.shape[1] // HEAD_DIM):
        vo_ref[:, 2 * h * HEAD_DIM:(2 * h + 1) * HEAD_DIM] = v[:, h * HEAD_DIM:(h + 1) * HEAD_DIM].astype(vo_ref.dtype)
        vo_ref[:, (2 * h + 1) * HEAD_DIM:(2 * h + 2) * HEAD_DIM] = ones


def _qk_prep(proj, lay, cos_t, sin_t, gq, gk):
    t = proj.shape[0]
    tm = _pick(t, (256, 128))
    d_att = lay["d_att"]
    kvw = ATT_KV_HEADS * HEAD_DIM
    const = lambda i: (0, 0)
    return pl.pallas_call(
        _qk_prep_kernel,
        grid=(t // tm,),
        in_specs=[
            pl.BlockSpec((tm, d_att), lambda i: (i, lay["aq"] // d_att)),
            pl.BlockSpec((tm, kvw), lambda i: (i, lay["ak"] // kvw)),
            pl.BlockSpec((tm, kvw), lambda i: (i, lay["av"] // kvw)),
            pl.BlockSpec((tm, HEAD_DIM), lambda i: (i, 0)),
            pl.BlockSpec((tm, HEAD_DIM), lambda i: (i, 0)),
            pl.BlockSpec((1, HEAD_DIM), const),
            pl.BlockSpec((1, HEAD_DIM), const),
        ],
        out_specs=[
            pl.BlockSpec((tm, d_att), lambda i: (i, 0)),
            pl.BlockSpec((tm, kvw), lambda i: (i, 0)),
            pl.BlockSpec((tm, 2 * kvw), lambda i: (i, 0)),
        ],
        out_shape=[jax.ShapeDtypeStruct((t, d_att), BF16),
                   jax.ShapeDtypeStruct((t, kvw), BF16),
                   jax.ShapeDtypeStruct((t, 2 * kvw), BF16)],
        compiler_params=_cparams(("arbitrary",)),
        name="qk_norm_rope",
    )(proj, proj, proj, cos_t, sin_t, gq, gk)


FLASH_ROW_BLOCK = 16
FLASH_KV_SPLIT = 256


def _flash_kernel(q_ref, k_ref, v_ref, o_ref, s_scr, p_scr, m_scr, a_scr, acc_scr, *, group, tq):
    j = pl.program_id(2)

    @pl.when(j == 0)
    def _():
        m_scr[...] = jnp.full_like(m_scr, NEG_BIG)
        acc_scr[...] = jnp.zeros_like(acc_scr)

    n_rows = group * tq
    n_split, _, tks = s_scr.shape
    rb = FLASH_ROW_BLOCK
    lane_tiles = [slice(t * LANE, (t + 1) * LANE) for t in range(tks // LANE)]
    for c in range(n_split):
        kc = k_ref[c * tks:(c + 1) * tks, :]
        for h in range(group):
            s_scr[c, h * tq:(h + 1) * tq, :] = lax.dot_general(
                q_ref[:, h * HEAD_DIM:(h + 1) * HEAD_DIM], kc, (((1,), (1,)), ((), ())),
                preferred_element_type=F32)
    for c in range(n_split):
        for b in range(n_rows // rb):
            rows = slice(b * rb, (b + 1) * rb)
            mx = s_scr[c, rows, lane_tiles[0]]
            for tile in lane_tiles[1:]:
                mx = jnp.maximum(mx, s_scr[c, rows, tile])
            m_old = m_scr[rows, :]
            m_new = jnp.maximum(m_old, jnp.broadcast_to(jnp.max(mx, axis=1, keepdims=True), (rb, LANE)))
            m_scr[rows, :] = m_new
            a_scr[rows, :] = jnp.exp2(m_old - m_new)
        for b in range(n_rows // rb):
            rows = slice(b * rb, (b + 1) * rb)
            m = m_scr[rows, :]
            for tile in lane_tiles:
                p_scr[c, rows, tile] = jnp.exp2(s_scr[c, rows, tile] - m).astype(BF16)
            alpha = a_scr[rows, :]
            acc_scr[rows, :] = acc_scr[rows, :] * jnp.concatenate([alpha, alpha], axis=1)
        acc_scr[...] += jnp.dot(p_scr[c], v_ref[c * tks:(c + 1) * tks, :], preferred_element_type=F32)

    @pl.when(j == pl.num_programs(2) - 1)
    def _():
        for h in range(group):
            rows = slice(h * tq, (h + 1) * tq)
            o_ref[:, h * HEAD_DIM:(h + 1) * HEAD_DIM] = (
                acc_scr[rows, :HEAD_DIM] / acc_scr[rows, HEAD_DIM:]).astype(o_ref.dtype)


def _flash(qn, kn, vb, n_k):
    n_q, d_att = qn.shape
    group = d_att // HEAD_DIM // ATT_KV_HEADS
    gw = group * HEAD_DIM
    tq = _pick(n_q, (512, 256, 128))
    tk = _pick(n_k, (768, 512, 256, 128))
    tks = min(tk, FLASH_KV_SPLIT)
    return pl.pallas_call(
        functools.partial(_flash_kernel, group=group, tq=tq),
        grid=(ATT_KV_HEADS, n_q // tq, n_k // tk),
        in_specs=[
            pl.BlockSpec((tq, gw), lambda g, i, j: (i, g)),
            pl.BlockSpec((tk, HEAD_DIM), lambda g, i, j: (j, g)),
            pl.BlockSpec((tk, 2 * HEAD_DIM), lambda g, i, j: (j, g)),
        ],
        out_specs=pl.BlockSpec((tq, gw), lambda g, i, j: (i, g)),
        out_shape=jax.ShapeDtypeStruct((n_q, d_att), BF16),
        scratch_shapes=[pltpu.VMEM((tk // tks, group * tq, tks), F32),
                        pltpu.VMEM((tk // tks, group * tq, tks), BF16),
                        pltpu.VMEM((group * tq, LANE), F32),
                        pltpu.VMEM((group * tq, LANE), F32),
                        pltpu.VMEM((group * tq, 2 * HEAD_DIM), F32)],
        compiler_params=_cparams(("arbitrary", "arbitrary", "arbitrary")),
        name="flash_gqa",
    )(qn, kn, vb)


def _outproj_kernel(a1_ref, a2_ref, a3_ref, w1_ref, w2_ref, w3_ref, o_ref):
    acc = jnp.dot(a1_ref[...], w1_ref[...], preferred_element_type=F32)
    acc = acc + jnp.dot(a2_ref[...], w2_ref[...], preferred_element_type=F32)
    acc = acc + jnp.dot(a3_ref[...], w3_ref[...], preferred_element_type=F32)
    o_ref[...] = acc


def _outproj(a1, a2, a3, w1, w2, w3):
    t = a1.shape[0]
    n = w1.shape[1]
    tm = _pick(t, (768, 512, 256, 128))
    tn = _pick(n, (512, 256, 128))
    lhs = lambda a: pl.BlockSpec((tm, a.shape[1]), lambda i, j: (i, 0))
    rhs = lambda w: pl.BlockSpec((w.shape[0], tn), lambda i, j: (0, j))
    return pl.pallas_call(
        _outproj_kernel,
        grid=(t // tm, n // tn),
        in_specs=[lhs(a1), lhs(a2), lhs(a3), rhs(w1), rhs(w2), rhs(w3)],
        out_specs=pl.BlockSpec((tm, tn), lambda i, j: (i, j)),
        out_shape=jax.ShapeDtypeStruct((t, n), F32),
        compiler_params=_cparams(("arbitrary", "arbitrary")),
        name="out_proj",
    )(a1, a2, a3, w1, w2, w3)


def _topk_route(logits, n_experts):
    lane = lax.broadcasted_iota(jnp.int32, logits.shape, 1)
    l = jnp.where(lane < n_experts, logits, NEG_BIG)
    tops = []
    for _ in range(TOP_K):
        m = jnp.max(l, axis=1, keepdims=True)
        idx = jnp.min(jnp.where(l == m, lane, LANE), axis=1, keepdims=True)
        tops.append((m, idx))
        l = jnp.where(lane == idx, NEG_BIG, l)
    es = [jnp.exp(m - tops[0][0]) for m, _ in tops]
    den = es[0]
    for e in es[1:]:
        den = den + e
    wts = jnp.zeros(logits.shape, F32)
    ids = jnp.zeros(logits.shape, jnp.int32)
    for k, ((m, idx), e) in enumerate(zip(tops, es)):
        wts = jnp.where(lane == k, e / den, wts)
        ids = jnp.where(lane == k, idx, ids)
    return wts, ids


def _ln_route_kernel(x_ref, y_ref, gate_ref, lng_ref, lnb_ref, sh_ref, sc_ref, wr_ref, br_ref,
                     xo_ref, tok_ref, wts_ref, ids_ref, *, n_ctx, tm, alpha, n_experts):
    rows = pl.program_id(0) * tm + lax.broadcasted_iota(jnp.int32, (tm, 1), 0)
    is_ctx = rows < n_ctx
    gate = jnp.where(is_ctx, gate_ref[1:2, :], gate_ref[0:1, :])
    u = alpha * x_ref[...] + gate * y_ref[...]
    mu = jnp.mean(u, axis=-1, keepdims=True)
    uc = u - mu
    var = jnp.mean(uc * uc, axis=-1, keepdims=True)
    xn = uc * lax.rsqrt(var + 1e-5) * lng_ref[...] + lnb_ref[...]
    xo_ref[...] = xn
    sc = jnp.where(is_ctx, sc_ref[1:2, :], sc_ref[0:1, :])
    sh = jnp.where(is_ctx, sh_ref[1:2, :], sh_ref[0:1, :])
    tok = xn * (1.0 + sc) + sh
    tok_ref[...] = _pack_bf16_pairs(tok)
    t_hi, t_lo = _split_bf16(tok, 2)
    w_hi, w_lo = _split_bf16(wr_ref[...], 2)
    logits = (jnp.dot(t_hi, w_hi, preferred_element_type=F32)
              + jnp.dot(t_hi, w_lo, preferred_element_type=F32)
              + jnp.dot(t_lo, w_hi, preferred_element_type=F32)) + br_ref[...]
    wts_ref[...], ids_ref[...] = _topk_route(logits, n_experts)


def _resid_ln_route(xall, y, mod, gate_blk, sh_blk, sc_blk, lng, lnb, w_router, b_router, n_experts, n_ctx, alpha):
    t, d = xall.shape
    tm = _pick(t, (256, 128))
    const = lambda i: (0, 0)
    row = pl.BlockSpec((tm, d), lambda i: (i, 0))
    lanes = pl.BlockSpec((tm, LANE), lambda i: (i, 0))
    mod_blk = lambda blk: pl.BlockSpec((8, d), lambda i: (0, blk))
    return pl.pallas_call(
        functools.partial(_ln_route_kernel, n_ctx=n_ctx, tm=tm, alpha=alpha, n_experts=n_experts),
        grid=(t // tm,),
        in_specs=[row, row, mod_blk(gate_blk), pl.BlockSpec((1, d), const), pl.BlockSpec((1, d), const),
                  mod_blk(sh_blk), mod_blk(sc_blk), pl.BlockSpec((d, LANE), const),
                  pl.BlockSpec((1, LANE), const)],
        out_specs=[row, pl.BlockSpec((tm, d // 2), lambda i: (i, 0)), lanes, lanes],
        out_shape=[jax.ShapeDtypeStruct((t, d), F32), jax.ShapeDtypeStruct((t, d // 2), jnp.uint32),
                   jax.ShapeDtypeStruct((t, LANE), F32), jax.ShapeDtypeStruct((t, LANE), jnp.int32)],
        compiler_params=_cparams(("arbitrary",)),
        name="resid_ln_route",
    )(xall, y, mod, lng, lnb, mod, mod, w_router, b_router)


MOE_ROW_TILE = 256
COMBINE_TILE = 128


def _route_plan(ids, n_experts, tm):
    t = ids.shape[0]
    n_rows = t * TOP_K
    flat = ids[:, :TOP_K].reshape(-1)
    onehot = (flat[:, None] == jnp.arange(n_experts, dtype=jnp.int32)[None, :]).astype(jnp.int32)
    csum = jnp.cumsum(onehot, axis=0)
    rank = jnp.sum((csum - onehot) * onehot, axis=1)
    cnt = csum[-1]
    ptiles = (cnt + tm - 1) // tm
    tile_end = jnp.cumsum(ptiles)
    row_start = (tile_end - ptiles) * tm
    pos = (jnp.sum(onehot * row_start[None, :], axis=1) + rank).astype(jnp.int32)
    n_tiles = (n_rows + n_experts * (tm - 1)) // tm
    row_token = jnp.zeros((n_tiles * tm,), jnp.int32).at[pos].set(
        jnp.arange(n_rows, dtype=jnp.int32) // TOP_K)
    tiles = jnp.arange(n_tiles, dtype=jnp.int32)
    tile_expert = jnp.minimum(jnp.sum((tiles[:, None] >= tile_end[None, :]).astype(jnp.int32), axis=1),
                              n_experts - 1).astype(jnp.int32)
    tile_valid = (tiles < tile_end[-1]).astype(jnp.int32)
    return pos, row_token, tile_expert, tile_valid, n_tiles


HI16 = 0xFFFF0000


def _pack_bf16_pairs(x):
    half = x.shape[1] // 2
    lo = pltpu.bitcast(x[:, :half].astype(BF16).astype(F32), jnp.uint32) >> 16
    hi = pltpu.bitcast(x[:, half:].astype(BF16).astype(F32), jnp.uint32) & jnp.uint32(HI16)
    return hi | lo


def _unpack_bf16_pairs(w):
    return pltpu.bitcast(w << 16, F32), pltpu.bitcast(w & jnp.uint32(HI16), F32)


def _moe_kernel(te_ref, tv_ref, rt_ref, tok_hbm, wgu_ref, bgu_ref, wdn_ref, bdn_ref, sel_ref, o_ref,
                buf, sem, *, tm, n_tiles):
    i = pl.program_id(0)

    def issue(tile, slot):
        def body(h, carry):
            for prio in range(2):
                r = 2 * h + prio
                row = rt_ref[tile * tm + r]
                pltpu.make_async_copy(tok_hbm.at[pl.ds(row, 1), :], buf.at[slot, pl.ds(r, 1), :],
                                      sem.at[slot]).start(priority=prio)
            return carry
        lax.fori_loop(0, tm // 2, body, 0, unroll=4)

    @pl.when(jnp.logical_and(i == 0, tv_ref[0] > 0))
    def _():
        issue(0, 0)

    nxt = jnp.minimum(i + 1, n_tiles - 1)

    @pl.when(jnp.logical_and(i + 1 < n_tiles, tv_ref[nxt] > 0))
    def _():
        issue(nxt, nxt % 2)

    slot = i % 2

    @pl.when(tv_ref[i] > 0)
    def _():
        pltpu.make_async_copy(tok_hbm.at[pl.ds(0, tm), :], buf.at[slot], sem.at[slot]).wait()
        x_lo, x_hi = _unpack_bf16_pairs(buf[slot])
        half = x_lo.shape[1]
        gu = (jnp.dot(x_lo.astype(BF16), wgu_ref[:half, :], preferred_element_type=F32)
              + jnp.dot(x_hi.astype(BF16), wgu_ref[half:, :], preferred_element_type=F32)) + bgu_ref[...]
        g = jnp.minimum(gu, SWIGLU_LIMIT)
        a = g * _sigmoid(SWIGLU_ALPHA * g)
        lin1 = jnp.clip(gu, -SWIGLU_LIMIT, SWIGLU_LIMIT) + 1.0
        pair = (a * pltpu.roll(lin1, gu.shape[1] - 1, axis=1)).astype(BF16)
        act = jnp.dot(pair, sel_ref[...], preferred_element_type=F32).astype(BF16)
        o_ref[...] = _pack_bf16_pairs(jnp.dot(act, wdn_ref[...], preferred_element_type=F32) + bdn_ref[...])

    @pl.when(tv_ref[i] == 0)
    def _():
        o_ref[...] = jnp.zeros_like(o_ref)


def _moe_experts(l, tok, plan, wgu, bgu, wdn, bdn):
    _, row_token, tile_expert, tile_valid, n_tiles = plan
    tm = MOE_ROW_TILE
    half = tok.shape[1]
    d = 2 * half
    two_de, de = wgu.shape[3], wdn.shape[2]
    sel = jnp.asarray(np.arange(two_de)[:, None] == 2 * np.arange(de)[None, :], BF16)
    grid_spec = pltpu.PrefetchScalarGridSpec(
        num_scalar_prefetch=3,
        grid=(n_tiles,),
        in_specs=[
            pl.BlockSpec(memory_space=pl.ANY),
            pl.BlockSpec((None, None, d, two_de), lambda i, te, tv, rt: (l, te[i], 0, 0)),
            pl.BlockSpec((None, None, 1, two_de), lambda i, te, tv, rt: (l, te[i], 0, 0)),
            pl.BlockSpec((None, None, de, d), lambda i, te, tv, rt: (l, te[i], 0, 0)),
            pl.BlockSpec((None, None, 1, d), lambda i, te, tv, rt: (l, te[i], 0, 0)),
            pl.BlockSpec((two_de, de), lambda i, te, tv, rt: (0, 0)),
        ],
        out_specs=pl.BlockSpec((tm, half), lambda i, te, tv, rt: (i, 0)),
        scratch_shapes=[pltpu.VMEM((2, tm, half), jnp.uint32), pltpu.SemaphoreType.DMA((2,))],
    )
    return pl.pallas_call(
        functools.partial(_moe_kernel, tm=tm, n_tiles=n_tiles),
        grid_spec=grid_spec,
        out_shape=jax.ShapeDtypeStruct((n_tiles * tm, half), jnp.uint32),
        compiler_params=_cparams(("arbitrary",)),
        name="moe_experts",
    )(tile_expert, tile_valid, row_token, tok, wgu, bgu, wdn, bdn, sel)


def _combine_ln_kernel(pos_ref, y_hbm, x_ref, wts_ref, gate_ref, lng_ref, lnb_ref, xo_ref, buf, sem,
                       *, n_ctx, tm, alpha, n_steps):
    i = pl.program_id(0)

    def issue(tile, slot):
        def body(r, carry):
            for k in range(TOP_K):
                row = pos_ref[(tile * tm + r) * TOP_K + k]
                pltpu.make_async_copy(y_hbm.at[pl.ds(row, 1), :], buf.at[slot, k, pl.ds(r, 1), :],
                                      sem.at[slot]).start(priority=k % 2)
            return carry
        lax.fori_loop(0, tm, body, 0, unroll=4)

    @pl.when(i == 0)
    def _():
        issue(0, 0)

    @pl.when(i + 1 < n_steps)
    def _():
        issue(i + 1, (i + 1) % 2)

    slot = i % 2
    for k in range(TOP_K):
        pltpu.make_async_copy(y_hbm.at[pl.ds(0, tm), :], buf.at[slot, k], sem.at[slot]).wait()
    wts = wts_ref[...]
    f_lo, f_hi = _unpack_bf16_pairs(buf[slot, 0])
    f_lo, f_hi = wts[:, 0:1] * f_lo, wts[:, 0:1] * f_hi
    for k in range(1, TOP_K):
        y_lo, y_hi = _unpack_bf16_pairs(buf[slot, k])
        f_lo = f_lo + wts[:, k:k + 1] * y_lo
        f_hi = f_hi + wts[:, k:k + 1] * y_hi
    f = jnp.concatenate([f_lo, f_hi], axis=1)
    rows = i * tm + lax.broadcasted_iota(jnp.int32, (tm, 1), 0)
    gate = jnp.where(rows < n_ctx, gate_ref[1:2, :], gate_ref[0:1, :])
    u = alpha * x_ref[...] + gate * f
    mu = jnp.mean(u, axis=-1, keepdims=True)
    uc = u - mu
    var = jnp.mean(uc * uc, axis=-1, keepdims=True)
    xo_ref[...] = uc * lax.rsqrt(var + 1e-5) * lng_ref[...] + lnb_ref[...]


def _combine_ln(x1, y_sorted, pos, wts, mod, gate_blk, lng, lnb, n_ctx, alpha):
    t, d = x1.shape
    tm = COMBINE_TILE
    n_steps = t // tm
    const = lambda i, p: (0, 0)
    grid_spec = pltpu.PrefetchScalarGridSpec(
        num_scalar_prefetch=1,
        grid=(n_steps,),
        in_specs=[
            pl.BlockSpec(memory_space=pl.ANY),
            pl.BlockSpec((tm, d), lambda i, p: (i, 0)),
            pl.BlockSpec((tm, LANE), lambda i, p: (i, 0)),
            pl.BlockSpec((8, d), lambda i, p: (0, gate_blk)),
            pl.BlockSpec((1, d), const),
            pl.BlockSpec((1, d), const),
        ],
        out_specs=pl.BlockSpec((tm, d), lambda i, p: (i, 0)),
        scratch_shapes=[pltpu.VMEM((2, TOP_K, tm, d // 2), jnp.uint32), pltpu.SemaphoreType.DMA((2,))],
    )
    return pl.pallas_call(
        functools.partial(_combine_ln_kernel, n_ctx=n_ctx, tm=tm, alpha=alpha, n_steps=n_steps),
        grid_spec=grid_spec,
        out_shape=jax.ShapeDtypeStruct((t, d), F32),
        compiler_params=_cparams(("arbitrary",)),
        name="moe_combine_ln",
    )(pos, y_sorted, x1, wts, mod, lng, lnb)


GATE_COL0 = 48
MLS_IO_BLOCK = 512


def _layout(d_model):
    d_ssd = 3 * d_model // 8
    d_att = d_model // 4
    d_mls = d_model - d_ssd - d_att
    ssd_heads = d_ssd // SSD_HEADDIM
    mls_heads = d_mls // MLS_V_DIM
    gs = SSD_GROUPS * SSD_STATE
    kvw = ATT_KV_HEADS * HEAD_DIM
    qkw = mls_heads * MLS_QK_DIM
    assert 2 * ssd_heads == GATE_COL0 and GATE_COL0 + 2 * mls_heads <= SMALL_W
    sizes = (d_ssd, d_ssd + 2 * gs, 2 * ssd_heads, d_att, kvw, kvw, 2 * qkw, d_mls, d_mls, 4 * mls_heads)
    o = np.concatenate([[0], np.cumsum(sizes)])
    src = dict(z=o[0], xbc=o[1], dt=o[2], aq=o[3], ak=o[4], av=o[5], mqk=o[6], mv=o[7], mo=o[8], mg=o[9])
    lay = dict(d_ssd=d_ssd, d_att=d_att, d_mls=d_mls, ssd_heads=ssd_heads, mls_heads=mls_heads,
               src={k: int(v) for k, v in src.items()}, gs=gs, kvw=kvw, qkw=qkw)
    cols = []

    def put(name, idx):
        lay[name] = len(cols)
        cols.extend(int(v) for v in idx)

    put("z", src["z"] + np.arange(d_ssd))
    put("xbc", src["xbc"] + np.arange(d_ssd + 2 * gs))
    put("aq", src["aq"] + np.arange(d_att))
    put("ak", src["ak"] + np.arange(kvw))
    put("av", src["av"] + np.arange(kvw))
    put("mqk", src["mqk"] + np.arange(2 * qkw))
    put("mv", src["mv"] + np.arange(d_mls))
    put("mo", src["mo"] + np.arange(d_mls))
    sm1 = -np.ones(SMALL_W, np.int64)
    sm1[:2 * ssd_heads] = src["dt"] + np.arange(2 * ssd_heads)
    sm2 = -np.ones(SMALL_W, np.int64)
    for direction in range(2):
        c = GATE_COL0 + direction * mls_heads
        sm1[c:c + mls_heads] = src["mg"] + (2 * direction + 1) * mls_heads + np.arange(mls_heads)
        sm2[c:c + mls_heads] = src["mg"] + (2 * direction) * mls_heads + np.arange(mls_heads)
    put("sm1", sm1)
    put("sm2", sm2)
    n_pad = -len(cols) % 512
    cols.extend([-1] * n_pad)
    lay["cols"] = np.asarray(cols, np.int64)
    lay["co_q"], lay["co_k"], lay["co_xs"] = 0, qkw, 2 * qkw
    lay["co_b"], lay["co_c"] = 2 * qkw + d_ssd, 2 * qkw + d_ssd + gs
    for name, width in (("z", d_ssd), ("mv", MLS_IO_BLOCK), ("mo", MLS_IO_BLOCK), ("aq", d_att), ("ak", kvw),
                        ("av", kvw), ("sm1", SMALL_W), ("sm2", SMALL_W), ("mqk", LANE), ("xbc", LANE)):
        assert lay[name] % width == 0, name
    assert d_mls % MLS_IO_BLOCK == 0
    assert lay["co_k"] % qkw == 0 and lay["co_xs"] % d_ssd == 0 and lay["co_b"] % gs == 0 and lay["co_c"] % gs == 0
    return lay


def _relayout_w_in(w, lay):
    d = w.shape[0]
    s = lay["src"]
    sh, mh = lay["ssd_heads"], lay["mls_heads"]

    def seg(name, width):
        return w[:, s[name]:s[name] + width]

    zeros = lambda n: jnp.zeros((d, n), w.dtype)
    mg = seg("mg", 4 * mh).reshape(d, 4, mh)
    tail = SMALL_W - GATE_COL0 - 2 * mh
    parts = [w[:, s["z"]:s["dt"]], w[:, s["aq"]:s["mg"]],
             seg("dt", 2 * sh), mg[:, 1], mg[:, 3], zeros(tail),
             zeros(GATE_COL0), mg[:, 0], mg[:, 2], zeros(tail)]
    parts.append(zeros(len(lay["cols"]) - sum(a.shape[1] for a in parts)))
    return jnp.concatenate(parts, axis=1).astype(BF16)


def _rope_tables(n_ctx, seq):
    rows = seq // GRID_W
    row = jnp.repeat(jnp.arange(rows), GRID_W).astype(F32)
    col = jnp.tile(jnp.arange(GRID_W), rows).astype(F32)
    n_freq = HEAD_DIM // 4
    inv = ROPE_THETA ** (-jnp.arange(n_freq, dtype=F32) / n_freq)
    ang = jnp.concatenate([row[:, None] * inv, col[:, None] * inv], axis=-1)
    cos, sin = jnp.cos(ang), jnp.sin(ang)
    cos_pairs = jnp.repeat(cos, 2, axis=-1)
    sin_pairs = jnp.stack([-sin, sin], axis=-1).reshape(seq, HEAD_DIM)
    cos_t = jnp.concatenate([jnp.ones((n_ctx, HEAD_DIM), F32), cos_pairs], axis=0)
    sin_t = jnp.concatenate([jnp.zeros((n_ctx, HEAD_DIM), F32), sin_pairs], axis=0)
    return cos_t, sin_t


def _pad_row(v, width=SMALL_W, at=0):
    out = jnp.zeros((1, width), F32)
    return out.at[0, at:at + v.shape[0]].set(v.astype(F32))


def _layer(l, xall, mod, lay, tables, n_ctx, alpha, p):
    d = xall.shape[1]
    proj = _inproj(xall, mod, _relayout_w_in(p["w_in"][l], lay), n_ctx)

    conv_w = jnp.concatenate([p["mls_conv_w"][l], p["ssd_conv_w"][l]], axis=1)
    conv_w8 = jnp.concatenate([conv_w, jnp.zeros((8 - CONV_W, conv_w.shape[1]), F32)], axis=0)
    conv_b = jnp.concatenate([p["mls_conv_b"][l], p["ssd_conv_b"][l]])[None, :]
    conv_out = _conv(proj, conv_w8, conv_b, (lay["mqk"], 2 * lay["qkw"]),
                     (lay["xbc"], lay["d_ssd"] + 2 * lay["gs"]), n_ctx)

    sh, mh = lay["ssd_heads"], lay["mls_heads"]
    gate_b = p["mls_gate_b"][l]
    bias1 = (_pad_row(p["ssd_dt_bias"][l].reshape(-1))
             + _pad_row(gate_b[1], at=GATE_COL0) + _pad_row(gate_b[3], at=GATE_COL0 + mh))
    bias2 = _pad_row(gate_b[0], at=GATE_COL0) + _pad_row(gate_b[2], at=GATE_COL0 + mh)
    alog_row = _pad_row(p["ssd_A_log"][l].reshape(-1))
    hcol = np.repeat(np.arange(sh), SSD_HEADDIM)
    e_f = jnp.asarray(np.arange(SMALL_W)[:, None] == hcol[None, :], BF16)
    e_b = jnp.asarray(np.arange(SMALL_W)[:, None] == (hcol + sh)[None, :], BF16)
    dskip_x = jnp.repeat(p["ssd_D"][l], SSD_HEADDIM)[None, :]
    y_f = _ssd_pass(conv_out, proj, lay, bias1, alog_row, e_f, n_ctx, False)
    ssd = _ssd_pass(conv_out, proj, lay, bias1, alog_row, e_b, n_ctx, True,
                    extra=(y_f, dskip_x, p["ssd_norm_g"][l][None, :]))

    h_f = _mls_pass(conv_out, proj, lay, bias1, bias2, n_ctx, False)
    mls = _mls_pass(conv_out, proj, lay, bias1, bias2, n_ctx, True, extra=(h_f, p["mls_norm_g"][l][None, :]))

    gq = p["att_q_norm_g"][l][None, :]
    gk = p["att_k_norm_g"][l][None, :]
    qn, kn, vb = _qk_prep(proj, lay, tables[0], tables[1], gq, gk)
    t = xall.shape[0]
    att_c = _flash(qn[:n_ctx], kn, vb, n_ctx)
    att_l = _flash(qn[n_ctx:], kn, vb, t)
    att = jnp.concatenate([att_c, att_l], axis=0)

    w_out = p["w_out"][l].astype(BF16)
    d_ssd, d_att = lay["d_ssd"], lay["d_att"]
    mix = _outproj(ssd, att, mls, w_out[:d_ssd], w_out[d_ssd:d_ssd + d_att], w_out[d_ssd + d_att:])

    ne = p["w_router"].shape[2]
    w_router = jnp.zeros((d, LANE), F32).at[:, :ne].set(p["w_router"][l])
    b_router = _pad_row(p["b_router"][l], LANE)
    x1, tok, wts, ids = _resid_ln_route(xall, mix, mod, 2, 3, 4, p["ln1_g"][l][None, :], p["ln1_b"][l][None, :],
                                        w_router, b_router, ne, n_ctx, alpha)

    plan = _route_plan(ids, ne, MOE_ROW_TILE)
    y_sorted = _moe_experts(l, tok, plan, p["w_gu_bf16"], p["b_gu"][:, :, None, :], p["w_dn_bf16"],
                            p["b_dn"][:, :, None, :])
    x2 = _combine_ln(x1, y_sorted, plan[0], wts, mod, 5, p["ln2_g"][l][None, :], p["ln2_b"][l][None, :],
                     n_ctx, alpha)
    aux = dict(proj=proj, conv_out=conv_out, ssd=ssd, mls=mls, att=att, mix=mix, x1=x1, tok=tok, wts=wts,
               ids=ids, y_sorted=y_sorted, pos=plan[0])
    return x2, aux


def kernel(x, c, ctx, c_ctx, w_ada, b_ada, w_in, ssd_conv_w, ssd_conv_b, ssd_A_log, ssd_dt_bias, ssd_D,
           ssd_norm_g, att_q_norm_g, att_k_norm_g, mls_conv_w, mls_conv_b, mls_gate_b, mls_norm_g, w_out,
           ln1_g, ln1_b, w_router, b_router, w_gu, b_gu, w_dn, b_dn, ln2_g, ln2_b):
    p = dict(w_in=w_in, ssd_conv_w=ssd_conv_w, ssd_conv_b=ssd_conv_b, ssd_A_log=ssd_A_log,
             ssd_dt_bias=ssd_dt_bias, ssd_D=ssd_D, ssd_norm_g=ssd_norm_g, att_q_norm_g=att_q_norm_g,
             att_k_norm_g=att_k_norm_g, mls_conv_w=mls_conv_w, mls_conv_b=mls_conv_b, mls_gate_b=mls_gate_b,
             mls_norm_g=mls_norm_g, w_out=w_out, ln1_g=ln1_g, ln1_b=ln1_b, w_router=w_router,
             b_router=b_router, w_gu=w_gu, b_gu=b_gu, w_dn=w_dn, b_dn=b_dn, ln2_g=ln2_g, ln2_b=ln2_b)
    p["w_gu_bf16"] = w_gu.astype(BF16)
    p["w_dn_bf16"] = w_dn.astype(BF16)
    batch, seq, d = x.shape
    assert batch == 1
    n_ctx = ctx.shape[1]
    depth = w_ada.shape[0]
    alpha = (2 * depth) ** 0.25
    lay = _layout(d)
    tables = _rope_tables(n_ctx, seq)
    cvec = jnp.zeros((8, d), F32).at[0].set(c[0]).at[1].set(c_ctx)
    mods = _ada(cvec, w_ada, b_ada)
    xall = jnp.concatenate([ctx[0], x[0]], axis=0)
    for l in range(depth):
        xall, _ = _layer(l, xall, mods[l], lay, tables, n_ctx, alpha, p)
    return xall[n_ctx:][None]
```
